```python
import jax, jax.numpy as jnp
from jax import lax
import numpy as np

D_MODEL = 2048
BATCH = 2
SEQ = 4096
DEPTH = 2

D_MIX = D_MODEL
RWKV_HEAD_DIM = 64
RWKV_WIDTH = 3 * D_MIX // 8
RWKV_HEADS = RWKV_WIDTH // RWKV_HEAD_DIM
DECAY_LORA = 64
AAA_LORA = 64
GATE_LORA = 128
FOX_HEAD_DIM = 128
FOX_WIDTH = 3 * D_MIX // 8
FOX_HEADS = FOX_WIDTH // FOX_HEAD_DIM
FOX_BLOCK = 128
SG_GROUP_DIM = 128
SG_WIDTH = D_MIX - RWKV_WIDTH - FOX_WIDTH
SG_GROUPS = SG_WIDTH // SG_GROUP_DIM
SG_CHUNK = 128

C_R = 0
C_K = C_R + RWKV_WIDTH
C_V = C_K + RWKV_WIDTH
C_WD = C_V + RWKV_WIDTH
C_AD = C_WD + DECAY_LORA
C_GD = C_AD + AAA_LORA
C_RWKV_END = C_GD + GATE_LORA
C_Q = C_RWKV_END
C_FK = C_Q + FOX_WIDTH
C_FV = C_FK + FOX_WIDTH
C_FF = C_FV + FOX_WIDTH
C_U = C_FF + FOX_HEADS
C_SV = C_U + SG_WIDTH
N_IN = C_SV + SG_WIDTH

D_FF = 11 * D_MODEL // 4
N_EXPERTS = 8
TOP_K = 2
D_FF_EXPERT = D_FF // 2
N_DENSE = (DEPTH + 1) // 2
N_MOE = DEPTH // 2

N_MOD = 6
RMS_EPS = 1e-6
LN_EPS = 1e-5
LNX_EPS = 64e-5

kernel_name = "hybrid_rwkv7_fox_sgu_moe_adaln"


def rms_norm(x, g, eps=RMS_EPS):
    xf = x.astype(jnp.float32)
    y = xf * lax.rsqrt(jnp.mean(xf * xf, axis=-1, keepdims=True) + eps)
    return (y * g.astype(jnp.float32)).astype(x.dtype)


def standardize(x, eps):
    xf = x.astype(jnp.float32)
    mu = jnp.mean(xf, axis=-1, keepdims=True)
    var = jnp.mean(jnp.square(xf - mu), axis=-1, keepdims=True)
    return (xf - mu) * lax.rsqrt(var + eps)


def token_shift(z, mu):
    z_prev = jnp.pad(z, ((0, 0), (1, 0), (0, 0)))[:, :-1]
    return z + (z_prev - z) * mu


def swiglu(x, w1, w3, w2):
    return (jax.nn.silu(x @ w1) * (x @ w3)) @ w2


def rwkv7_group(z, w0, w2, a0, a2, g2, k_k, k_a, r_k, lnx_g, lnx_b):
    B, S, _ = z.shape
    dt = z.dtype
    zf = z.astype(jnp.float32)
    r = zf[..., C_R:C_K]
    k = zf[..., C_K:C_V]
    v = zf[..., C_V:C_WD]
    wd = zf[..., C_WD:C_AD]
    ad = zf[..., C_AD:C_GD]
    gd = zf[..., C_GD:C_RWKV_END]
    w_log = -jax.nn.softplus(-(w0.astype(jnp.float32) + jnp.tanh(wd) @ w2.astype(jnp.float32))) - 0.5
    decay = jnp.exp(-jnp.exp(w_log))
    a = jax.nn.sigmoid(a0.astype(jnp.float32) + ad @ a2.astype(jnp.float32))
    g = jax.nn.sigmoid(gd) @ g2.astype(jnp.float32)
    kk = k * k_k.astype(jnp.float32)
    k = k * (1.0 + (a - 1.0) * k_a.astype(jnp.float32))

    hd = lambda t: t.reshape(B, S, RWKV_HEADS, RWKV_HEAD_DIM)
    r_h, w_h, k_h, v_h, a_h, kk_h = hd(r), hd(decay), hd(k), hd(v), hd(a), hd(kk)
    kk_h = kk_h / jnp.maximum(jnp.sqrt(jnp.sum(kk_h * kk_h, axis=-1, keepdims=True)), 1e-12)

    def step(state, inp):
        r_t, w_t, k_t, v_t, kk_t, a_t = inp
        sa = jnp.einsum('bhvk,bhk->bhv', state, kk_t)
        state = (state * w_t[:, :, None, :]
                 - sa[..., None] * (kk_t * a_t)[:, :, None, :]
                 + v_t[..., None] * k_t[:, :, None, :])
        y = jnp.einsum('bhvk,bhk->bhv', state, r_t)
        return state, y

    xs = tuple(jnp.moveaxis(t, 1, 0) for t in (r_h, w_h, k_h, v_h, kk_h, a_h))
    state0 = jnp.zeros((B, RWKV_HEADS, RWKV_HEAD_DIM, RWKV_HEAD_DIM), jnp.float32)
    _, ys = lax.scan(step, state0, xs)
    y = jnp.moveaxis(ys, 0, 1)
    y = standardize(y, LNX_EPS).reshape(B, S, RWKV_WIDTH) * lnx_g.astype(jnp.float32) + lnx_b.astype(jnp.float32)
    bonus = jnp.sum(r_h * k_h * r_k.astype(jnp.float32), axis=-1, keepdims=True) * v_h
    out = (y + bonus.reshape(B, S, RWKV_WIDTH)) * g
    return out.astype(dt)


def fox_group(hz, qn_g, kn_g, fb):
    B, S, _ = hz.shape
    hds = lambda t: jnp.transpose(t.reshape(B, S, FOX_HEADS, FOX_HEAD_DIM), (0, 2, 1, 3))
    q = hds(rms_norm(hz[..., C_Q:C_FK].reshape(B, S, FOX_HEADS, FOX_HEAD_DIM), qn_g))
    k = hds(rms_norm(hz[..., C_FK:C_FV].reshape(B, S, FOX_HEADS, FOX_HEAD_DIM), kn_g))
    v = hds(hz[..., C_FV:C_FF])
    logf = jax.nn.log_sigmoid((hz[..., C_FF:C_U] + fb).astype(jnp.float32))
    cum = jnp.transpose(jnp.cumsum(logf, axis=1), (0, 2, 1))
    scale = FOX_HEAD_DIM ** -0.5
    outs = []
    for i in range(S // FOX_BLOCK):
        q0, q1 = i * FOX_BLOCK, (i + 1) * FOX_BLOCK
        qb = q[:, :, q0:q1]
        kb = k[:, :, :q1]
        vb = v[:, :, :q1]
        logits = (jnp.einsum('bhqd,bhkd->bhqk', qb, kb).astype(jnp.float32) * scale
                  + cum[:, :, q0:q1, None] - cum[:, :, None, :q1])
        causal = (q0 + jnp.arange(FOX_BLOCK))[:, None] >= jnp.arange(q1)[None, :]
        p = jax.nn.softmax(jnp.where(causal, logits, -jnp.inf), axis=-1)
        outs.append(jnp.einsum('bhqk,bhkd->bhqd', p.astype(vb.dtype), vb))
    o = jnp.concatenate(outs, axis=2)
    return jnp.transpose(o, (0, 2, 1, 3)).reshape(B, S, FOX_WIDTH)


def sgu_group(hz, ln_g, ln_b, ws, sb):
    B, S, _ = hz.shape
    u = jax.nn.gelu(hz[..., C_U:C_SV]).reshape(B, S, SG_GROUPS, SG_GROUP_DIM)
    v = jax.nn.gelu(hz[..., C_SV:N_IN]).reshape(B, S, SG_GROUPS, SG_GROUP_DIM)
    v = (standardize(v, LN_EPS) * ln_g.astype(jnp.float32) + ln_b.astype(jnp.float32)).astype(hz.dtype)
    v = v.reshape(B, S // SG_CHUNK, SG_CHUNK, SG_GROUPS, SG_GROUP_DIM)
    ws_c = ws * jnp.tril(jnp.ones((SG_CHUNK, SG_CHUNK), dtype=bool))
    mixed = jnp.einsum('gpq,bnqgc->bnpgc', ws_c, v) + jnp.transpose(sb)[None, None, :, :, None]
    return (u * mixed.reshape(B, S, SG_GROUPS, SG_GROUP_DIM)).reshape(B, S, SG_WIDTH)


def moe_ffn(h, router_w, router_b, w1, w3, w2):
    B, S, D = h.shape
    xf = h.reshape(B * S, D)
    logits = (xf @ router_w + router_b).astype(jnp.float32)
    top_v, top_i = lax.top_k(logits, TOP_K)
    top_w = jax.nn.softmax(top_v, axis=-1)
    gates = jnp.sum(jax.nn.one_hot(top_i, N_EXPERTS, dtype=jnp.float32) * top_w[..., None], axis=1)
    y = jnp.zeros_like(xf)
    for e in range(N_EXPERTS):
        y = y + gates[:, e:e + 1].astype(xf.dtype) * swiglu(xf, w1[e], w3[e], w2[e])
    return y.reshape(B, S, D)


def setup_inputs(seed: int = 0) -> dict:
    key = jax.random.key(seed)
    keys = iter(jax.random.split(key, 48))
    L = DEPTH
    D = D_MODEL
    nrm = lambda shape, s: jax.random.normal(next(keys), shape, jnp.float32) * s
    uni = lambda shape, lo, hi: jax.random.uniform(next(keys), shape, jnp.float32, lo, hi)
    return {
        "x": nrm((BATCH, SEQ, D), 1.0),
        "c": nrm((BATCH, D), 1.0),
        "ada_w": nrm((L, D, N_MOD * D), 0.5 * D ** -0.5),
        "ada_b": nrm((L, N_MOD * D), 0.02),
        "norm1_g": 1.0 + nrm((L, D), 0.05),
        "norm2_g": 1.0 + nrm((L, D), 0.05),
        "w_in": nrm((L, D, N_IN), D ** -0.5),
        "shift_mu": uni((L, C_RWKV_END), 0.0, 1.0),
        "rw_w0": uni((L, RWKV_WIDTH), -6.0, -1.0),
        "rw_w2": nrm((L, DECAY_LORA, RWKV_WIDTH), 0.5 * DECAY_LORA ** -0.5),
        "rw_a0": nrm((L, RWKV_WIDTH), 0.5),
        "rw_a2": nrm((L, AAA_LORA, RWKV_WIDTH), 0.5 * AAA_LORA ** -0.5),
        "rw_g2": nrm((L, GATE_LORA, RWKV_WIDTH), GATE_LORA ** -0.5),
        "rw_k_k": 0.85 + nrm((L, RWKV_WIDTH), 0.05),
        "rw_k_a": 1.0 + nrm((L, RWKV_WIDTH), 0.05),
        "rw_r_k": nrm((L, RWKV_HEADS, RWKV_HEAD_DIM), 0.1),
        "rw_lnx_g": 1.0 + nrm((L, RWKV_WIDTH), 0.05),
        "rw_lnx_b": nrm((L, RWKV_WIDTH), 0.02),
        "fox_qn_g": 1.0 + nrm((L, FOX_HEAD_DIM), 0.05),
        "fox_kn_g": 1.0 + nrm((L, FOX_HEAD_DIM), 0.05),
        "fox_fb": 2.0 + nrm((L, FOX_HEADS), 0.5),
        "sg_ln_g": 1.0 + nrm((L, SG_GROUPS, SG_GROUP_DIM), 0.05),
        "sg_ln_b": nrm((L, SG_GROUPS, SG_GROUP_DIM), 0.02),
        "sg_ws": nrm((L, SG_GROUPS, SG_CHUNK, SG_CHUNK), SG_CHUNK ** -0.5),
        "sg_b": 1.0 + nrm((L, SG_GROUPS, SG_CHUNK), 0.1),
        "w_out": nrm((L, D_MIX, D), D_MIX ** -0.5),
        "ffn_w1": nrm((N_DENSE, D, D_FF), D ** -0.5),
        "ffn_w3": nrm((N_DENSE, D, D_FF), D ** -0.5),
        "ffn_w2": nrm((N_DENSE, D_FF, D), D_FF ** -0.5),
        "moe_router_w": nrm((N_MOE, D, N_EXPERTS), D ** -0.5),
        "moe_router_b": nrm((N_MOE, N_EXPERTS), 0.01),
        "moe_w1": nrm((N_MOE, N_EXPERTS, D, D_FF_EXPERT), D ** -0.5),
        "moe_w3": nrm((N_MOE, N_EXPERTS, D, D_FF_EXPERT), D ** -0.5),
        "moe_w2": nrm((N_MOE, N_EXPERTS, D_FF_EXPERT, D), D_FF_EXPERT ** -0.5),
    }


def reference(x, c, ada_w, ada_b, norm1_g, norm2_g, w_in, shift_mu, rw_w0, rw_w2, rw_a0, rw_a2,
              rw_g2, rw_k_k, rw_k_a, rw_r_k, rw_lnx_g, rw_lnx_b, fox_qn_g, fox_kn_g, fox_fb,
              sg_ln_g, sg_ln_b, sg_ws, sg_b, w_out, ffn_w1, ffn_w3, ffn_w2, moe_router_w,
              moe_router_b, moe_w1, moe_w3, moe_w2):
    cs = jax.nn.silu(c)
    for l in range(DEPTH):
        mod = (cs @ ada_w[l] + ada_b[l])[:, None, :]
        sh1, sc1, g1, sh2, sc2, g2 = jnp.split(mod, N_MOD, axis=-1)

        h = rms_norm(x, norm1_g[l]) * (1.0 + sc1) + sh1
        hz = h @ w_in[l]
        z_rwkv = token_shift(hz[..., :C_RWKV_END], shift_mu[l])
        o_a = rwkv7_group(z_rwkv, rw_w0[l], rw_w2[l], rw_a0[l], rw_a2[l], rw_g2[l], rw_k_k[l],
                          rw_k_a[l], rw_r_k[l], rw_lnx_g[l], rw_lnx_b[l])
        o_b = fox_group(hz, fox_qn_g[l], fox_kn_g[l], fox_fb[l])
        o_c = sgu_group(hz, sg_ln_g[l], sg_ln_b[l], sg_ws[l], sg_b[l])
        mix = jnp.concatenate([o_a, o_b, o_c], axis=-1) @ w_out[l]
        x = x + g1 * mix

        h2 = rms_norm(x, norm2_g[l]) * (1.0 + sc2) + sh2
        j = l // 2
        if l % 2 == 0:
            f = swiglu(h2, ffn_w1[j], ffn_w3[j], ffn_w2[j])
        else:
            f = moe_ffn(h2, moe_router_w[j], moe_router_b[j], moe_w1[j], moe_w3[j], moe_w2[j])
        x = x + g2 * f
    return x
```

```python
import functools

import jax
import jax.numpy as jnp
from jax import lax
from jax.experimental import pallas as pl
from jax.experimental.pallas import tpu as pltpu

F32 = jnp.float32
BF16 = jnp.bfloat16
HIGHEST = lax.Precision.HIGHEST

D_MODEL = 2048
DEPTH = 2
RW_HD = 64
RW_W = 768
RW_H = 12
DECAY_LORA = 64
AAA_LORA = 64
GATE_LORA = 128
FOX_HD = 128
FOX_W = 768
FOX_H = 6
SG_GD = 128
SG_W = 512
SG_G = 4
SG_CHUNK = 128
N_EXPERTS = 8
N_MOD = 6
RMS_EPS = 1e-6
LN_EPS = 1e-5
LNX_EPS = 64e-5

C_R = 0
C_RWKV_END = 3 * RW_W + DECAY_LORA + AAA_LORA + GATE_LORA
C_Q = C_RWKV_END
C_FF = C_Q + 3 * FOX_W
C_U = C_FF + FOX_H
C_SV = C_U + SG_W
N_IN = C_SV + SG_W

P_R, P_K, P_V = 0, 768, 1536
P_Q, P_FK, P_FV = 2304, 3072, 3840
P_U = 4608
P_SV = 5120
P_LORA = 5632
P_FF = 5888
N_PACK = 6144

LANE = 128
VMEM_LIMIT = 56 * 1024 * 1024


def _cparams(sem):
    return pltpu.CompilerParams(dimension_semantics=sem, vmem_limit_bytes=VMEM_LIMIT)


def _sigmoid(x):
    return 1.0 / (1.0 + jnp.exp(-x))


def _softplus(x):
    return jnp.maximum(x, 0.0) + jnp.log(1.0 + jnp.exp(-jnp.abs(x)))


def _gelu_tanh(x):
    return 0.5 * x * (1.0 + jnp.tanh(0.7978845608028654 * (x + 0.044715 * (x * x * x))))


def _silu(x):
    return x * _sigmoid(x)


def _ada_kernel(c_ref, w_ref, b_ref, o_ref):
    cs = _silu(c_ref[...]).astype(BF16)
    w = w_ref[0].astype(BF16)
    o_ref[0] = jnp.dot(cs, w, preferred_element_type=F32) + b_ref[0]


def _ada_mod(c8, ada_w, ada_b3):
    L, D, N = ada_w.shape
    tn = 1536
    return pl.pallas_call(
        _ada_kernel, name="ada_mod",
        out_shape=jax.ShapeDtypeStruct((L, 8, N), F32),
        grid=(L, N // tn),
        in_specs=[
            pl.BlockSpec((8, D), lambda l, j: (0, 0)),
            pl.BlockSpec((1, D, tn), lambda l, j: (l, 0, j)),
            pl.BlockSpec((1, 1, tn), lambda l, j: (l, 0, j)),
        ],
        out_specs=pl.BlockSpec((1, 8, tn), lambda l, j: (l, 0, j)),
        compiler_params=_cparams(("parallel", "parallel")),
    )(c8, ada_w, ada_b3)


def _inproj_kernel(x_ref, g_ref, sc_ref, sh_ref, w_ref, o_ref, h_scr):
    @pl.when(pl.program_id(1) == 0)
    def _():
        x = x_ref[...]
        ms = jnp.mean(x * x, axis=-1, keepdims=True)
        y = x * lax.rsqrt(ms + RMS_EPS) * g_ref[...]
        h_scr[...] = (y * (1.0 + sc_ref[0]) + sh_ref[0]).astype(BF16)

    o_ref[...] = jnp.dot(h_scr[...], w_ref[...], preferred_element_type=F32)


def _inproj(x2, g, sc, sh, w, seq):
    M, D = x2.shape
    N = w.shape[1]
    tm, tn = 1024, 512
    return pl.pallas_call(
        _inproj_kernel, name="inproj",
        out_shape=jax.ShapeDtypeStruct((M, N), F32),
        grid=(M // tm, N // tn),
        in_specs=[
            pl.BlockSpec((tm, D), lambda i, j: (i, 0)),
            pl.BlockSpec((1, D), lambda i, j: (0, 0)),
            pl.BlockSpec((1, 1, D), lambda i, j: (i * tm // seq, 0, 0)),
            pl.BlockSpec((1, 1, D), lambda i, j: (i * tm // seq, 0, 0)),
            pl.BlockSpec((D, tn), lambda i, j: (0, j)),
        ],
        out_specs=pl.BlockSpec((tm, tn), lambda i, j: (i, j)),
        scratch_shapes=[pltpu.VMEM((tm, D), BF16)],
        compiler_params=_cparams(("parallel", "arbitrary")),
    )(x2, g, sc, sh, w)


def _rwkv_prep_kernel(zr_ref, zk_ref, zv_ref, zl_ref, pr_ref, pk_ref, pv_ref, pl_ref,
                      mu_ref, mul_ref, w0_ref, w2_ref, a0_ref, a2_ref, g2_ref, kk_ref, ka_ref,
                      r_out, lw_out, k_out, v_out, kk_out, a_out, g_out, *, tm, seq):
    first = (pl.program_id(0) * tm) % seq == 0

    def shift(z_ref, p_ref, mu):
        z = z_ref[...]
        prev_last = jnp.where(first, 0.0, p_ref[7:8, :])
        zp = pltpu.roll(z, 1, 0)
        row = lax.broadcasted_iota(jnp.int32, z.shape, 0)
        zp = jnp.where(row == 0, prev_last, zp)
        return z + (zp - z) * mu

    r = shift(zr_ref, pr_ref, mu_ref[0:1, :])
    k = shift(zk_ref, pk_ref, mu_ref[1:2, :])
    v = shift(zv_ref, pv_ref, mu_ref[2:3, :])
    lo = shift(zl_ref, pl_ref, mul_ref[...])
    wd = lo[:, 0:DECAY_LORA]
    ad = lo[:, DECAY_LORA:DECAY_LORA + AAA_LORA]
    gd = lo[:, DECAY_LORA + AAA_LORA:]

    dec = w0_ref[...] + jnp.dot(jnp.tanh(wd).astype(BF16), w2_ref[...], preferred_element_type=F32)
    w_log = -_softplus(-dec) - 0.5
    lw = -jnp.exp(w_log)
    a = _sigmoid(a0_ref[...] + jnp.dot(ad.astype(BF16), a2_ref[...], preferred_element_type=F32))
    g = jnp.dot(_sigmoid(gd).astype(BF16), g2_ref[...], preferred_element_type=F32)
    kk = k * kk_ref[...]
    km = k * (1.0 + (a - 1.0) * ka_ref[...])

    for h in range(RW_H):
        sl = slice(h * RW_HD, (h + 1) * RW_HD)
        r_out[h] = r[:, sl]
        lw_out[h] = lw[:, sl]
        k_out[h] = km[:, sl]
        v_out[h] = v[:, sl]
        kk_out[h] = kk[:, sl]
        a_out[h] = a[:, sl]
        g_out[h] = g[:, sl]


def _rwkv_prep(hz, mu3, mul, w0, w2, a0, a2, g2, k_k, k_a, seq):
    M = hz.shape[0]
    tm = 256
    prev = lambda i: jnp.maximum(i * (tm // 8) - 1, 0)
    row = lambda n: pl.BlockSpec((1, n), lambda i: (0, 0))
    in_specs = [
        pl.BlockSpec((tm, RW_W), lambda i: (i, P_R // RW_W)),
        pl.BlockSpec((tm, RW_W), lambda i: (i, P_K // RW_W)),
        pl.BlockSpec((tm, RW_W), lambda i: (i, P_V // RW_W)),
        pl.BlockSpec((tm, 256), lambda i: (i, P_LORA // 256)),
        pl.BlockSpec((8, RW_W), lambda i: (prev(i), P_R // RW_W)),
        pl.BlockSpec((8, RW_W), lambda i: (prev(i), P_K // RW_W)),
        pl.BlockSpec((8, RW_W), lambda i: (prev(i), P_V // RW_W)),
        pl.BlockSpec((8, 256), lambda i: (prev(i), P_LORA // 256)),
        pl.BlockSpec((3, RW_W), lambda i: (0, 0)),
        row(256), row(RW_W),
        pl.BlockSpec((DECAY_LORA, RW_W), lambda i: (0, 0)),
        row(RW_W),
        pl.BlockSpec((AAA_LORA, RW_W), lambda i: (0, 0)),
        pl.BlockSpec((GATE_LORA, RW_W), lambda i: (0, 0)),
        row(RW_W), row(RW_W),
    ]
    hm = jax.ShapeDtypeStruct((RW_H, M, RW_HD), F32)
    hm_spec = pl.BlockSpec((RW_H, tm, RW_HD), lambda i: (0, i, 0))
    return pl.pallas_call(
        functools.partial(_rwkv_prep_kernel, tm=tm, seq=seq), name="rwkv_prep",
        out_shape=[hm] * 7,
        grid=(M // tm,),
        in_specs=in_specs,
        out_specs=[hm_spec] * 7,
        compiler_params=_cparams(("parallel",)),
    )(hz, hz, hz, hz, hz, hz, hz, hz, mu3, mul, w0, w2, a0, a2, g2, k_k, k_a)


def _bmm(a, b, ca, cb, prec=HIGHEST):
    return lax.dot_general(a, b, (((ca,), (cb,)), ((0,), (0,))), precision=prec,
                           preferred_element_type=F32)


def _rwkv_scan_kernel(r_ref, lw_ref, k_ref, v_ref, kk_ref, a_ref, g_ref, rk_ref, lng_ref, lnb_ref,
                      o_ref, s_scr, *, L, nC):
    H, K = RW_H, RW_HD
    n = H * nC

    @pl.when(pl.program_id(1) == 0)
    def _():
        s_scr[...] = jnp.zeros_like(s_scr)

    ld = lambda ref: ref[...].reshape(n, L, K)
    r, lw, k, v, kk, a = ld(r_ref), ld(lw_ref), ld(k_ref), ld(v_ref), ld(kk_ref), ld(a_ref)

    kk = kk / jnp.maximum(jnp.sqrt(jnp.sum(kk * kk, axis=-1, keepdims=True)), 1e-12)
    b = kk * a

    ti = lax.broadcasted_iota(jnp.int32, (L, L), 0)
    si = lax.broadcasted_iota(jnp.int32, (L, L), 1)
    tri_incl = jnp.broadcast_to((ti >= si).astype(F32), (n, L, L))
    cw = _bmm(tri_incl, lw, 2, 1)
    cw_last = cw[:, L - 1:L, :]
    e_in = jnp.exp(cw)
    e_out = jnp.exp(-cw)
    e_ex = jnp.exp(cw - lw)
    e_end = jnp.exp(cw_last - cw)
    w_end = jnp.exp(cw_last)

    at = -kk * e_ex
    rt = r * e_in
    bb = b * e_out
    kb = k * e_out
    bh = b * e_end
    kh = k * e_end

    strict = (ti > si)[None]
    incl = (ti >= si)[None]
    a_ab = jnp.where(strict, _bmm(at, bb, 2, 2), 0.0)
    a_ak = jnp.where(strict, _bmm(at, kb, 2, 2), 0.0)
    b_rb = jnp.where(incl, _bmm(rt, bb, 2, 2), 0.0)
    b_rk = jnp.where(incl, _bmm(rt, kb, 2, 2), 0.0)

    x = jnp.broadcast_to((ti == si).astype(F32), (n, L, L))
    size = 1
    while size < L:
        sh = size.bit_length() - 1
        m = (((ti >> (sh + 1)) == (si >> (sh + 1))) & (((ti >> sh) & 1) == 1) & (((si >> sh) & 1) == 0))[None]
        x = x + _bmm(_bmm(x, jnp.where(m, a_ab, 0.0), 2, 1), x, 2, 1)
        size *= 2

    ap = _bmm(x, at, 2, 1)
    u0 = _bmm(x, _bmm(a_ak, v, 2, 1), 2, 1)
    rp = rt + _bmm(b_rb, ap, 2, 1)
    y0 = _bmm(b_rb, u0, 2, 1) + _bmm(b_rk, v, 2, 1)
    gp = _bmm(ap, bh, 1, 1)
    cc = _bmm(u0, bh, 1, 1) + _bmm(v, kh, 1, 1)

    c4 = lambda t: t.reshape((H, nC) + t.shape[1:])
    rp, y0, gp, cc, w_end = c4(rp), c4(y0), c4(gp), c4(cc), c4(w_end)
    s = s_scr[...]
    ys = []
    for c in range(nC):
        ys.append(_bmm(rp[:, c], s, 2, 2) + y0[:, c])
        s = s * w_end[:, c] + _bmm(s, gp[:, c], 2, 1) + cc[:, c]
    s_scr[...] = s
    y = ys[0] if nC == 1 else jnp.concatenate(ys, axis=1)

    mu = jnp.mean(y, axis=-1, keepdims=True)
    yc = y - mu
    var = jnp.mean(yc * yc, axis=-1, keepdims=True)
    yn = yc * lax.rsqrt(var + LNX_EPS) * lng_ref[...] + lnb_ref[...]
    r3, k3, v3 = r_ref[...], k_ref[...], v_ref[...]
    bonus = jnp.sum(r3 * k3 * rk_ref[...], axis=-1, keepdims=True) * v3
    out = (yn + bonus) * g_ref[...]
    for p in range(H // 2):
        pair = jnp.concatenate([out[2 * p], out[2 * p + 1]], axis=-1)
        o_ref[:, p * LANE:(p + 1) * LANE] = pair.astype(o_ref.dtype)


def _rwkv_scan(r, lw, k, v, kk, a, g, r_k, lnx_g, lnx_b, batch, seq):
    M = r.shape[1]
    L, nC = 64, 2
    tb = L * nC
    nj = seq // tb
    hm_spec = pl.BlockSpec((RW_H, tb, RW_HD), lambda b, j: (0, b * nj + j, 0))
    par = pl.BlockSpec((RW_H, 1, RW_HD), lambda b, j: (0, 0, 0))
    return pl.pallas_call(
        functools.partial(_rwkv_scan_kernel, L=L, nC=nC), name="rwkv_scan",
        out_shape=jax.ShapeDtypeStruct((M, RW_W), BF16),
        grid=(batch, nj),
        in_specs=[hm_spec] * 7 + [par] * 3,
        out_specs=pl.BlockSpec((tb, RW_W), lambda b, j: (b * nj + j, 0)),
        scratch_shapes=[pltpu.VMEM((RW_H, RW_HD, RW_HD), F32)],
        compiler_params=_cparams(("parallel", "arbitrary")),
    )(r, lw, k, v, kk, a, g, r_k, lnx_g, lnx_b)


def _fox_prep_kernel(q_ref, k_ref, v_ref, f_ref, qg_ref, kg_ref, fb_ref,
                     qo_ref, ko_ref, vo_ref, cum_ref, carry, *, tm, seq):
    @pl.when((pl.program_id(0) * tm) % seq == 0)
    def _():
        carry[...] = jnp.zeros_like(carry)

    scale = FOX_HD ** -0.5
    for h in range(FOX_H):
        sl = slice(h * FOX_HD, (h + 1) * FOX_HD)
        q = q_ref[:, sl]
        k = k_ref[:, sl]
        qn = q * lax.rsqrt(jnp.mean(q * q, axis=-1, keepdims=True) + RMS_EPS) * qg_ref[...]
        kn = k * lax.rsqrt(jnp.mean(k * k, axis=-1, keepdims=True) + RMS_EPS) * kg_ref[...]
        qo_ref[:, sl] = (qn * scale).astype(BF16)
        ko_ref[:, sl] = kn.astype(BF16)
    vo_ref[...] = v_ref[...].astype(BF16)

    logf = -_softplus(-(f_ref[...] + fb_ref[...]))
    ti = lax.broadcasted_iota(jnp.int32, (tm, tm), 0)
    si = lax.broadcasted_iota(jnp.int32, (tm, tm), 1)
    tri = (ti >= si).astype(F32)
    cum = jnp.dot(tri, logf, precision=HIGHEST, preferred_element_type=F32) + carry[...]
    cum_ref[...] = cum
    carry[...] = cum[tm - 1:tm, :]


def _fox_prep(hz, qn_g, kn_g, fb128, seq):
    M = hz.shape[0]
    tm = 256
    wide = lambda c: pl.BlockSpec((tm, FOX_W), lambda i: (i, c))
    row = pl.BlockSpec((1, LANE), lambda i: (0, 0))
    o = jax.ShapeDtypeStruct((M, FOX_W), BF16)
    ospec = pl.BlockSpec((tm, FOX_W), lambda i: (i, 0))
    return pl.pallas_call(
        functools.partial(_fox_prep_kernel, tm=tm, seq=seq), name="fox_prep",
        out_shape=[o, o, o, jax.ShapeDtypeStruct((M, LANE), F32)],
        grid=(M // tm,),
        in_specs=[
            _col_spec(tm, FOX_W, P_Q), _col_spec(tm, FOX_W, P_FK), _col_spec(tm, FOX_W, P_FV),
            _col_spec(tm, LANE, P_FF), row, row, row,
        ],
        out_specs=[ospec, ospec, ospec, pl.BlockSpec((tm, LANE), lambda i: (i, 0))],
        scratch_shapes=[pltpu.VMEM((1, LANE), F32)],
        compiler_params=_cparams(("arbitrary",)),
    )(hz, hz, hz, hz, qn_g, kn_g, fb128)


def _col_spec(tm, width, start):
    assert start % width == 0
    return pl.BlockSpec((tm, width), lambda i, *_: (i, start // width))


def _fox_attn_kernel(q_ref, k_ref, v_ref, cq_ref, ck_ref, o_ref, m_scr, l_scr, acc_scr, *, tq, tk):
    qi = pl.program_id(1)
    ki = pl.program_id(2)

    @pl.when(ki == 0)
    def _():
        m_scr[...] = jnp.full_like(m_scr, -1e30)
        l_scr[...] = jnp.zeros_like(l_scr)
        acc_scr[...] = jnp.zeros_like(acc_scr)

    @pl.when(ki * tk < (qi + 1) * tq)
    def _():
        s = lax.dot_general(q_ref[...], k_ref[...], (((1,), (1,)), ((), ())),
                            preferred_element_type=F32)
        s = s + cq_ref[0] - ck_ref[0]
        qpos = qi * tq + lax.broadcasted_iota(jnp.int32, (tq, tk), 0)
        kpos = ki * tk + lax.broadcasted_iota(jnp.int32, (tq, tk), 1)
        s = jnp.where(qpos >= kpos, s, -1e30)
        m_prev = m_scr[...]
        m_new = jnp.maximum(m_prev, jnp.max(s, axis=-1, keepdims=True))
        alpha = jnp.exp(m_prev - m_new)
        p = jnp.exp(s - m_new)
        l_scr[...] = alpha * l_scr[...] + jnp.sum(p, axis=-1, keepdims=True)
        acc_scr[...] = alpha * acc_scr[...] + jnp.dot(p.astype(BF16), v_ref[...],
                                                      preferred_element_type=F32)
        m_scr[...] = m_new

    @pl.when(ki == pl.num_programs(2) - 1)
    def _():
        o_ref[...] = (acc_scr[...] / l_scr[...]).astype(o_ref.dtype)


def _fox_attn(qn, kn, vb, cum_col, cum_row, batch, seq):
    M = qn.shape[0]
    tq = tk = 512
    nq, nk = seq // tq, seq // tk
    last_k = lambda qi, ki: jnp.minimum(ki, ((qi + 1) * tq - 1) // tk)
    bh = lambda g: (g // FOX_H, g % FOX_H)
    q_spec = pl.BlockSpec((tq, FOX_HD), lambda g, qi, ki: (bh(g)[0] * nq + qi, bh(g)[1]))
    kv_spec = pl.BlockSpec((tk, FOX_HD), lambda g, qi, ki: (bh(g)[0] * nk + last_k(qi, ki), bh(g)[1]))
    return pl.pallas_call(
        functools.partial(_fox_attn_kernel, tq=tq, tk=tk), name="fox_attn",
        out_shape=jax.ShapeDtypeStruct((M, FOX_W), BF16),
        grid=(batch * FOX_H, nq, nk),
        in_specs=[
            q_spec, kv_spec, kv_spec,
            pl.BlockSpec((1, tq, 1), lambda g, qi, ki: (g, qi, 0)),
            pl.BlockSpec((1, 1, tk), lambda g, qi, ki: (g, 0, last_k(qi, ki))),
        ],
        out_specs=q_spec,
        scratch_shapes=[pltpu.VMEM((tq, 1), F32), pltpu.VMEM((tq, 1), F32),
                        pltpu.VMEM((tq, FOX_HD), F32)],
        compiler_params=_cparams(("parallel", "parallel", "arbitrary")),
    )(qn, kn, vb, cum_col, cum_row)


def _sgu_kernel(u_ref, v_ref, lg_ref, lb_ref, ws_ref, sb_ref, o_ref, *, tm):
    ti = lax.broadcasted_iota(jnp.int32, (SG_CHUNK, SG_CHUNK), 0)
    si = lax.broadcasted_iota(jnp.int32, (SG_CHUNK, SG_CHUNK), 1)
    for g in range(SG_G):
        sl = slice(g * SG_GD, (g + 1) * SG_GD)
        u = _gelu_tanh(u_ref[:, sl])
        v = _gelu_tanh(v_ref[:, sl])
        mu = jnp.mean(v, axis=-1, keepdims=True)
        vc = v - mu
        var = jnp.mean(vc * vc, axis=-1, keepdims=True)
        vn = (vc * lax.rsqrt(var + LN_EPS) * lg_ref[g:g + 1, :] + lb_ref[g:g + 1, :]).astype(BF16)
        ws = jnp.where(ti >= si, ws_ref[g], 0.0).astype(BF16)
        bias = sb_ref[:, g:g + 1]
        for c in range(tm // SG_CHUNK):
            rows = slice(c * SG_CHUNK, (c + 1) * SG_CHUNK)
            mixed = jnp.dot(ws, vn[rows], preferred_element_type=F32) + bias
            o_ref[rows, sl] = (u[rows] * mixed).astype(o_ref.dtype)


def _sgu(hz, ln_g, ln_b, ws, sb_t):
    M = hz.shape[0]
    tm = 512
    return pl.pallas_call(
        functools.partial(_sgu_kernel, tm=tm), name="sgu",
        out_shape=jax.ShapeDtypeStruct((M, SG_W), BF16),
        grid=(M // tm,),
        in_specs=[
            _col_spec(tm, SG_W, P_U), _col_spec(tm, SG_W, P_SV),
            pl.BlockSpec((SG_G, SG_GD), lambda i: (0, 0)),
            pl.BlockSpec((SG_G, SG_GD), lambda i: (0, 0)),
            pl.BlockSpec((SG_G, SG_CHUNK, SG_CHUNK), lambda i: (0, 0, 0)),
            pl.BlockSpec((SG_CHUNK, SG_G), lambda i: (0, 0)),
        ],
        out_specs=pl.BlockSpec((tm, SG_W), lambda i: (i, 0)),
        compiler_params=_cparams(("parallel",)),
    )(hz, hz, ln_g, ln_b, ws, sb_t)


def _outproj_kernel(oa_ref, ob_ref, oc_ref, w_ref, x_ref, g1_ref, ng_ref, sc_ref, sh_ref,
                    xo_ref, h_ref):
    mix = jnp.dot(oa_ref[...], w_ref[0:RW_W, :], preferred_element_type=F32)
    mix += jnp.dot(ob_ref[...], w_ref[RW_W:RW_W + FOX_W, :], preferred_element_type=F32)
    mix += jnp.dot(oc_ref[...], w_ref[RW_W + FOX_W:, :], preferred_element_type=F32)
    x = x_ref[...] + g1_ref[0] * mix
    xo_ref[...] = x
    ms = jnp.mean(x * x, axis=-1, keepdims=True)
    y = x * lax.rsqrt(ms + RMS_EPS) * ng_ref[...]
    h_ref[...] = (y * (1.0 + sc_ref[0]) + sh_ref[0]).astype(BF16)


def _outproj(oa, ob, oc, w, x2, g1, ng, sc, sh, seq):
    M, D = x2.shape
    tm = 512
    bidx = lambda i: (i * tm // seq, 0, 0)
    mod = pl.BlockSpec((1, 1, D), bidx)
    return pl.pallas_call(
        _outproj_kernel, name="outproj",
        out_shape=[jax.ShapeDtypeStruct((M, D), F32), jax.ShapeDtypeStruct((M, D), BF16)],
        grid=(M // tm,),
        in_specs=[
            pl.BlockSpec((tm, RW_W), lambda i: (i, 0)),
            pl.BlockSpec((tm, FOX_W), lambda i: (i, 0)),
            pl.BlockSpec((tm, SG_W), lambda i: (i, 0)),
            pl.BlockSpec((D, D), lambda i: (0, 0)),
            pl.BlockSpec((tm, D), lambda i: (i, 0)),
            mod,
            pl.BlockSpec((1, D), lambda i: (0, 0)),
            mod, mod,
        ],
        out_specs=[pl.BlockSpec((tm, D), lambda i: (i, 0)), pl.BlockSpec((tm, D), lambda i: (i, 0))],
        compiler_params=_cparams(("parallel",)),
    )(oa, ob, oc, w, x2, g1, ng, sc, sh)


def _ffn_kernel(h_ref, w1_ref, w3_ref, w2_ref, x_ref, g2_ref, o_ref, acc):
    j = pl.program_id(1)

    @pl.when(j == 0)
    def _():
        acc[...] = jnp.zeros_like(acc)

    h = h_ref[...]
    a = jnp.dot(h, w1_ref[...], preferred_element_type=F32)
    b = jnp.dot(h, w3_ref[...], preferred_element_type=F32)
    acc[...] += jnp.dot((_silu(a) * b).astype(BF16), w2_ref[...], preferred_element_type=F32)

    @pl.when(j == pl.num_programs(1) - 1)
    def _():
        o_ref[...] = x_ref[...] + g2_ref[0] * acc[...]


def _ffn(h2, w1, w3, w2, x2, g2, seq):
    M, D = x2.shape
    F = w1.shape[1]
    tm, tf = 512, 512
    mod = pl.BlockSpec((1, 1, D), lambda i, j: (i * tm // seq, 0, 0))
    return pl.pallas_call(
        _ffn_kernel, name="ffn",
        out_shape=jax.ShapeDtypeStruct((M, D), F32),
        grid=(M // tm, F // tf),
        in_specs=[
            pl.BlockSpec((tm, D), lambda i, j: (i, 0)),
            pl.BlockSpec((D, tf), lambda i, j: (0, j)),
            pl.BlockSpec((D, tf), lambda i, j: (0, j)),
            pl.BlockSpec((tf, D), lambda i, j: (j, 0)),
            pl.BlockSpec((tm, D), lambda i, j: (i, 0)),
            mod,
        ],
        out_specs=pl.BlockSpec((tm, D), lambda i, j: (i, 0)),
        scratch_shapes=[pltpu.VMEM((tm, D), F32)],
        compiler_params=_cparams(("parallel", "arbitrary")),
    )(h2, w1, w3, w2, x2, g2)


def _router_kernel(h_ref, w_ref, b_ref, o_ref):
    logits = jnp.dot(h_ref[...], w_ref[...], preferred_element_type=F32) + b_ref[...]
    lane = lax.broadcasted_iota(jnp.int32, logits.shape, 1)
    neg = -1e30
    logits = jnp.where(lane < N_EXPERTS, logits, neg)
    m1 = jnp.max(logits, axis=-1, keepdims=True)
    i1 = jnp.min(jnp.where(logits == m1, lane, LANE), axis=-1, keepdims=True)
    rest = jnp.where(lane == i1, neg, logits)
    m2 = jnp.max(rest, axis=-1, keepdims=True)
    i2 = jnp.min(jnp.where(rest == m2, lane, LANE), axis=-1, keepdims=True)
    e2 = jnp.exp(m2 - m1)
    p1 = 1.0 / (1.0 + e2)
    p2 = e2 / (1.0 + e2)
    o_ref[...] = jnp.where(lane == i1, p1, 0.0) + jnp.where(lane == i2, p2, 0.0)


def _router(h2, rw, rb):
    M, D = h2.shape
    tm = 1024
    return pl.pallas_call(
        _router_kernel, name="router",
        out_shape=jax.ShapeDtypeStruct((M, LANE), F32),
        grid=(M // tm,),
        in_specs=[
            pl.BlockSpec((tm, D), lambda i: (i, 0)),
            pl.BlockSpec((D, LANE), lambda i: (0, 0)),
            pl.BlockSpec((1, LANE), lambda i: (0, 0)),
        ],
        out_specs=pl.BlockSpec((tm, LANE), lambda i: (i, 0)),
        compiler_params=_cparams(("parallel",)),
    )(h2, rw, rb)


def _moe_kernel(h_ref, gate_ref, w1_ref, w3_ref, w2_ref, x_ref, g2_ref, o_ref, acc, *, nf):
    j = pl.program_id(1)

    @pl.when(j == 0)
    def _():
        acc[...] = jnp.zeros_like(acc)

    e = j // nf
    gates = gate_ref[...]
    lane = lax.broadcasted_iota(jnp.int32, gates.shape, 1)
    ge = jnp.sum(jnp.where(lane == e, gates, 0.0), axis=-1, keepdims=True)
    h = h_ref[...]
    a = jnp.dot(h, w1_ref[0], preferred_element_type=F32)
    b = jnp.dot(h, w3_ref[0], preferred_element_type=F32)
    acc[...] += jnp.dot((_silu(a) * b * ge).astype(BF16), w2_ref[0], preferred_element_type=F32)

    @pl.when(j == pl.num_programs(1) - 1)
    def _():
        o_ref[...] = x_ref[...] + g2_ref[0] * acc[...]


def _moe(h2, gates, w1, w3, w2, x2, g2, seq):
    M, D = x2.shape
    E, _, F = w1.shape
    tm, tf = 512, 256
    nf = F // tf
    mod = pl.BlockSpec((1, 1, D), lambda i, j: (i * tm // seq, 0, 0))
    return pl.pallas_call(
        functools.partial(_moe_kernel, nf=nf), name="moe",
        out_shape=jax.ShapeDtypeStruct((M, D), F32),
        grid=(M // tm, E * nf),
        in_specs=[
            pl.BlockSpec((tm, D), lambda i, j: (i, 0)),
            pl.BlockSpec((tm, LANE), lambda i, j: (i, 0)),
            pl.BlockSpec((1, D, tf), lambda i, j: (j // nf, 0, j % nf)),
            pl.BlockSpec((1, D, tf), lambda i, j: (j // nf, 0, j % nf)),
            pl.BlockSpec((1, tf, D), lambda i, j: (j // nf, j % nf, 0)),
            pl.BlockSpec((tm, D), lambda i, j: (i, 0)),
            mod,
        ],
        out_specs=pl.BlockSpec((tm, D), lambda i, j: (i, 0)),
        scratch_shapes=[pltpu.VMEM((tm, D), F32)],
        compiler_params=_cparams(("parallel", "arbitrary")),
    )(h2, gates, w1, w3, w2, x2, g2)


def _pack_w_in(w):
    wb = w.astype(BF16)
    d = w.shape[0]
    rw = 3 * RW_W
    parts = [
        wb[:, 0:rw],
        wb[:, C_Q:C_FF],
        wb[:, C_U:N_IN],
        wb[:, rw:C_RWKV_END],
        wb[:, C_FF:C_U],
        jnp.zeros((d, N_PACK - P_FF - FOX_H), BF16),
    ]
    return jnp.concatenate(parts, axis=1)


def _mixing_layer(x2, mods, p, batch, seq):
    D = D_MODEL
    row = lambda t: t.reshape(1, -1)
    hz = _inproj(x2, row(p["norm1_g"]), mods["sc1"], mods["sh1"], _pack_w_in(p["w_in"]), seq)

    mu = p["shift_mu"]
    mu3 = mu[:3 * RW_W].reshape(3, RW_W)
    mul = row(mu[3 * RW_W:])
    r, lw, k, v, kk, a, g = _rwkv_prep(
        hz, mu3, mul, row(p["rw_w0"]), p["rw_w2"].astype(BF16), row(p["rw_a0"]),
        p["rw_a2"].astype(BF16), p["rw_g2"].astype(BF16), row(p["rw_k_k"]), row(p["rw_k_a"]), seq)
    hm = lambda t: t.reshape(RW_H, 1, RW_HD)
    o_a = _rwkv_scan(r, lw, k, v, kk, a, g, hm(p["rw_r_k"]), hm(p["rw_lnx_g"]), hm(p["rw_lnx_b"]),
                     batch, seq)

    fb128 = jnp.zeros((1, LANE), F32).at[0, :FOX_H].set(p["fox_fb"])
    qn, kn, vb, cum = _fox_prep(hz, row(p["fox_qn_g"]), row(p["fox_kn_g"]), fb128, seq)
    cum_hs = jnp.transpose(cum[:, :FOX_H].reshape(batch, seq, FOX_H), (0, 2, 1))
    cum_col = cum_hs.reshape(batch * FOX_H, seq, 1)
    cum_row = cum_hs.reshape(batch * FOX_H, 1, seq)
    o_b = _fox_attn(qn, kn, vb, cum_col, cum_row, batch, seq)

    o_c = _sgu(hz, p["sg_ln_g"], p["sg_ln_b"], p["sg_ws"], jnp.transpose(p["sg_b"]))

    return _outproj(o_a, o_b, o_c, p["w_out"].astype(BF16), x2, mods["g1"], row(p["norm2_g"]),
                    mods["sc2"], mods["sh2"], seq)


def kernel(x, c, ada_w, ada_b, norm1_g, norm2_g, w_in, shift_mu, rw_w0, rw_w2, rw_a0, rw_a2, rw_g2, rw_k_k, rw_k_a, rw_r_k, rw_lnx_g, rw_lnx_b, fox_qn_g, fox_kn_g, fox_fb, sg_ln_g, sg_ln_b, sg_ws, sg_b, w_out, ffn_w1, ffn_w3, ffn_w2, moe_router_w, moe_router_b, moe_w1, moe_w3, moe_w2):
    B, S, D = x.shape
    L = ada_w.shape[0]
    x2 = x.reshape(B * S, D)
    c8 = jnp.zeros((8, D), F32).at[:B].set(c)
    mod = _ada_mod(c8, ada_w, ada_b.reshape(L, 1, N_MOD * D))

    layer_params = dict(
        norm1_g=norm1_g, norm2_g=norm2_g, w_in=w_in, shift_mu=shift_mu, rw_w0=rw_w0, rw_w2=rw_w2,
        rw_a0=rw_a0, rw_a2=rw_a2, rw_g2=rw_g2, rw_k_k=rw_k_k, rw_k_a=rw_k_a, rw_r_k=rw_r_k,
        rw_lnx_g=rw_lnx_g, rw_lnx_b=rw_lnx_b, fox_qn_g=fox_qn_g, fox_kn_g=fox_kn_g, fox_fb=fox_fb,
        sg_ln_g=sg_ln_g, sg_ln_b=sg_ln_b, sg_ws=sg_ws, sg_b=sg_b, w_out=w_out)

    for l in range(L):
        names = ("sh1", "sc1", "g1", "sh2", "sc2", "g2")
        mods = {n: mod[l, :B, i * D:(i + 1) * D].reshape(B, 1, D) for i, n in enumerate(names)}
        p = {n: t[l] for n, t in layer_params.items()}
        x2, h2 = _mixing_layer(x2, mods, p, B, S)
        j = l // 2
        if l % 2 == 0:
            x2 = _ffn(h2, ffn_w1[j].astype(BF16), ffn_w3[j].astype(BF16), ffn_w2[j].astype(BF16),
                      x2, mods["g2"], S)
        else:
            rw = jnp.zeros((D, LANE), BF16).at[:, :N_EXPERTS].set(moe_router_w[j].astype(BF16))
            rb = jnp.zeros((1, LANE), F32).at[0, :N_EXPERTS].set(moe_router_b[j])
            gates = _router(h2, rw, rb)
            x2 = _moe(h2, gates, moe_w1[j].astype(BF16), moe_w3[j].astype(BF16),
                      moe_w2[j].astype(BF16), x2, mods["g2"], S)
    return x2.reshape(B, S, D)
```

```python
import functools

import jax
import jax.numpy as jnp
from jax import lax
from jax.experimental import pallas as pl
from jax.experimental.pallas import tpu as pltpu

F32 = jnp.float32
BF16 = jnp.bfloat16
HIGHEST = lax.Precision.HIGHEST

D_MODEL = 2048
DEPTH = 2
RW_HD = 64
RW_W = 768
RW_H = 12
DECAY_LORA = 64
AAA_LORA = 64
GATE_LORA = 128
FOX_HD = 128
FOX_W = 768
FOX_H = 6
SG_GD = 128
SG_W = 512
SG_G = 4
SG_CHUNK = 128
N_EXPERTS = 8
N_MOD = 6
RMS_EPS = 1e-6
LN_EPS = 1e-5
LNX_EPS = 64e-5

C_R = 0
C_RWKV_END = 3 * RW_W + DECAY_LORA + AAA_LORA + GATE_LORA
C_Q = C_RWKV_END
C_FF = C_Q + 3 * FOX_W
C_U = C_FF + FOX_H
C_SV = C_U + SG_W
N_IN = C_SV + SG_W

P_R, P_K, P_V = 0, 768, 1536
P_Q, P_FK, P_FV = 2304, 3072, 3840
P_U = 4608
P_SV = 5120
P_LORA = 5632
P_FF = 5888
N_PACK = 6144

LANE = 128
VMEM_LIMIT = 56 * 1024 * 1024


def _cparams(sem):
    return pltpu.CompilerParams(dimension_semantics=sem, vmem_limit_bytes=VMEM_LIMIT)


def _sigmoid(x):
    return 1.0 / (1.0 + jnp.exp(-x))


def _softplus(x):
    return jnp.maximum(x, 0.0) + jnp.log(1.0 + jnp.exp(-jnp.abs(x)))


def _gelu_tanh(x):
    return 0.5 * x * (1.0 + jnp.tanh(0.7978845608028654 * (x + 0.044715 * (x * x * x))))


def _silu(x):
    return x * _sigmoid(x)


def _ada_kernel(c_ref, w_ref, b_ref, o_ref):
    cs = _silu(c_ref[...]).astype(BF16)
    w = w_ref[0].astype(BF16)
    o_ref[0] = jnp.dot(cs, w, preferred_element_type=F32) + b_ref[0]


def _ada_mod(c8, ada_w, ada_b3):
    L, D, N = ada_w.shape
    tn = 1536
    return pl.pallas_call(
        _ada_kernel, name="ada_mod",
        out_shape=jax.ShapeDtypeStruct((L, 8, N), F32),
        grid=(L, N // tn),
        in_specs=[
            pl.BlockSpec((8, D), lambda l, j: (0, 0)),
            pl.BlockSpec((1, D, tn), lambda l, j: (l, 0, j)),
            pl.BlockSpec((1, 1, tn), lambda l, j: (l, 0, j)),
        ],
        out_specs=pl.BlockSpec((1, 8, tn), lambda l, j: (l, 0, j)),
        compiler_params=_cparams(("parallel", "parallel")),
    )(c8, ada_w, ada_b3)


def _inproj_kernel(x_ref, g_ref, sc_ref, sh_ref, w_ref, o_ref, h_scr):
    @pl.when(pl.program_id(1) == 0)
    def _():
        x = x_ref[...]
        ms = jnp.mean(x * x, axis=-1, keepdims=True)
        y = x * lax.rsqrt(ms + RMS_EPS) * g_ref[...]
        h_scr[...] = (y * (1.0 + sc_ref[0]) + sh_ref[0]).astype(BF16)

    o_ref[...] = jnp.dot(h_scr[...], w_ref[...], preferred_element_type=F32)


def _inproj(x2, g, sc, sh, w, seq):
    M, D = x2.shape
    N = w.shape[1]
    tm, tn = 1024, 512
    return pl.pallas_call(
        _inproj_kernel, name="inproj",
        out_shape=jax.ShapeDtypeStruct((M, N), F32),
        grid=(M // tm, N // tn),
        in_specs=[
            pl.BlockSpec((tm, D), lambda i, j: (i, 0)),
            pl.BlockSpec((1, D), lambda i, j: (0, 0)),
            pl.BlockSpec((1, 1, D), lambda i, j: (i * tm // seq, 0, 0)),
            pl.BlockSpec((1, 1, D), lambda i, j: (i * tm // seq, 0, 0)),
            pl.BlockSpec((D, tn), lambda i, j: (0, j)),
        ],
        out_specs=pl.BlockSpec((tm, tn), lambda i, j: (i, j)),
        scratch_shapes=[pltpu.VMEM((tm, D), BF16)],
        compiler_params=_cparams(("parallel", "arbitrary")),
    )(x2, g, sc, sh, w)


def _rwkv_prep_kernel(zr_ref, zk_ref, zv_ref, zl_ref, pr_ref, pk_ref, pv_ref, pl_ref,
                      mu_ref, mul_ref, w0_ref, w2_ref, a0_ref, a2_ref, g2_ref, kk_ref, ka_ref,
                      r_out, lw_out, k_out, v_out, kk_out, a_out, g_out, *, tm, seq):
    first = (pl.program_id(0) * tm) % seq == 0

    def shift(z_ref, p_ref, mu):
        z = z_ref[...]
        prev_last = jnp.where(first, 0.0, p_ref[7:8, :])
        zp = pltpu.roll(z, 1, 0)
        row = lax.broadcasted_iota(jnp.int32, z.shape, 0)
        zp = jnp.where(row == 0, prev_last, zp)
        return z + (zp - z) * mu

    r = shift(zr_ref, pr_ref, mu_ref[0:1, :])
    k = shift(zk_ref, pk_ref, mu_ref[1:2, :])
    v = shift(zv_ref, pv_ref, mu_ref[2:3, :])
    lo = shift(zl_ref, pl_ref, mul_ref[...])
    wd = lo[:, 0:DECAY_LORA]
    ad = lo[:, DECAY_LORA:DECAY_LORA + AAA_LORA]
    gd = lo[:, DECAY_LORA + AAA_LORA:]

    dec = w0_ref[...] + jnp.dot(jnp.tanh(wd).astype(BF16), w2_ref[...], preferred_element_type=F32)
    w_log = -_softplus(-dec) - 0.5
    lw = -jnp.exp(w_log)
    a = _sigmoid(a0_ref[...] + jnp.dot(ad.astype(BF16), a2_ref[...], preferred_element_type=F32))
    g = jnp.dot(_sigmoid(gd).astype(BF16), g2_ref[...], preferred_element_type=F32)
    kk = k * kk_ref[...]
    km = k * (1.0 + (a - 1.0) * ka_ref[...])

    for h in range(RW_H):
        sl = slice(h * RW_HD, (h + 1) * RW_HD)
        r_out[h] = r[:, sl]
        lw_out[h] = lw[:, sl]
        k_out[h] = km[:, sl]
        v_out[h] = v[:, sl]
        kk_out[h] = kk[:, sl]
        a_out[h] = a[:, sl]
        g_out[h] = g[:, sl]


def _rwkv_prep(hz, mu3, mul, w0, w2, a0, a2, g2, k_k, k_a, seq):
    M = hz.shape[0]
    tm = 256
    prev = lambda i: jnp.maximum(i * (tm // 8) - 1, 0)
    row = lambda n: pl.BlockSpec((1, n), lambda i: (0, 0))
    in_specs = [
        pl.BlockSpec((tm, RW_W), lambda i: (i, P_R // RW_W)),
        pl.BlockSpec((tm, RW_W), lambda i: (i, P_K // RW_W)),
        pl.BlockSpec((tm, RW_W), lambda i: (i, P_V // RW_W)),
        pl.BlockSpec((tm, 256), lambda i: (i, P_LORA // 256)),
        pl.BlockSpec((8, RW_W), lambda i: (prev(i), P_R // RW_W)),
        pl.BlockSpec((8, RW_W), lambda i: (prev(i), P_K // RW_W)),
        pl.BlockSpec((8, RW_W), lambda i: (prev(i), P_V // RW_W)),
        pl.BlockSpec((8, 256), lambda i: (prev(i), P_LORA // 256)),
        pl.BlockSpec((3, RW_W), lambda i: (0, 0)),
        row(256), row(RW_W),
        pl.BlockSpec((DECAY_LORA, RW_W), lambda i: (0, 0)),
        row(RW_W),
        pl.BlockSpec((AAA_LORA, RW_W), lambda i: (0, 0)),
        pl.BlockSpec((GATE_LORA, RW_W), lambda i: (0, 0)),
        row(RW_W), row(RW_W),
    ]
    hm = jax.ShapeDtypeStruct((RW_H, M, RW_HD), F32)
    hm_spec = pl.BlockSpec((RW_H, tm, RW_HD), lambda i: (0, i, 0))
    return pl.pallas_call(
        functools.partial(_rwkv_prep_kernel, tm=tm, seq=seq), name="rwkv_prep",
        out_shape=[hm] * 7,
        grid=(M // tm,),
        in_specs=in_specs,
        out_specs=[hm_spec] * 7,
        compiler_params=_cparams(("parallel",)),
    )(hz, hz, hz, hz, hz, hz, hz, hz, mu3, mul, w0, w2, a0, a2, g2, k_k, k_a)


def _bmm(a, b, ca, cb):
    return lax.dot_general(a.astype(BF16), b.astype(BF16), (((ca,), (cb,)), ((0,), (0,))),
                           preferred_element_type=F32)


def _split3(x):
    hi = x.astype(BF16)
    r1 = x - hi.astype(F32)
    mid = r1.astype(BF16)
    lo = (r1 - mid.astype(F32)).astype(BF16)
    return hi, mid, lo


def _rwkv_scan_kernel(r_ref, lw_ref, k_ref, v_ref, kk_ref, a_ref, g_ref, rk_ref, lng_ref, lnb_ref,
                      o_ref, s_scr, *, L, nC):
    H, K = RW_H, RW_HD
    n = H * nC

    @pl.when(pl.program_id(1) == 0)
    def _():
        s_scr[...] = jnp.zeros_like(s_scr)

    ld = lambda ref: ref[...].reshape(n, L, K)
    r, lw, k, v, kk, a = ld(r_ref), ld(lw_ref), ld(k_ref), ld(v_ref), ld(kk_ref), ld(a_ref)

    kk = kk / jnp.maximum(jnp.sqrt(jnp.sum(kk * kk, axis=-1, keepdims=True)), 1e-12)
    b = kk * a

    ti = lax.broadcasted_iota(jnp.int32, (L, L), 0)
    si = lax.broadcasted_iota(jnp.int32, (L, L), 1)
    tri_incl = jnp.broadcast_to((ti >= si).astype(BF16), (n, L, L))
    cw = sum(_bmm(tri_incl, piece, 2, 1) for piece in reversed(_split3(lw)))
    cw_last = cw[:, L - 1:L, :]
    e_in = jnp.exp(cw)
    e_out = jnp.exp(-cw)
    e_ex = jnp.exp(cw - lw)
    e_end = jnp.exp(cw_last - cw)
    w_end = jnp.exp(cw_last)

    at = -kk * e_ex
    rt = r * e_in
    bb = b * e_out
    kb = k * e_out
    bh = b * e_end
    kh = k * e_end

    strict = (ti > si)[None]
    incl = (ti >= si)[None]
    a_ab = jnp.where(strict, _bmm(at, bb, 2, 2), 0.0)
    a_ak = jnp.where(strict, _bmm(at, kb, 2, 2), 0.0)
    b_rb = jnp.where(incl, _bmm(rt, bb, 2, 2), 0.0)
    b_rk = jnp.where(incl, _bmm(rt, kb, 2, 2), 0.0)

    x = jnp.broadcast_to((ti == si).astype(F32), (n, L, L))
    size = 1
    while size < L:
        sh = size.bit_length() - 1
        m = (((ti >> (sh + 1)) == (si >> (sh + 1))) & (((ti >> sh) & 1) == 1) & (((si >> sh) & 1) == 0))[None]
        x = x + _bmm(_bmm(x, jnp.where(m, a_ab, 0.0), 2, 1), x, 2, 1)
        size *= 2

    ap = _bmm(x, at, 2, 1)
    u0 = _bmm(x, _bmm(a_ak, v, 2, 1), 2, 1)
    rp = rt + _bmm(b_rb, ap, 2, 1)
    y0 = _bmm(b_rb, u0, 2, 1) + _bmm(b_rk, v, 2, 1)
    gp = _bmm(ap, bh, 1, 1)
    cc = _bmm(u0, bh, 1, 1) + _bmm(v, kh, 1, 1)

    c4 = lambda t: t.reshape((H, nC) + t.shape[1:])
    rp, y0, gp, cc, w_end = c4(rp), c4(y0), c4(gp), c4(cc), c4(w_end)
    s = s_scr[...]
    ys = []
    for c in range(nC):
        ys.append(_bmm(rp[:, c], s, 2, 2) + y0[:, c])
        s = s * w_end[:, c] + _bmm(s, gp[:, c], 2, 1) + cc[:, c]
    s_scr[...] = s
    y = ys[0] if nC == 1 else jnp.concatenate(ys, axis=1)

    mu = jnp.mean(y, axis=-1, keepdims=True)
    yc = y - mu
    var = jnp.mean(yc * yc, axis=-1, keepdims=True)
    yn = yc * lax.rsqrt(var + LNX_EPS) * lng_ref[...] + lnb_ref[...]
    r3, k3, v3 = r_ref[...], k_ref[...], v_ref[...]
    bonus = jnp.sum(r3 * k3 * rk_ref[...], axis=-1, keepdims=True) * v3
    out = (yn + bonus) * g_ref[...]
    for p in range(H // 2):
        pair = jnp.concatenate([out[2 * p], out[2 * p + 1]], axis=-1)
        o_ref[:, p * LANE:(p + 1) * LANE] = pair.astype(o_ref.dtype)


def _rwkv_scan(r, lw, k, v, kk, a, g, r_k, lnx_g, lnx_b, batch, seq):
    M = r.shape[1]
    L, nC = 64, 2
    tb = L * nC
    nj = seq // tb
    hm_spec = pl.BlockSpec((RW_H, tb, RW_HD), lambda b, j: (0, b * nj + j, 0))
    par = pl.BlockSpec((RW_H, 1, RW_HD), lambda b, j: (0, 0, 0))
    return pl.pallas_call(
        functools.partial(_rwkv_scan_kernel, L=L, nC=nC), name="rwkv_scan",
        out_shape=jax.ShapeDtypeStruct((M, RW_W), BF16),
        grid=(batch, nj),
        in_specs=[hm_spec] * 7 + [par] * 3,
        out_specs=pl.BlockSpec((tb, RW_W), lambda b, j: (b * nj + j, 0)),
        scratch_shapes=[pltpu.VMEM((RW_H, RW_HD, RW_HD), F32)],
        compiler_params=_cparams(("parallel", "arbitrary")),
    )(r, lw, k, v, kk, a, g, r_k, lnx_g, lnx_b)


def _fox_prep_kernel(q_ref, k_ref, v_ref, f_ref, qg_ref, kg_ref, fb_ref,
                     qo_ref, ko_ref, vo_ref, cum_ref, carry, *, tm, seq):
    @pl.when((pl.program_id(0) * tm) % seq == 0)
    def _():
        carry[...] = jnp.zeros_like(carry)

    scale = FOX_HD ** -0.5
    for h in range(FOX_H):
        sl = slice(h * FOX_HD, (h + 1) * FOX_HD)
        q = q_ref[:, sl]
        k = k_ref[:, sl]
        qn = q * lax.rsqrt(jnp.mean(q * q, axis=-1, keepdims=True) + RMS_EPS) * qg_ref[...]
        kn = k * lax.rsqrt(jnp.mean(k * k, axis=-1, keepdims=True) + RMS_EPS) * kg_ref[...]
        qo_ref[:, sl] = (qn * scale).astype(BF16)
        ko_ref[:, sl] = kn.astype(BF16)
    vo_ref[...] = v_ref[...].astype(BF16)

    logf = -_softplus(-(f_ref[...] + fb_ref[...]))
    ti = lax.broadcasted_iota(jnp.int32, (tm, tm), 0)
    si = lax.broadcasted_iota(jnp.int32, (tm, tm), 1)
    tri = (ti >= si).astype(BF16)
    cum = carry[...]
    for piece in reversed(_split3(logf)):
        cum = cum + jnp.dot(tri, piece, preferred_element_type=F32)
    cum_ref[...] = cum
    carry[...] = cum[tm - 1:tm, :]


def _fox_prep(hz, qn_g, kn_g, fb128, seq):
    M = hz.shape[0]
    tm = 256
    wide = lambda c: pl.BlockSpec((tm, FOX_W), lambda i: (i, c))
    row = pl.BlockSpec((1, LANE), lambda i: (0, 0))
    o = jax.ShapeDtypeStruct((M, FOX_W), BF16)
    ospec = pl.BlockSpec((tm, FOX_W), lambda i: (i, 0))
    return pl.pallas_call(
        functools.partial(_fox_prep_kernel, tm=tm, seq=seq), name="fox_prep",
        out_shape=[o, o, o, jax.ShapeDtypeStruct((M, LANE), F32)],
        grid=(M // tm,),
        in_specs=[
            _col_spec(tm, FOX_W, P_Q), _col_spec(tm, FOX_W, P_FK), _col_spec(tm, FOX_W, P_FV),
            _col_spec(tm, LANE, P_FF), row, row, row,
        ],
        out_specs=[ospec, ospec, ospec, pl.BlockSpec((tm, LANE), lambda i: (i, 0))],
        scratch_shapes=[pltpu.VMEM((1, LANE), F32)],
        compiler_params=_cparams(("arbitrary",)),
    )(hz, hz, hz, hz, qn_g, kn_g, fb128)


def _col_spec(tm, width, start):
    assert start % width == 0
    return pl.BlockSpec((tm, width), lambda i, *_: (i, start // width))


def _fox_attn_kernel(q_ref, k_ref, v_ref, cq_ref, ck_ref, o_ref, m_scr, l_scr, acc_scr, *, t):
    qi = pl.program_id(1)
    ki = pl.program_id(2)

    @pl.when(ki == 0)
    def _():
        m_scr[...] = jnp.full_like(m_scr, -1e30)
        l_scr[...] = jnp.zeros_like(l_scr)
        acc_scr[...] = jnp.zeros_like(acc_scr)

    def step(diagonal):
        if diagonal:
            keep = (lax.broadcasted_iota(jnp.int32, (t, t), 0)
                    >= lax.broadcasted_iota(jnp.int32, (t, t), 1))
        for h in range(FOX_H):
            sl = slice(h * FOX_HD, (h + 1) * FOX_HD)
            s = lax.dot_general(q_ref[:, sl], k_ref[:, sl], (((1,), (1,)), ((), ())),
                                preferred_element_type=F32)
            s = s + cq_ref[h] - ck_ref[h]
            if diagonal:
                s = jnp.where(keep, s, -1e30)
            m_prev = m_scr[h]
            m_new = jnp.maximum(m_prev, jnp.max(s, axis=-1, keepdims=True))
            alpha = jnp.exp(m_prev - m_new)
            p = jnp.exp(s - m_new)
            l_new = alpha * l_scr[h] + jnp.sum(p, axis=-1, keepdims=True)
            acc = alpha * acc_scr[:, sl] + jnp.dot(p.astype(BF16), v_ref[:, sl],
                                                   preferred_element_type=F32)
            if diagonal:
                o_ref[:, sl] = (acc / l_new).astype(o_ref.dtype)
            else:
                m_scr[h] = m_new
                l_scr[h] = l_new
                acc_scr[:, sl] = acc

    pl.when(ki < qi)(lambda: step(False))
    pl.when(ki == qi)(lambda: step(True))


def _fox_attn(qn, kn, vb, cum_col, cum_row, batch, seq):
    M = qn.shape[0]
    t = 512
    nt = seq // t
    kidx = lambda qi, ki: jnp.minimum(ki, qi)
    q_spec = pl.BlockSpec((t, FOX_W), lambda b, qi, ki: (b * nt + qi, 0))
    kv_spec = pl.BlockSpec((t, FOX_W), lambda b, qi, ki: (b * nt + kidx(qi, ki), 0))
    return pl.pallas_call(
        functools.partial(_fox_attn_kernel, t=t), name="fox_attn",
        out_shape=jax.ShapeDtypeStruct((M, FOX_W), BF16),
        grid=(batch, nt, nt),
        in_specs=[
            q_spec, kv_spec, kv_spec,
            pl.BlockSpec((FOX_H, t, 1), lambda b, qi, ki: (b, qi, 0)),
            pl.BlockSpec((FOX_H, 1, t), lambda b, qi, ki: (b, 0, kidx(qi, ki))),
        ],
        out_specs=q_spec,
        scratch_shapes=[pltpu.VMEM((FOX_H, t, 1), F32), pltpu.VMEM((FOX_H, t, 1), F32),
                        pltpu.VMEM((t, FOX_W), F32)],
        compiler_params=_cparams(("parallel", "parallel", "arbitrary")),
    )(qn, kn, vb, cum_col, cum_row)


def _sgu_kernel(u_ref, v_ref, lg_ref, lb_ref, ws_ref, sb_ref, o_ref, *, tm):
    ti = lax.broadcasted_iota(jnp.int32, (SG_CHUNK, SG_CHUNK), 0)
    si = lax.broadcasted_iota(jnp.int32, (SG_CHUNK, SG_CHUNK), 1)
    for g in range(SG_G):
        sl = slice(g * SG_GD, (g + 1) * SG_GD)
        u = _gelu_tanh(u_ref[:, sl])
        v = _gelu_tanh(v_ref[:, sl])
        mu = jnp.mean(v, axis=-1, keepdims=True)
        vc = v - mu
        var = jnp.mean(vc * vc, axis=-1, keepdims=True)
        vn = (vc * lax.rsqrt(var + LN_EPS) * lg_ref[g:g + 1, :] + lb_ref[g:g + 1, :]).astype(BF16)
        ws = jnp.where(ti >= si, ws_ref[g], 0.0).astype(BF16)
        bias = sb_ref[:, g:g + 1]
        for c in range(tm // SG_CHUNK):
            rows = slice(c * SG_CHUNK, (c + 1) * SG_CHUNK)
            mixed = jnp.dot(ws, vn[rows], preferred_element_type=F32) + bias
            o_ref[rows, sl] = (u[rows] * mixed).astype(o_ref.dtype)


def _sgu(hz, ln_g, ln_b, ws, sb_t):
    M = hz.shape[0]
    tm = 512
    return pl.pallas_call(
        functools.partial(_sgu_kernel, tm=tm), name="sgu",
        out_shape=jax.ShapeDtypeStruct((M, SG_W), BF16),
        grid=(M // tm,),
        in_specs=[
            _col_spec(tm, SG_W, P_U), _col_spec(tm, SG_W, P_SV),
            pl.BlockSpec((SG_G, SG_GD), lambda i: (0, 0)),
            pl.BlockSpec((SG_G, SG_GD), lambda i: (0, 0)),
            pl.BlockSpec((SG_G, SG_CHUNK, SG_CHUNK), lambda i: (0, 0, 0)),
            pl.BlockSpec((SG_CHUNK, SG_G), lambda i: (0, 0)),
        ],
        out_specs=pl.BlockSpec((tm, SG_W), lambda i: (i, 0)),
        compiler_params=_cparams(("parallel",)),
    )(hz, hz, ln_g, ln_b, ws, sb_t)


def _outproj_kernel(oa_ref, ob_ref, oc_ref, w_ref, x_ref, g1_ref, ng_ref, sc_ref, sh_ref,
                    xo_ref, h_ref):
    mix = jnp.dot(oa_ref[...], w_ref[0:RW_W, :], preferred_element_type=F32)
    mix += jnp.dot(ob_ref[...], w_ref[RW_W:RW_W + FOX_W, :], preferred_element_type=F32)
    mix += jnp.dot(oc_ref[...], w_ref[RW_W + FOX_W:, :], preferred_element_type=F32)
    x = x_ref[...] + g1_ref[0] * mix
    xo_ref[...] = x
    ms = jnp.mean(x * x, axis=-1, keepdims=True)
    y = x * lax.rsqrt(ms + RMS_EPS) * ng_ref[...]
    h_ref[...] = (y * (1.0 + sc_ref[0]) + sh_ref[0]).astype(h_ref.dtype)


def _outproj(oa, ob, oc, w, x2, g1, ng, sc, sh, seq, h_dtype):
    M, D = x2.shape
    tm = 512
    bidx = lambda i: (i * tm // seq, 0, 0)
    mod = pl.BlockSpec((1, 1, D), bidx)
    return pl.pallas_call(
        _outproj_kernel, name="outproj",
        out_shape=[jax.ShapeDtypeStruct((M, D), F32), jax.ShapeDtypeStruct((M, D), h_dtype)],
        grid=(M // tm,),
        in_specs=[
            pl.BlockSpec((tm, RW_W), lambda i: (i, 0)),
            pl.BlockSpec((tm, FOX_W), lambda i: (i, 0)),
            pl.BlockSpec((tm, SG_W), lambda i: (i, 0)),
            pl.BlockSpec((D, D), lambda i: (0, 0)),
            pl.BlockSpec((tm, D), lambda i: (i, 0)),
            mod,
            pl.BlockSpec((1, D), lambda i: (0, 0)),
            mod, mod,
        ],
        out_specs=[pl.BlockSpec((tm, D), lambda i: (i, 0)), pl.BlockSpec((tm, D), lambda i: (i, 0))],
        compiler_params=_cparams(("parallel",)),
    )(oa, ob, oc, w, x2, g1, ng, sc, sh)


def _ffn_kernel(h_ref, w1_ref, w3_ref, w2_ref, x_ref, g2_ref, o_ref, acc):
    j = pl.program_id(1)

    @pl.when(j == 0)
    def _():
        acc[...] = jnp.zeros_like(acc)

    h = h_ref[...]
    a = jnp.dot(h, w1_ref[...], preferred_element_type=F32)
    b = jnp.dot(h, w3_ref[...], preferred_element_type=F32)
    acc[...] += jnp.dot((_silu(a) * b).astype(BF16), w2_ref[...], preferred_element_type=F32)

    @pl.when(j == pl.num_programs(1) - 1)
    def _():
        o_ref[...] = x_ref[...] + g2_ref[0] * acc[...]


def _ffn(h2, w1, w3, w2, x2, g2, seq):
    M, D = x2.shape
    F = w1.shape[1]
    tm, tf = 512, 512
    mod = pl.BlockSpec((1, 1, D), lambda i, j: (i * tm // seq, 0, 0))
    return pl.pallas_call(
        _ffn_kernel, name="ffn",
        out_shape=jax.ShapeDtypeStruct((M, D), F32),
        grid=(M // tm, F // tf),
        in_specs=[
            pl.BlockSpec((tm, D), lambda i, j: (i, 0)),
            pl.BlockSpec((D, tf), lambda i, j: (0, j)),
            pl.BlockSpec((D, tf), lambda i, j: (0, j)),
            pl.BlockSpec((tf, D), lambda i, j: (j, 0)),
            pl.BlockSpec((tm, D), lambda i, j: (i, 0)),
            mod,
        ],
        out_specs=pl.BlockSpec((tm, D), lambda i, j: (i, 0)),
        scratch_shapes=[pltpu.VMEM((tm, D), F32)],
        compiler_params=_cparams(("parallel", "arbitrary")),
    )(h2, w1, w3, w2, x2, g2)


def _router_kernel(h_ref, w_ref, b_ref, info_ref, cnt_ref, carry, *, tm):
    @pl.when(pl.program_id(0) == 0)
    def _():
        carry[...] = jnp.zeros_like(carry)

    logits = jnp.dot(h_ref[...].astype(BF16), w_ref[...], preferred_element_type=F32) + b_ref[...]
    lane = lax.broadcasted_iota(jnp.int32, logits.shape, 1)
    neg = -1e30
    logits = jnp.where(lane < N_EXPERTS, logits, neg)
    m1 = jnp.max(logits, axis=-1, keepdims=True)
    i1 = jnp.min(jnp.where(logits == m1, lane, LANE), axis=-1, keepdims=True)
    rest = jnp.where(lane == i1, neg, logits)
    m2 = jnp.max(rest, axis=-1, keepdims=True)
    i2 = jnp.min(jnp.where(rest == m2, lane, LANE), axis=-1, keepdims=True)
    e2 = jnp.exp(m2 - m1)
    p1 = 1.0 / (1.0 + e2)
    p2 = e2 / (1.0 + e2)

    oh1 = (lane == i1).astype(F32)
    oh2 = (lane == i2).astype(F32)
    both = oh1 + oh2
    ti = lax.broadcasted_iota(jnp.int32, (tm, tm), 0)
    si = lax.broadcasted_iota(jnp.int32, (tm, tm), 1)
    strict = (ti > si).astype(BF16)
    before = jnp.dot(strict, both.astype(BF16), preferred_element_type=F32) + carry[...]
    rank1 = jnp.sum(oh1 * before, axis=-1, keepdims=True)
    rank2 = jnp.sum(oh2 * before, axis=-1, keepdims=True)
    total = carry[...] + jnp.sum(both, axis=0, keepdims=True)
    carry[...] = total
    cnt_ref[...] = total

    info = jnp.where(lane == 0, i1.astype(F32), 0.0)
    info = jnp.where(lane == 1, i2.astype(F32), info)
    info = jnp.where(lane == 2, rank1, info)
    info = jnp.where(lane == 3, rank2, info)
    info = jnp.where(lane == 4, p1, info)
    info = jnp.where(lane == 5, p2, info)
    info_ref[...] = info


def _router(h2, rw, rb):
    M, D = h2.shape
    tm = 512
    return pl.pallas_call(
        functools.partial(_router_kernel, tm=tm), name="router",
        out_shape=[jax.ShapeDtypeStruct((M, LANE), F32), jax.ShapeDtypeStruct((1, LANE), F32)],
        grid=(M // tm,),
        in_specs=[
            pl.BlockSpec((tm, D), lambda i: (i, 0)),
            pl.BlockSpec((D, LANE), lambda i: (0, 0)),
            pl.BlockSpec((1, LANE), lambda i: (0, 0)),
        ],
        out_specs=[pl.BlockSpec((tm, LANE), lambda i: (i, 0)), pl.BlockSpec((1, LANE), lambda i: (0, 0))],
        scratch_shapes=[pltpu.VMEM((1, LANE), F32)],
        compiler_params=_cparams(("arbitrary",)),
    )(h2, rw, rb)


MOE_TILE = 512


def _moe_dispatch_kernel(s1_ref, s2_ref, h_ref, xs_in_ref, xs_ref, sem, *, tm):
    del xs_in_ref
    base = pl.program_id(0) * tm

    def copies(r):
        src = h_ref.at[pl.ds(r, 1)]
        return (pltpu.make_async_copy(src, xs_ref.at[pl.ds(s1_ref[base + r], 1)], sem.at[0]),
                pltpu.make_async_copy(src, xs_ref.at[pl.ds(s2_ref[base + r], 1)], sem.at[1]))

    def start(r, carry):
        for cp in copies(r):
            cp.start()
        return carry

    def wait(r, carry):
        for cp in copies(r):
            cp.wait()
        return carry

    lax.fori_loop(0, tm, start, 0)
    lax.fori_loop(0, tm, wait, 0)


def _moe_dispatch(slot1, slot2, h2, n_rows):
    M, D = h2.shape
    tm = 256
    xs0 = jnp.zeros((n_rows, D), F32)
    return pl.pallas_call(
        functools.partial(_moe_dispatch_kernel, tm=tm), name="moe_dispatch",
        out_shape=jax.ShapeDtypeStruct((n_rows, D), F32),
        grid_spec=pltpu.PrefetchScalarGridSpec(
            num_scalar_prefetch=2,
            grid=(M // tm,),
            in_specs=[pl.BlockSpec((tm, D), lambda i, s1, s2: (i, 0)),
                      pl.BlockSpec(memory_space=pl.ANY)],
            out_specs=pl.BlockSpec(memory_space=pl.ANY),
            scratch_shapes=[pltpu.SemaphoreType.DMA((2,))],
        ),
        input_output_aliases={3: 0},
        compiler_params=_cparams(("arbitrary",)),
    )(slot1, slot2, h2, xs0)


def _moe_expert_kernel(te_ref, nu_ref, xs_ref, w1_ref, w3_ref, w2_ref, ys_ref, xb, acc):
    i = pl.program_id(0)
    j = pl.program_id(1)
    used = i < nu_ref[0]

    @pl.when(j == 0)
    def _():
        acc[...] = jnp.zeros_like(acc)
        xb[...] = xs_ref[...].astype(BF16)

    @pl.when(used)
    def _():
        h = xb[...]
        a = jnp.dot(h, w1_ref[0], preferred_element_type=F32)
        b = jnp.dot(h, w3_ref[0], preferred_element_type=F32)
        acc[...] += jnp.dot((_silu(a) * b).astype(BF16), w2_ref[0], preferred_element_type=F32)

    @pl.when(j == pl.num_programs(1) - 1)
    def _():
        ys_ref[...] = acc[...]


def _moe_experts(tile_expert, n_used, xs, w1, w3, w2):
    P, D = xs.shape
    E, _, F = w1.shape
    tm, tf = MOE_TILE, 256
    nf = F // tf
    fj = lambda i, j, nu: jnp.where(i < nu[0], j, nf - 1)
    return pl.pallas_call(
        _moe_expert_kernel, name="moe_experts",
        out_shape=jax.ShapeDtypeStruct((P, D), F32),
        grid_spec=pltpu.PrefetchScalarGridSpec(
            num_scalar_prefetch=2,
            grid=(P // tm, nf),
            in_specs=[
                pl.BlockSpec((tm, D), lambda i, j, te, nu: (i, 0)),
                pl.BlockSpec((1, D, tf), lambda i, j, te, nu: (te[i], 0, fj(i, j, nu))),
                pl.BlockSpec((1, D, tf), lambda i, j, te, nu: (te[i], 0, fj(i, j, nu))),
                pl.BlockSpec((1, tf, D), lambda i, j, te, nu: (te[i], fj(i, j, nu), 0)),
            ],
            out_specs=pl.BlockSpec((tm, D), lambda i, j, te, nu: (i, 0)),
            scratch_shapes=[pltpu.VMEM((tm, D), BF16), pltpu.VMEM((tm, D), F32)],
        ),
        compiler_params=_cparams(("arbitrary", "arbitrary")),
    )(tile_expert, n_used, xs, w1, w3, w2)


def _moe_combine_kernel(s1_ref, s2_ref, ys_ref, info_ref, x_ref, g2_ref, o_ref, a_buf, b_buf, sem, *, tm):
    base = pl.program_id(0) * tm

    def copies(r):
        return (pltpu.make_async_copy(ys_ref.at[pl.ds(s1_ref[base + r], 1)], a_buf.at[pl.ds(r, 1)], sem.at[0]),
                pltpu.make_async_copy(ys_ref.at[pl.ds(s2_ref[base + r], 1)], b_buf.at[pl.ds(r, 1)], sem.at[1]))

    def start(r, carry):
        for cp in copies(r):
            cp.start()
        return carry

    def wait(r, carry):
        for cp in copies(r):
            cp.wait()
        return carry

    lax.fori_loop(0, tm, start, 0)
    lax.fori_loop(0, tm, wait, 0)
    info = info_ref[...]
    p1 = info[:, 4:5]
    p2 = info[:, 5:6]
    o_ref[...] = x_ref[...] + g2_ref[0] * (p1 * a_buf[...] + p2 * b_buf[...])


def _moe_combine(slot1, slot2, ys, info, x2, g2, seq):
    M, D = x2.shape
    tm = 256
    return pl.pallas_call(
        functools.partial(_moe_combine_kernel, tm=tm), name="moe_combine",
        out_shape=jax.ShapeDtypeStruct((M, D), F32),
        grid_spec=pltpu.PrefetchScalarGridSpec(
            num_scalar_prefetch=2,
            grid=(M // tm,),
            in_specs=[
                pl.BlockSpec(memory_space=pl.ANY),
                pl.BlockSpec((tm, LANE), lambda i, s1, s2: (i, 0)),
                pl.BlockSpec((tm, D), lambda i, s1, s2: (i, 0)),
                pl.BlockSpec((1, 1, D), lambda i, s1, s2: (i * tm // seq, 0, 0)),
            ],
            out_specs=pl.BlockSpec((tm, D), lambda i, s1, s2: (i, 0)),
            scratch_shapes=[pltpu.VMEM((tm, D), F32), pltpu.VMEM((tm, D), F32),
                            pltpu.SemaphoreType.DMA((2,))],
        ),
        compiler_params=_cparams(("arbitrary",)),
    )(slot1, slot2, ys, info, x2, g2)


def _moe(h2, rw, rb, w1, w3, w2, x2, g2, seq):
    M, D = x2.shape
    E = w1.shape[0]
    T = MOE_TILE
    n_rows = 2 * M + E * T
    n_tiles = n_rows // T
    info, counts = _router(h2, rw, rb)

    e1, e2, rank1, rank2 = (info[:, c].astype(jnp.int32) for c in range(4))
    cnt = counts[0, :E].astype(jnp.int32)
    padded = (cnt + T - 1) // T * T
    ends = jnp.cumsum(padded)
    off = ends - padded
    expert_ids = jnp.arange(E, dtype=jnp.int32)
    offset_of = lambda e: jnp.sum(jnp.where(e[:, None] == expert_ids[None, :], off[None, :], 0), axis=1)
    slot1 = offset_of(e1) + rank1
    slot2 = offset_of(e2) + rank2
    n_used = (ends[E - 1] // T).astype(jnp.int32)
    tile_start = jnp.arange(n_tiles, dtype=jnp.int32) * T
    tile_start = jnp.minimum(tile_start, (n_used - 1) * T)
    tile_expert = jnp.sum(tile_start[:, None] >= ends[None, :], axis=1).astype(jnp.int32)

    xs = _moe_dispatch(slot1, slot2, h2, n_rows)
    ys = _moe_experts(tile_expert, n_used.reshape(1), xs, w1, w3, w2)
    return _moe_combine(slot1, slot2, ys, info, x2, g2, seq)


def _pack_w_in(w):
    wb = w.astype(BF16)
    d = w.shape[0]
    rw = 3 * RW_W
    parts = [
        wb[:, 0:rw],
        wb[:, C_Q:C_FF],
        wb[:, C_U:N_IN],
        wb[:, rw:C_RWKV_END],
        wb[:, C_FF:C_U],
        jnp.zeros((d, N_PACK - P_FF - FOX_H), BF16),
    ]
    return jnp.concatenate(parts, axis=1)


def _mixing_layer(x2, mods, p, batch, seq, h_dtype):
    row = lambda t: t.reshape(1, -1)
    hz = _inproj(x2, row(p["norm1_g"]), mods["sc1"], mods["sh1"], _pack_w_in(p["w_in"]), seq)

    mu = p["shift_mu"]
    mu3 = mu[:3 * RW_W].reshape(3, RW_W)
    mul = row(mu[3 * RW_W:])
    r, lw, k, v, kk, a, g = _rwkv_prep(
        hz, mu3, mul, row(p["rw_w0"]), p["rw_w2"].astype(BF16), row(p["rw_a0"]),
        p["rw_a2"].astype(BF16), p["rw_g2"].astype(BF16), row(p["rw_k_k"]), row(p["rw_k_a"]), seq)
    hm = lambda t: t.reshape(RW_H, 1, RW_HD)
    o_a = _rwkv_scan(r, lw, k, v, kk, a, g, hm(p["rw_r_k"]), hm(p["rw_lnx_g"]), hm(p["rw_lnx_b"]),
                     batch, seq)

    fb128 = jnp.zeros((1, LANE), F32).at[0, :FOX_H].set(p["fox_fb"])
    qn, kn, vb, cum = _fox_prep(hz, row(p["fox_qn_g"]), row(p["fox_kn_g"]), fb128, seq)
    cum_hs = jnp.transpose(cum[:, :FOX_H].reshape(batch, seq, FOX_H), (0, 2, 1))
    cum_col = cum_hs.reshape(batch * FOX_H, seq, 1)
    cum_row = cum_hs.reshape(batch * FOX_H, 1, seq)
    o_b = _fox_attn(qn, kn, vb, cum_col, cum_row, batch, seq)

    o_c = _sgu(hz, p["sg_ln_g"], p["sg_ln_b"], p["sg_ws"], jnp.transpose(p["sg_b"]))

    return _outproj(o_a, o_b, o_c, p["w_out"].astype(BF16), x2, mods["g1"], row(p["norm2_g"]),
                    mods["sc2"], mods["sh2"], seq, h_dtype)


def kernel(x, c, ada_w, ada_b, norm1_g, norm2_g, w_in, shift_mu, rw_w0, rw_w2, rw_a0, rw_a2, rw_g2, rw_k_k, rw_k_a, rw_r_k, rw_lnx_g, rw_lnx_b, fox_qn_g, fox_kn_g, fox_fb, sg_ln_g, sg_ln_b, sg_ws, sg_b, w_out, ffn_w1, ffn_w3, ffn_w2, moe_router_w, moe_router_b, moe_w1, moe_w3, moe_w2):
    B, S, D = x.shape
    L = ada_w.shape[0]
    x2 = x.reshape(B * S, D)
    c8 = jnp.zeros((8, D), F32).at[:B].set(c)
    mod = _ada_mod(c8, ada_w, ada_b.reshape(L, 1, N_MOD * D))

    layer_params = dict(
        norm1_g=norm1_g, norm2_g=norm2_g, w_in=w_in, shift_mu=shift_mu, rw_w0=rw_w0, rw_w2=rw_w2,
        rw_a0=rw_a0, rw_a2=rw_a2, rw_g2=rw_g2, rw_k_k=rw_k_k, rw_k_a=rw_k_a, rw_r_k=rw_r_k,
        rw_lnx_g=rw_lnx_g, rw_lnx_b=rw_lnx_b, fox_qn_g=fox_qn_g, fox_kn_g=fox_kn_g, fox_fb=fox_fb,
        sg_ln_g=sg_ln_g, sg_ln_b=sg_ln_b, sg_ws=sg_ws, sg_b=sg_b, w_out=w_out)

    for l in range(L):
        names = ("sh1", "sc1", "g1", "sh2", "sc2", "g2")
        mods = {n: mod[l, :B, i * D:(i + 1) * D].reshape(B, 1, D) for i, n in enumerate(names)}
        p = {n: t[l] for n, t in layer_params.items()}
        dense = l % 2 == 0
        x2, h2 = _mixing_layer(x2, mods, p, B, S, BF16 if dense else F32)
        j = l // 2
        if dense:
            x2 = _ffn(h2, ffn_w1[j].astype(BF16), ffn_w3[j].astype(BF16), ffn_w2[j].astype(BF16),
                      x2, mods["g2"], S)
        else:
            rw = jnp.zeros((D, LANE), BF16).at[:, :N_EXPERTS].set(moe_router_w[j].astype(BF16))
            rb = jnp.zeros((1, LANE), F32).at[0, :N_EXPERTS].set(moe_router_b[j])
            x2 = _moe(h2, rw, rb, moe_w1[j].astype(BF16), moe_w3[j].astype(BF16),
                      moe_w2[j].astype(BF16), x2, mods["g2"], S)
    return x2.reshape(B, S, D)
```

```python
import functools

import jax
import jax.numpy as jnp
from jax import lax
from jax.experimental import pallas as pl
from jax.experimental.pallas import tpu as pltpu

F32 = jnp.float32
BF16 = jnp.bfloat16

D_MODEL = 2048
DEPTH = 2
RW_HD = 64
RW_W = 768
RW_H = 12
DECAY_LORA = 64
AAA_LORA = 64
GATE_LORA = 128
FOX_HD = 128
FOX_W = 768
FOX_H = 6
SG_GD = 128
SG_W = 512
SG_G = 4
SG_CHUNK = 128
N_EXPERTS = 8
N_MOD = 6
RMS_EPS = 1e-6
LN_EPS = 1e-5
LNX_EPS = 64e-5

C_R = 0
C_RWKV_END = 3 * RW_W + DECAY_LORA + AAA_LORA + GATE_LORA
C_Q = C_RWKV_END
C_FF = C_Q + 3 * FOX_W
C_U = C_FF + FOX_H
C_SV = C_U + SG_W
N_IN = C_SV + SG_W

P_R, P_K, P_V = 0, 768, 1536
P_LORA = 2304
P_Q, P_FK, P_FV = 2560, 3328, 4096
N_MAIN = C_FF
P_U = N_MAIN
P_SV = P_U + SG_W
P_FF = P_SV + SG_W
N_PACK = 6144
N_TAIL = N_PACK - N_MAIN
COL_TILE = 256

LANE = 128
VMEM_LIMIT = 56 * 1024 * 1024


def _cparams(sem):
    return pltpu.CompilerParams(dimension_semantics=sem, vmem_limit_bytes=VMEM_LIMIT)


def _sigmoid(x):
    return 1.0 / (1.0 + jnp.exp(-x))


def _softplus(x):
    return jnp.maximum(x, 0.0) + jnp.log(1.0 + jnp.exp(-jnp.abs(x)))


def _gelu_tanh(x):
    return 0.5 * x * (1.0 + jnp.tanh(0.7978845608028654 * (x + 0.044715 * (x * x * x))))


def _silu(x):
    return x * _sigmoid(x)


def _ada_kernel(c_ref, w_ref, b_ref, o_ref):
    cs = _silu(c_ref[...]).astype(BF16)
    w = w_ref[0].astype(BF16)
    o_ref[0] = jnp.dot(cs, w, preferred_element_type=F32) + b_ref[0]


def _ada_mod(c8, ada_w, ada_b3):
    L, D, N = ada_w.shape
    tn = 1536
    return pl.pallas_call(
        _ada_kernel, name="ada_mod",
        out_shape=jax.ShapeDtypeStruct((L, 8, N), F32),
        grid=(L, N // tn),
        in_specs=[
            pl.BlockSpec((8, D), lambda l, j: (0, 0)),
            pl.BlockSpec((1, D, tn), lambda l, j: (l, 0, j)),
            pl.BlockSpec((1, 1, tn), lambda l, j: (l, 0, j)),
        ],
        out_specs=pl.BlockSpec((1, 8, tn), lambda l, j: (l, 0, j)),
        compiler_params=_cparams(("parallel", "parallel")),
    )(c8, ada_w, ada_b3)


def _col_spec(tm, width, start):
    assert start % width == 0
    return pl.BlockSpec((tm, width), lambda i, *_: (i, start // width))


def _inproj_kernel(x_ref, g_ref, sc_ref, sh_ref, wm_ref, wt_ref, o_ref, h_scr, *, n_main):
    j = pl.program_id(1)

    @pl.when(j == 0)
    def _():
        x = x_ref[...]
        ms = jnp.mean(x * x, axis=-1, keepdims=True)
        y = x * lax.rsqrt(ms + RMS_EPS) * g_ref[...]
        h_scr[...] = (y * (1.0 + sc_ref[0]) + sh_ref[0]).astype(BF16)

    @pl.when(j < n_main)
    def _():
        o_ref[...] = jnp.dot(h_scr[...], wm_ref[0].astype(BF16), preferred_element_type=F32)

    @pl.when(j >= n_main)
    def _():
        o_ref[...] = jnp.dot(h_scr[...], wt_ref[...], preferred_element_type=F32)


def _inproj(x2, g, sc, sh, w_in, layer, w_tail, seq):
    M, D = x2.shape
    tm, tn = min(2048, seq), COL_TILE
    n_main = N_MAIN // tn
    return pl.pallas_call(
        functools.partial(_inproj_kernel, n_main=n_main), name="inproj",
        out_shape=jax.ShapeDtypeStruct((M, N_PACK), F32),
        grid=(M // tm, N_PACK // tn),
        in_specs=[
            pl.BlockSpec((tm, D), lambda i, j: (i, 0), pipeline_mode=pl.Buffered(1)),
            pl.BlockSpec((1, D), lambda i, j: (0, 0)),
            pl.BlockSpec((1, 1, D), lambda i, j: (i * tm // seq, 0, 0)),
            pl.BlockSpec((1, 1, D), lambda i, j: (i * tm // seq, 0, 0)),
            pl.BlockSpec((1, D, tn), lambda i, j: (layer, 0, jnp.minimum(j, n_main - 1))),
            pl.BlockSpec((D, tn), lambda i, j: (0, jnp.maximum(j - n_main, 0))),
        ],
        out_specs=pl.BlockSpec((tm, tn), lambda i, j: (i, j)),
        scratch_shapes=[pltpu.VMEM((tm, D), BF16)],
        compiler_params=_cparams(("parallel", "arbitrary")),
    )(x2, g, sc, sh, w_in, w_tail)


def _rwkv_prep_kernel(zr_ref, zk_ref, zv_ref, zl_ref, pr_ref, pk_ref, pv_ref, pl_ref,
                      mu_ref, mul_ref, w0_ref, w2_ref, a0_ref, a2_ref, g2_ref, kk_ref, ka_ref,
                      r_out, lw_out, k_out, v_out, kk_out, a_out, g_out, *, tm, seq):
    first = (pl.program_id(0) * tm) % seq == 0

    def shift(z_ref, p_ref, mu):
        z = z_ref[...]
        prev_last = jnp.where(first, 0.0, p_ref[7:8, :])
        zp = pltpu.roll(z, 1, 0)
        row = lax.broadcasted_iota(jnp.int32, z.shape, 0)
        zp = jnp.where(row == 0, prev_last, zp)
        return z + (zp - z) * mu

    r = shift(zr_ref, pr_ref, mu_ref[0:1, :])
    k = shift(zk_ref, pk_ref, mu_ref[1:2, :])
    v = shift(zv_ref, pv_ref, mu_ref[2:3, :])
    lo = shift(zl_ref, pl_ref, mul_ref[...])
    wd = lo[:, 0:DECAY_LORA]
    ad = lo[:, DECAY_LORA:DECAY_LORA + AAA_LORA]
    gd = lo[:, DECAY_LORA + AAA_LORA:]

    dec = w0_ref[...] + jnp.dot(jnp.tanh(wd).astype(BF16), w2_ref[...], preferred_element_type=F32)
    w_log = -_softplus(-dec) - 0.5
    lw = -jnp.exp(w_log)
    a = _sigmoid(a0_ref[...] + jnp.dot(ad.astype(BF16), a2_ref[...], preferred_element_type=F32))
    g = jnp.dot(_sigmoid(gd).astype(BF16), g2_ref[...], preferred_element_type=F32)
    kk = k * kk_ref[...]
    km = k * (1.0 + (a - 1.0) * ka_ref[...])

    for h in range(RW_H):
        sl = slice(h * RW_HD, (h + 1) * RW_HD)
        r_out[h] = r[:, sl]
        lw_out[h] = lw[:, sl]
        k_out[h] = km[:, sl]
        v_out[h] = v[:, sl]
        kk_out[h] = kk[:, sl]
        a_out[h] = a[:, sl]
        g_out[h] = g[:, sl]


def _rwkv_prep(hz, mu3, mul, w0, w2, a0, a2, g2, k_k, k_a, seq):
    M = hz.shape[0]
    tm = 256
    prev = lambda i: jnp.maximum(i * (tm // 8) - 1, 0)
    row = lambda n: pl.BlockSpec((1, n), lambda i: (0, 0))
    in_specs = [
        pl.BlockSpec((tm, RW_W), lambda i: (i, P_R // RW_W)),
        pl.BlockSpec((tm, RW_W), lambda i: (i, P_K // RW_W)),
        pl.BlockSpec((tm, RW_W), lambda i: (i, P_V // RW_W)),
        pl.BlockSpec((tm, 256), lambda i: (i, P_LORA // 256)),
        pl.BlockSpec((8, RW_W), lambda i: (prev(i), P_R // RW_W)),
        pl.BlockSpec((8, RW_W), lambda i: (prev(i), P_K // RW_W)),
        pl.BlockSpec((8, RW_W), lambda i: (prev(i), P_V // RW_W)),
        pl.BlockSpec((8, 256), lambda i: (prev(i), P_LORA // 256)),
        pl.BlockSpec((3, RW_W), lambda i: (0, 0)),
        row(256), row(RW_W),
        pl.BlockSpec((DECAY_LORA, RW_W), lambda i: (0, 0)),
        row(RW_W),
        pl.BlockSpec((AAA_LORA, RW_W), lambda i: (0, 0)),
        pl.BlockSpec((GATE_LORA, RW_W), lambda i: (0, 0)),
        row(RW_W), row(RW_W),
    ]
    hm = jax.ShapeDtypeStruct((RW_H, M, RW_HD), F32)
    hm_spec = pl.BlockSpec((RW_H, tm, RW_HD), lambda i: (0, i, 0))
    return pl.pallas_call(
        functools.partial(_rwkv_prep_kernel, tm=tm, seq=seq), name="rwkv_prep",
        out_shape=[hm] * 7,
        grid=(M // tm,),
        in_specs=in_specs,
        out_specs=[hm_spec] * 7,
        compiler_params=_cparams(("parallel",)),
    )(hz, hz, hz, hz, hz, hz, hz, hz, mu3, mul, w0, w2, a0, a2, g2, k_k, k_a)


def _bmm(a, b, ca, cb):
    return lax.dot_general(a.astype(BF16), b.astype(BF16), (((ca,), (cb,)), ((0,), (0,))),
                           preferred_element_type=F32)


def _split3(x):
    hi = x.astype(BF16)
    r1 = x - hi.astype(F32)
    mid = r1.astype(BF16)
    lo = (r1 - mid.astype(F32)).astype(BF16)
    return hi, mid, lo


def _rwkv_scan_kernel(r_ref, lw_ref, k_ref, v_ref, kk_ref, a_ref, g_ref, rk_ref, lng_ref, lnb_ref,
                      o_ref, s_scr, *, L, nC):
    H, K = RW_H, RW_HD
    n = H * nC

    @pl.when(pl.program_id(1) == 0)
    def _():
        s_scr[...] = jnp.zeros_like(s_scr)

    ld = lambda ref: ref[...].reshape(n, L, K)
    r, lw, k, v, kk, a = ld(r_ref), ld(lw_ref), ld(k_ref), ld(v_ref), ld(kk_ref), ld(a_ref)

    kk = kk / jnp.maximum(jnp.sqrt(jnp.sum(kk * kk, axis=-1, keepdims=True)), 1e-12)
    b = kk * a

    ti = lax.broadcasted_iota(jnp.int32, (L, L), 0)
    si = lax.broadcasted_iota(jnp.int32, (L, L), 1)
    tri_incl = jnp.broadcast_to((ti >= si).astype(BF16), (n, L, L))
    c3 = _bmm(tri_incl, jnp.concatenate(_split3(lw), axis=-1), 2, 1)
    cw = c3[..., 2 * K:] + c3[..., K:2 * K] + c3[..., :K]
    cw_last = cw[:, L - 1:L, :]
    e_in = jnp.exp(cw)
    e_out = jnp.exp(-cw)
    e_ex = jnp.exp(cw - lw)
    e_end = jnp.exp(cw_last - cw)
    w_end = jnp.exp(cw_last)

    at = -kk * e_ex
    rt = r * e_in
    bb = b * e_out
    kb = k * e_out
    bh = b * e_end
    kh = k * e_end

    pm = _bmm(jnp.concatenate([at, rt], axis=1), jnp.concatenate([bb, kb], axis=1), 2, 2)
    t2 = lax.broadcasted_iota(jnp.int32, (2 * L, 2 * L), 0)
    s2 = lax.broadcasted_iota(jnp.int32, (2 * L, 2 * L), 1)
    tt, ss = t2 & (L - 1), s2 & (L - 1)
    causal = (tt - ss + jnp.where(t2 < L, 0, 1)) > 0
    pm = jnp.where(causal[None], pm, 0.0)
    a_ab = pm[:, :L, :L]
    a_ak = pm[:, :L, L:]
    b_all = pm[:, L:, :]

    x = jnp.broadcast_to((ti == si).astype(F32), (n, L, L))
    size = 1
    while size < L:
        sh = size.bit_length() - 1
        m = (((ti >> (sh + 1)) == (si >> (sh + 1))) & (((ti >> sh) & 1) == 1) & (((si >> sh) & 1) == 0))[None]
        x = x + _bmm(_bmm(x, jnp.where(m, a_ab, 0.0), 2, 1), x, 2, 1)
        size *= 2

    xu = _bmm(x, jnp.concatenate([at, _bmm(a_ak, v, 2, 1)], axis=-1), 2, 1)
    lower = jnp.concatenate([jnp.zeros_like(v), v], axis=-1)
    w = _bmm(b_all, jnp.concatenate([xu, lower], axis=1), 2, 1)
    rp = rt + w[..., :K]
    y0 = w[..., K:]
    big = _bmm(jnp.concatenate([xu, v], axis=-1), jnp.concatenate([bh, kh], axis=-1), 1, 1)
    gp = big[:, :K, :K]
    cc = big[:, K:2 * K, :K] + big[:, 2 * K:, K:]

    c4 = lambda t: t.reshape((H, nC) + t.shape[1:])
    rp, y0, gp, cc, w_end = c4(rp), c4(y0), c4(gp), c4(cc), c4(w_end)
    s = s_scr[...]
    ys = []
    for c in range(nC):
        ys.append(_bmm(rp[:, c], s, 2, 2) + y0[:, c])
        s = s * w_end[:, c] + _bmm(s, gp[:, c], 2, 1) + cc[:, c]
    s_scr[...] = s
    y = ys[0] if nC == 1 else jnp.concatenate(ys, axis=1)

    mu = jnp.mean(y, axis=-1, keepdims=True)
    yc = y - mu
    var = jnp.mean(yc * yc, axis=-1, keepdims=True)
    yn = yc * lax.rsqrt(var + LNX_EPS) * lng_ref[...] + lnb_ref[...]
    r3, k3, v3 = r_ref[...], k_ref[...], v_ref[...]
    bonus = jnp.sum(r3 * k3 * rk_ref[...], axis=-1, keepdims=True) * v3
    out = (yn + bonus) * g_ref[...]
    for p in range(H // 2):
        pair = jnp.concatenate([out[2 * p], out[2 * p + 1]], axis=-1)
        o_ref[:, p * LANE:(p + 1) * LANE] = pair.astype(o_ref.dtype)


def _rwkv_scan(r, lw, k, v, kk, a, g, r_k, lnx_g, lnx_b, batch, seq):
    M = r.shape[1]
    L, nC = 64, 2
    tb = L * nC
    nj = seq // tb
    hm_spec = pl.BlockSpec((RW_H, tb, RW_HD), lambda b, j: (0, b * nj + j, 0))
    par = pl.BlockSpec((RW_H, 1, RW_HD), lambda b, j: (0, 0, 0))
    return pl.pallas_call(
        functools.partial(_rwkv_scan_kernel, L=L, nC=nC), name="rwkv_scan",
        out_shape=jax.ShapeDtypeStruct((M, RW_W), BF16),
        grid=(batch, nj),
        in_specs=[hm_spec] * 7 + [par] * 3,
        out_specs=pl.BlockSpec((tb, RW_W), lambda b, j: (b * nj + j, 0)),
        scratch_shapes=[pltpu.VMEM((RW_H, RW_HD, RW_HD), F32)],
        compiler_params=_cparams(("parallel", "arbitrary")),
    )(r, lw, k, v, kk, a, g, r_k, lnx_g, lnx_b)


FOX_XD = 2 * FOX_HD


def _fox_prep_kernel(*refs, tm, seq):
    q_refs, k_refs, v_refs = refs[0:3], refs[3:6], refs[6:9]
    f_ref, qg_ref, kg_ref, fb_ref, kx_ref, qxT_ref, vT_ref, carry = refs[9:]

    @pl.when((pl.program_id(0) * tm) % seq == 0)
    def _():
        carry[...] = jnp.zeros_like(carry)

    logf = -_softplus(-(f_ref[...] + fb_ref[...]))
    ti = lax.broadcasted_iota(jnp.int32, (tm, tm), 0)
    si = lax.broadcasted_iota(jnp.int32, (tm, tm), 1)
    tri = (ti >= si).astype(BF16)
    cum = carry[...]
    for piece in reversed(_split3(logf)):
        cum = cum + jnp.dot(tri, piece, preferred_element_type=F32)
    carry[...] = cum[tm - 1:tm, :]
    cum_t = cum.T

    scale = FOX_HD ** -0.5
    lane = lax.broadcasted_iota(jnp.int32, (tm, FOX_HD), 1)
    sub = lax.broadcasted_iota(jnp.int32, (FOX_HD, tm), 0)
    pieces = lambda x: [t.astype(F32) for t in _split3(x)]
    for h in range(FOX_H):
        blk, half = h // 2, slice((h % 2) * FOX_HD, (h % 2 + 1) * FOX_HD)
        q = q_refs[blk][:, half]
        k = k_refs[blk][:, half]
        qn = q * lax.rsqrt(jnp.mean(q * q, axis=-1, keepdims=True) + RMS_EPS) * qg_ref[...]
        kn = k * lax.rsqrt(jnp.mean(k * k, axis=-1, keepdims=True) + RMS_EPS) * kg_ref[...]

        c_hi, c_mid, c_lo = pieces(cum[:, h:h + 1])
        k_extra = jnp.where(lane < 3, 1.0, 0.0)
        k_extra = jnp.where(lane == 3, -c_hi, k_extra)
        k_extra = jnp.where(lane == 4, -c_mid, k_extra)
        k_extra = jnp.where(lane == 5, -c_lo, k_extra)
        kx_ref[:, h * FOX_XD:h * FOX_XD + FOX_HD] = kn.astype(BF16)
        kx_ref[:, h * FOX_XD + FOX_HD:(h + 1) * FOX_XD] = k_extra.astype(BF16)

        r_hi, r_mid, r_lo = pieces(cum_t[h:h + 1, :])
        q_extra = jnp.where(sub < 6, 1.0, 0.0)
        q_extra = jnp.where(sub == 0, r_hi, q_extra)
        q_extra = jnp.where(sub == 1, r_mid, q_extra)
        q_extra = jnp.where(sub == 2, r_lo, q_extra)
        qxT_ref[h * FOX_XD:h * FOX_XD + FOX_HD, :] = (qn * scale).T.astype(BF16)
        qxT_ref[h * FOX_XD + FOX_HD:(h + 1) * FOX_XD, :] = q_extra.astype(BF16)

        vT_ref[h * FOX_HD:(h + 1) * FOX_HD, :] = v_refs[blk][:, half].T.astype(BF16)


def _fox_prep(hz, qn_g, kn_g, fb128, seq):
    M = hz.shape[0]
    tm = 256
    row = pl.BlockSpec((1, LANE), lambda i: (0, 0))
    pieces = lambda start: [_col_spec(tm, COL_TILE, start + c * COL_TILE) for c in range(FOX_W // COL_TILE)]
    return pl.pallas_call(
        functools.partial(_fox_prep_kernel, tm=tm, seq=seq), name="fox_prep",
        out_shape=[jax.ShapeDtypeStruct((M, FOX_H * FOX_XD), BF16),
                   jax.ShapeDtypeStruct((FOX_H * FOX_XD, M), BF16),
                   jax.ShapeDtypeStruct((FOX_W, M), BF16)],
        grid=(M // tm,),
        in_specs=pieces(P_Q) + pieces(P_FK) + pieces(P_FV) + [_col_spec(tm, LANE, P_FF), row, row, row],
        out_specs=[pl.BlockSpec((tm, FOX_H * FOX_XD), lambda i: (i, 0)),
                   pl.BlockSpec((FOX_H * FOX_XD, tm), lambda i: (0, i)),
                   pl.BlockSpec((FOX_W, tm), lambda i: (0, i))],
        scratch_shapes=[pltpu.VMEM((1, LANE), F32)],
        compiler_params=_cparams(("arbitrary",)),
    )(*([hz] * 10), qn_g, kn_g, fb128)


def _fox_attn_kernel(kx_ref, qxT_ref, vT_ref, o_ref, m_scr, l_scr, acc_scr, *, t):
    qi = pl.program_id(1)
    ki = pl.program_id(2)

    @pl.when(ki == 0)
    def _():
        m_scr[...] = jnp.full_like(m_scr, -1e30)
        l_scr[...] = jnp.zeros_like(l_scr)
        acc_scr[...] = jnp.zeros_like(acc_scr)

    def step(diagonal):
        if diagonal:
            keep = (lax.broadcasted_iota(jnp.int32, (t, t), 1)
                    >= lax.broadcasted_iota(jnp.int32, (t, t), 0))
        for h in range(FOX_H):
            xs = slice(h * FOX_XD, (h + 1) * FOX_XD)
            hs = slice(h * FOX_HD, (h + 1) * FOX_HD)
            s = jnp.dot(kx_ref[:, xs], qxT_ref[xs, :], preferred_element_type=F32)
            if diagonal:
                s = jnp.where(keep, s, -1e30)
            m_prev = m_scr[h]
            m_new = jnp.maximum(m_prev, jnp.max(s, axis=0, keepdims=True))
            alpha = jnp.exp(m_prev - m_new)
            p = jnp.exp(s - m_new)
            l_new = alpha * l_scr[h] + jnp.sum(p, axis=0, keepdims=True)
            acc = alpha * acc_scr[hs, :] + jnp.dot(vT_ref[hs, :], p.astype(BF16),
                                                   preferred_element_type=F32)
            if diagonal:
                o_ref[:, hs] = (acc / l_new).T.astype(o_ref.dtype)
            else:
                m_scr[h] = m_new
                l_scr[h] = l_new
                acc_scr[hs, :] = acc

    pl.when(ki < qi)(lambda: step(False))
    pl.when(ki == qi)(lambda: step(True))


def _fox_attn(kx, qxT, vT, batch, seq):
    M = kx.shape[0]
    t = 512
    nt = seq // t
    kidx = lambda qi, ki: jnp.minimum(ki, qi)
    return pl.pallas_call(
        functools.partial(_fox_attn_kernel, t=t), name="fox_attn",
        out_shape=jax.ShapeDtypeStruct((M, FOX_W), BF16),
        grid=(batch, nt, nt),
        in_specs=[
            pl.BlockSpec((t, FOX_H * FOX_XD), lambda b, qi, ki: (b * nt + kidx(qi, ki), 0)),
            pl.BlockSpec((FOX_H * FOX_XD, t), lambda b, qi, ki: (0, b * nt + qi)),
            pl.BlockSpec((FOX_W, t), lambda b, qi, ki: (0, b * nt + kidx(qi, ki))),
        ],
        out_specs=pl.BlockSpec((t, FOX_W), lambda b, qi, ki: (b * nt + qi, 0)),
        scratch_shapes=[pltpu.VMEM((FOX_H, 1, t), F32), pltpu.VMEM((FOX_H, 1, t), F32),
                        pltpu.VMEM((FOX_W, t), F32)],
        compiler_params=_cparams(("parallel", "parallel", "arbitrary")),
    )(kx, qxT, vT)


def _sgu_kernel(u0_ref, u1_ref, v0_ref, v1_ref, lg_ref, lb_ref, ws_ref, sb_ref, o_ref, *, tm):
    ti = lax.broadcasted_iota(jnp.int32, (SG_CHUNK, SG_CHUNK), 0)
    si = lax.broadcasted_iota(jnp.int32, (SG_CHUNK, SG_CHUNK), 1)
    u_refs, v_refs = (u0_ref, u1_ref), (v0_ref, v1_ref)
    for g in range(SG_G):
        sl = slice(g * SG_GD, (g + 1) * SG_GD)
        half = slice((g % 2) * SG_GD, (g % 2 + 1) * SG_GD)
        u = _gelu_tanh(u_refs[g // 2][:, half])
        v = _gelu_tanh(v_refs[g // 2][:, half])
        mu = jnp.mean(v, axis=-1, keepdims=True)
        vc = v - mu
        var = jnp.mean(vc * vc, axis=-1, keepdims=True)
        vn = (vc * lax.rsqrt(var + LN_EPS) * lg_ref[g:g + 1, :] + lb_ref[g:g + 1, :]).astype(BF16)
        ws = jnp.where(ti >= si, ws_ref[g], 0.0).astype(BF16)
        bias = sb_ref[:, g:g + 1]
        for c in range(tm // SG_CHUNK):
            rows = slice(c * SG_CHUNK, (c + 1) * SG_CHUNK)
            mixed = jnp.dot(ws, vn[rows], preferred_element_type=F32) + bias
            o_ref[rows, sl] = (u[rows] * mixed).astype(o_ref.dtype)


def _sgu(hz, ln_g, ln_b, ws, sb_t):
    M = hz.shape[0]
    tm = 512
    return pl.pallas_call(
        functools.partial(_sgu_kernel, tm=tm), name="sgu",
        out_shape=jax.ShapeDtypeStruct((M, SG_W), BF16),
        grid=(M // tm,),
        in_specs=[
            _col_spec(tm, COL_TILE, P_U), _col_spec(tm, COL_TILE, P_U + COL_TILE),
            _col_spec(tm, COL_TILE, P_SV), _col_spec(tm, COL_TILE, P_SV + COL_TILE),
            pl.BlockSpec((SG_G, SG_GD), lambda i: (0, 0)),
            pl.BlockSpec((SG_G, SG_GD), lambda i: (0, 0)),
            pl.BlockSpec((SG_G, SG_CHUNK, SG_CHUNK), lambda i: (0, 0, 0)),
            pl.BlockSpec((SG_CHUNK, SG_G), lambda i: (0, 0)),
        ],
        out_specs=pl.BlockSpec((tm, SG_W), lambda i: (i, 0)),
        compiler_params=_cparams(("parallel",)),
    )(hz, hz, hz, hz, ln_g, ln_b, ws, sb_t)


def _outproj_kernel(oa_ref, ob_ref, oc_ref, w_ref, x_ref, g1_ref, ng_ref, sc_ref, sh_ref,
                    xo_ref, h_ref):
    mix = jnp.dot(oa_ref[...], w_ref[0:RW_W, :], preferred_element_type=F32)
    mix += jnp.dot(ob_ref[...], w_ref[RW_W:RW_W + FOX_W, :], preferred_element_type=F32)
    mix += jnp.dot(oc_ref[...], w_ref[RW_W + FOX_W:, :], preferred_element_type=F32)
    x = x_ref[...] + g1_ref[0] * mix
    xo_ref[...] = x
    ms = jnp.mean(x * x, axis=-1, keepdims=True)
    y = x * lax.rsqrt(ms + RMS_EPS) * ng_ref[...]
    h_ref[...] = (y * (1.0 + sc_ref[0]) + sh_ref[0]).astype(h_ref.dtype)


def _outproj(oa, ob, oc, w, x2, g1, ng, sc, sh, seq, h_dtype):
    M, D = x2.shape
    tm = 512
    bidx = lambda i: (i * tm // seq, 0, 0)
    mod = pl.BlockSpec((1, 1, D), bidx)
    return pl.pallas_call(
        _outproj_kernel, name="outproj",
        out_shape=[jax.ShapeDtypeStruct((M, D), F32), jax.ShapeDtypeStruct((M, D), h_dtype)],
        grid=(M // tm,),
        in_specs=[
            pl.BlockSpec((tm, RW_W), lambda i: (i, 0)),
            pl.BlockSpec((tm, FOX_W), lambda i: (i, 0)),
            pl.BlockSpec((tm, SG_W), lambda i: (i, 0)),
            pl.BlockSpec((D, D), lambda i: (0, 0)),
            pl.BlockSpec((tm, D), lambda i: (i, 0)),
            mod,
            pl.BlockSpec((1, D), lambda i: (0, 0)),
            mod, mod,
        ],
        out_specs=[pl.BlockSpec((tm, D), lambda i: (i, 0)), pl.BlockSpec((tm, D), lambda i: (i, 0))],
        compiler_params=_cparams(("parallel",)),
    )(oa, ob, oc, w, x2, g1, ng, sc, sh)


def _ffn_kernel(h_ref, w1_ref, w3_ref, w2_ref, x_ref, g2_ref, o_ref, acc):
    j = pl.program_id(1)

    @pl.when(j == 0)
    def _():
        acc[...] = jnp.zeros_like(acc)

    h = h_ref[...]
    a = jnp.dot(h, w1_ref[...], preferred_element_type=F32)
    b = jnp.dot(h, w3_ref[...], preferred_element_type=F32)
    acc[...] += jnp.dot((_silu(a) * b).astype(BF16), w2_ref[...], preferred_element_type=F32)

    @pl.when(j == pl.num_programs(1) - 1)
    def _():
        o_ref[...] = x_ref[...] + g2_ref[0] * acc[...]


def _ffn(h2, w1, w3, w2, x2, g2, seq):
    M, D = x2.shape
    F = w1.shape[1]
    tm, tf = 512, 512
    mod = pl.BlockSpec((1, 1, D), lambda i, j: (i * tm // seq, 0, 0))
    return pl.pallas_call(
        _ffn_kernel, name="ffn",
        out_shape=jax.ShapeDtypeStruct((M, D), F32),
        grid=(M // tm, F // tf),
        in_specs=[
            pl.BlockSpec((tm, D), lambda i, j: (i, 0)),
            pl.BlockSpec((D, tf), lambda i, j: (0, j)),
            pl.BlockSpec((D, tf), lambda i, j: (0, j)),
            pl.BlockSpec((tf, D), lambda i, j: (j, 0)),
            pl.BlockSpec((tm, D), lambda i, j: (i, 0)),
            mod,
        ],
        out_specs=pl.BlockSpec((tm, D), lambda i, j: (i, 0)),
        scratch_shapes=[pltpu.VMEM((tm, D), F32)],
        compiler_params=_cparams(("parallel", "arbitrary")),
    )(h2, w1, w3, w2, x2, g2)


def _router_kernel(h_ref, w_ref, b_ref, info_ref, cnt_ref, carry, *, tm):
    @pl.when(pl.program_id(0) == 0)
    def _():
        carry[...] = jnp.zeros_like(carry)

    logits = jnp.dot(h_ref[...].astype(BF16), w_ref[...], preferred_element_type=F32) + b_ref[...]
    lane = lax.broadcasted_iota(jnp.int32, logits.shape, 1)
    neg = -1e30
    logits = jnp.where(lane < N_EXPERTS, logits, neg)
    m1 = jnp.max(logits, axis=-1, keepdims=True)
    i1 = jnp.min(jnp.where(logits == m1, lane, LANE), axis=-1, keepdims=True)
    rest = jnp.where(lane == i1, neg, logits)
    m2 = jnp.max(rest, axis=-1, keepdims=True)
    i2 = jnp.min(jnp.where(rest == m2, lane, LANE), axis=-1, keepdims=True)
    e2 = jnp.exp(m2 - m1)
    p1 = 1.0 / (1.0 + e2)
    p2 = e2 / (1.0 + e2)

    oh1 = (lane == i1).astype(F32)
    oh2 = (lane == i2).astype(F32)
    both = oh1 + oh2
    ti = lax.broadcasted_iota(jnp.int32, (tm, tm), 0)
    si = lax.broadcasted_iota(jnp.int32, (tm, tm), 1)
    strict = (ti > si).astype(BF16)
    before = jnp.dot(strict, both.astype(BF16), preferred_element_type=F32) + carry[...]
    rank1 = jnp.sum(oh1 * before, axis=-1, keepdims=True)
    rank2 = jnp.sum(oh2 * before, axis=-1, keepdims=True)
    total = carry[...] + jnp.sum(both, axis=0, keepdims=True)
    carry[...] = total
    cnt_ref[...] = total

    info = jnp.where(lane == 0, i1.astype(F32), 0.0)
    info = jnp.where(lane == 1, i2.astype(F32), info)
    info = jnp.where(lane == 2, rank1, info)
    info = jnp.where(lane == 3, rank2, info)
    info = jnp.where(lane == 4, p1, info)
    info = jnp.where(lane == 5, p2, info)
    info_ref[...] = info


def _router(h2, rw, rb):
    M, D = h2.shape
    tm = 512
    return pl.pallas_call(
        functools.partial(_router_kernel, tm=tm), name="router",
        out_shape=[jax.ShapeDtypeStruct((M, LANE), F32), jax.ShapeDtypeStruct((1, LANE), F32)],
        grid=(M // tm,),
        in_specs=[
            pl.BlockSpec((tm, D), lambda i: (i, 0)),
            pl.BlockSpec((D, LANE), lambda i: (0, 0)),
            pl.BlockSpec((1, LANE), lambda i: (0, 0)),
        ],
        out_specs=[pl.BlockSpec((tm, LANE), lambda i: (i, 0)), pl.BlockSpec((1, LANE), lambda i: (0, 0))],
        scratch_shapes=[pltpu.VMEM((1, LANE), F32)],
        compiler_params=_cparams(("arbitrary",)),
    )(h2, rw, rb)


MOE_TILE = 512


def _moe_dispatch_kernel(s1_ref, s2_ref, h_ref, xs_in_ref, xs_ref, sem, *, tm):
    del xs_in_ref
    base = pl.program_id(0) * tm

    def copies(r):
        src = h_ref.at[pl.ds(r, 1)]
        return (pltpu.make_async_copy(src, xs_ref.at[pl.ds(s1_ref[base + r], 1)], sem.at[0]),
                pltpu.make_async_copy(src, xs_ref.at[pl.ds(s2_ref[base + r], 1)], sem.at[1]))

    def start(r, carry):
        for cp in copies(r):
            cp.start()
        return carry

    def wait(r, carry):
        for cp in copies(r):
            cp.wait()
        return carry

    lax.fori_loop(0, tm, start, 0)
    lax.fori_loop(0, tm, wait, 0)


def _moe_dispatch(slot1, slot2, h2, n_rows):
    M, D = h2.shape
    tm = 256
    xs0 = jnp.zeros((n_rows, D), F32)
    return pl.pallas_call(
        functools.partial(_moe_dispatch_kernel, tm=tm), name="moe_dispatch",
        out_shape=jax.ShapeDtypeStruct((n_rows, D), F32),
        grid_spec=pltpu.PrefetchScalarGridSpec(
            num_scalar_prefetch=2,
            grid=(M // tm,),
            in_specs=[pl.BlockSpec((tm, D), lambda i, s1, s2: (i, 0)),
                      pl.BlockSpec(memory_space=pl.ANY)],
            out_specs=pl.BlockSpec(memory_space=pl.ANY),
            scratch_shapes=[pltpu.SemaphoreType.DMA((2,))],
        ),
        input_output_aliases={3: 0},
        compiler_params=_cparams(("arbitrary",)),
    )(slot1, slot2, h2, xs0)


def _moe_expert_kernel(te_ref, nu_ref, xs_ref, w1_ref, w3_ref, w2_ref, ys_ref, xb, acc):
    i = pl.program_id(0)
    j = pl.program_id(1)
    used = i < nu_ref[0]

    @pl.when(j == 0)
    def _():
        acc[...] = jnp.zeros_like(acc)
        xb[...] = xs_ref[...].astype(BF16)

    @pl.when(used)
    def _():
        h = xb[...]
        a = jnp.dot(h, w1_ref[0], preferred_element_type=F32)
        b = jnp.dot(h, w3_ref[0], preferred_element_type=F32)
        acc[...] += jnp.dot((_silu(a) * b).astype(BF16), w2_ref[0], preferred_element_type=F32)

    @pl.when(j == pl.num_programs(1) - 1)
    def _():
        ys_ref[...] = acc[...]


def _moe_experts(tile_expert, n_used, xs, w1, w3, w2):
    P, D = xs.shape
    E, _, F = w1.shape
    tm, tf = MOE_TILE, 256
    nf = F // tf
    fj = lambda i, j, nu: jnp.where(i < nu[0], j, nf - 1)
    return pl.pallas_call(
        _moe_expert_kernel, name="moe_experts",
        out_shape=jax.ShapeDtypeStruct((P, D), F32),
        grid_spec=pltpu.PrefetchScalarGridSpec(
            num_scalar_prefetch=2,
            grid=(P // tm, nf),
            in_specs=[
                pl.BlockSpec((tm, D), lambda i, j, te, nu: (i, 0)),
                pl.BlockSpec((1, D, tf), lambda i, j, te, nu: (te[i], 0, fj(i, j, nu))),
                pl.BlockSpec((1, D, tf), lambda i, j, te, nu: (te[i], 0, fj(i, j, nu))),
                pl.BlockSpec((1, tf, D), lambda i, j, te, nu: (te[i], fj(i, j, nu), 0)),
            ],
            out_specs=pl.BlockSpec((tm, D), lambda i, j, te, nu: (i, 0)),
            scratch_shapes=[pltpu.VMEM((tm, D), BF16), pltpu.VMEM((tm, D), F32)],
        ),
        compiler_params=_cparams(("arbitrary", "arbitrary")),
    )(tile_expert, n_used, xs, w1, w3, w2)


def _moe_combine_kernel(s1_ref, s2_ref, ys_ref, info_ref, x_ref, g2_ref, o_ref, a_buf, b_buf, sem, *, tm):
    base = pl.program_id(0) * tm

    def copies(r):
        return (pltpu.make_async_copy(ys_ref.at[pl.ds(s1_ref[base + r], 1)], a_buf.at[pl.ds(r, 1)], sem.at[0]),
                pltpu.make_async_copy(ys_ref.at[pl.ds(s2_ref[base + r], 1)], b_buf.at[pl.ds(r, 1)], sem.at[1]))

    def start(r, carry):
        for cp in copies(r):
            cp.start()
        return carry

    def wait(r, carry):
        for cp in copies(r):
            cp.wait()
        return carry

    lax.fori_loop(0, tm, start, 0)
    lax.fori_loop(0, tm, wait, 0)
    info = info_ref[...]
    p1 = info[:, 4:5]
    p2 = info[:, 5:6]
    o_ref[...] = x_ref[...] + g2_ref[0] * (p1 * a_buf[...] + p2 * b_buf[...])


def _moe_combine(slot1, slot2, ys, info, x2, g2, seq):
    M, D = x2.shape
    tm = 256
    return pl.pallas_call(
        functools.partial(_moe_combine_kernel, tm=tm), name="moe_combine",
        out_shape=jax.ShapeDtypeStruct((M, D), F32),
        grid_spec=pltpu.PrefetchScalarGridSpec(
            num_scalar_prefetch=2,
            grid=(M // tm,),
            in_specs=[
                pl.BlockSpec(memory_space=pl.ANY),
                pl.BlockSpec((tm, LANE), lambda i, s1, s2: (i, 0)),
                pl.BlockSpec((tm, D), lambda i, s1, s2: (i, 0)),
                pl.BlockSpec((1, 1, D), lambda i, s1, s2: (i * tm // seq, 0, 0)),
            ],
            out_specs=pl.BlockSpec((tm, D), lambda i, s1, s2: (i, 0)),
            scratch_shapes=[pltpu.VMEM((tm, D), F32), pltpu.VMEM((tm, D), F32),
                            pltpu.SemaphoreType.DMA((2,))],
        ),
        compiler_params=_cparams(("arbitrary",)),
    )(slot1, slot2, ys, info, x2, g2)


def _moe(h2, rw, rb, w1, w3, w2, x2, g2, seq):
    M, D = x2.shape
    E = w1.shape[0]
    T = MOE_TILE
    n_rows = 2 * M + E * T
    n_tiles = n_rows // T
    info, counts = _router(h2, rw, rb)

    e1, e2, rank1, rank2 = (info[:, c].astype(jnp.int32) for c in range(4))
    cnt = counts[0, :E].astype(jnp.int32)
    padded = (cnt + T - 1) // T * T
    ends = jnp.cumsum(padded)
    off = ends - padded
    expert_ids = jnp.arange(E, dtype=jnp.int32)
    offset_of = lambda e: jnp.sum(jnp.where(e[:, None] == expert_ids[None, :], off[None, :], 0), axis=1)
    slot1 = offset_of(e1) + rank1
    slot2 = offset_of(e2) + rank2
    n_used = (ends[E - 1] // T).astype(jnp.int32)
    tile_start = jnp.arange(n_tiles, dtype=jnp.int32) * T
    tile_start = jnp.minimum(tile_start, (n_used - 1) * T)
    tile_expert = jnp.sum(tile_start[:, None] >= ends[None, :], axis=1).astype(jnp.int32)

    xs = _moe_dispatch(slot1, slot2, h2, n_rows)
    ys = _moe_experts(tile_expert, n_used.reshape(1), xs, w1, w3, w2)
    return _moe_combine(slot1, slot2, ys, info, x2, g2, seq)


def _w_in_tail(w):
    d = w.shape[0]
    parts = [w[:, C_U:N_IN], w[:, C_FF:C_U], jnp.zeros((d, N_PACK - P_FF - FOX_H), w.dtype)]
    return jnp.concatenate(parts, axis=1).astype(BF16)


def _mixing_layer(x2, mods, p, w_in, layer, batch, seq, h_dtype):
    row = lambda t: t.reshape(1, -1)
    hz = _inproj(x2, row(p["norm1_g"]), mods["sc1"], mods["sh1"], w_in, layer,
                 _w_in_tail(w_in[layer]), seq)

    mu = p["shift_mu"]
    mu3 = mu[:3 * RW_W].reshape(3, RW_W)
    mul = row(mu[3 * RW_W:])
    r, lw, k, v, kk, a, g = _rwkv_prep(
        hz, mu3, mul, row(p["rw_w0"]), p["rw_w2"].astype(BF16), row(p["rw_a0"]),
        p["rw_a2"].astype(BF16), p["rw_g2"].astype(BF16), row(p["rw_k_k"]), row(p["rw_k_a"]), seq)
    hm = lambda t: t.reshape(RW_H, 1, RW_HD)
    o_a = _rwkv_scan(r, lw, k, v, kk, a, g, hm(p["rw_r_k"]), hm(p["rw_lnx_g"]), hm(p["rw_lnx_b"]),
                     batch, seq)

    fb128 = jnp.zeros((1, LANE), F32).at[0, :FOX_H].set(p["fox_fb"])
    kx, qxT, vT = _fox_prep(hz, row(p["fox_qn_g"]), row(p["fox_kn_g"]), fb128, seq)
    o_b = _fox_attn(kx, qxT, vT, batch, seq)

    o_c = _sgu(hz, p["sg_ln_g"], p["sg_ln_b"], p["sg_ws"], jnp.transpose(p["sg_b"]))

    return _outproj(o_a, o_b, o_c, p["w_out"].astype(BF16), x2, mods["g1"], row(p["norm2_g"]),
                    mods["sc2"], mods["sh2"], seq, h_dtype)


def kernel(x, c, ada_w, ada_b, norm1_g, norm2_g, w_in, shift_mu, rw_w0, rw_w2, rw_a0, rw_a2, rw_g2, rw_k_k, rw_k_a, rw_r_k, rw_lnx_g, rw_lnx_b, fox_qn_g, fox_kn_g, fox_fb, sg_ln_g, sg_ln_b, sg_ws, sg_b, w_out, ffn_w1, ffn_w3, ffn_w2, moe_router_w, moe_router_b, moe_w1, moe_w3, moe_w2):
    B, S, D = x.shape
    L = ada_w.shape[0]
    x2 = x.reshape(B * S, D)
    c8 = jnp.zeros((8, D), F32).at[:B].set(c)
    mod = _ada_mod(c8, ada_w, ada_b.reshape(L, 1, N_MOD * D))

    layer_params = dict(
        norm1_g=norm1_g, norm2_g=norm2_g, shift_mu=shift_mu, rw_w0=rw_w0, rw_w2=rw_w2,
        rw_a0=rw_a0, rw_a2=rw_a2, rw_g2=rw_g2, rw_k_k=rw_k_k, rw_k_a=rw_k_a, rw_r_k=rw_r_k,
        rw_lnx_g=rw_lnx_g, rw_lnx_b=rw_lnx_b, fox_qn_g=fox_qn_g, fox_kn_g=fox_kn_g, fox_fb=fox_fb,
        sg_ln_g=sg_ln_g, sg_ln_b=sg_ln_b, sg_ws=sg_ws, sg_b=sg_b, w_out=w_out)

    for l in range(L):
        names = ("sh1", "sc1", "g1", "sh2", "sc2", "g2")
        mods = {n: mod[l, :B, i * D:(i + 1) * D].reshape(B, 1, D) for i, n in enumerate(names)}
        p = {n: t[l] for n, t in layer_params.items()}
        dense = l % 2 == 0
        x2, h2 = _mixing_layer(x2, mods, p, w_in, l, B, S, BF16 if dense else F32)
        j = l // 2
        if dense:
            x2 = _ffn(h2, ffn_w1[j].astype(BF16), ffn_w3[j].astype(BF16), ffn_w2[j].astype(BF16),
                      x2, mods["g2"], S)
        else:
            rw = jnp.zeros((D, LANE), BF16).at[:, :N_EXPERTS].set(moe_router_w[j].astype(BF16))
            rb = jnp.zeros((1, LANE), F32).at[0, :N_EXPERTS].set(moe_router_b[j])
            x2 = _moe(h2, rw, rb, moe_w1[j].astype(BF16), moe_w3[j].astype(BF16),
                      moe_w2[j].astype(BF16), x2, mods["g2"], S)
    return x2.reshape(B, S, D)
```

```python
import functools

import jax
import jax.numpy as jnp
from jax import lax
from jax.experimental import pallas as pl
from jax.experimental.pallas import tpu as pltpu

F32 = jnp.float32
BF16 = jnp.bfloat16

D_MODEL = 2048
DEPTH = 2
RW_HD = 64
RW_W = 768
RW_H = 12
DECAY_LORA = 64
AAA_LORA = 64
GATE_LORA = 128
FOX_HD = 128
FOX_W = 768
FOX_H = 6
SG_GD = 128
SG_W = 512
SG_G = 4
SG_CHUNK = 128
N_EXPERTS = 8
N_MOD = 6
RMS_EPS = 1e-6
LN_EPS = 1e-5
LNX_EPS = 64e-5

C_R = 0
C_RWKV_END = 3 * RW_W + DECAY_LORA + AAA_LORA + GATE_LORA
C_Q = C_RWKV_END
C_FF = C_Q + 3 * FOX_W
C_U = C_FF + FOX_H
C_SV = C_U + SG_W
N_IN = C_SV + SG_W

P_R, P_K, P_V = 0, 768, 1536
P_LORA = 2304
P_Q, P_FK, P_FV = 2560, 3328, 4096
N_MAIN = C_FF
P_U = N_MAIN
P_SV = P_U + SG_W
P_FF = P_SV + SG_W
N_PACK = 6144
N_TAIL = N_PACK - N_MAIN
COL_TILE = 256

LANE = 128
VMEM_LIMIT = 56 * 1024 * 1024


def _cparams(sem):
    return pltpu.CompilerParams(dimension_semantics=sem, vmem_limit_bytes=VMEM_LIMIT)


def _sigmoid(x):
    return 1.0 / (1.0 + jnp.exp(-x))


def _softplus(x):
    return jnp.maximum(x, 0.0) + jnp.log(1.0 + jnp.exp(-jnp.abs(x)))


def _gelu_tanh(x):
    return 0.5 * x * (1.0 + jnp.tanh(0.7978845608028654 * (x + 0.044715 * (x * x * x))))


def _silu(x):
    return x * _sigmoid(x)


def _ada_kernel(c_ref, w_ref, b_ref, o_ref):
    cs = _silu(c_ref[...]).astype(BF16)
    w = w_ref[0].astype(BF16)
    o_ref[0] = jnp.dot(cs, w, preferred_element_type=F32) + b_ref[0]


def _ada_mod(c8, ada_w, ada_b3):
    L, D, N = ada_w.shape
    tn = 1536
    return pl.pallas_call(
        _ada_kernel, name="ada_mod",
        out_shape=jax.ShapeDtypeStruct((L, 8, N), F32),
        grid=(L, N // tn),
        in_specs=[
            pl.BlockSpec((8, D), lambda l, j: (0, 0)),
            pl.BlockSpec((1, D, tn), lambda l, j: (l, 0, j)),
            pl.BlockSpec((1, 1, tn), lambda l, j: (l, 0, j)),
        ],
        out_specs=pl.BlockSpec((1, 8, tn), lambda l, j: (l, 0, j)),
        compiler_params=_cparams(("parallel", "parallel")),
    )(c8, ada_w, ada_b3)


def _col_spec(tm, width, start):
    assert start % width == 0
    return pl.BlockSpec((tm, width), lambda i, *_: (i, start // width))


def _inproj_kernel(x_ref, g_ref, sc_ref, sh_ref, wm_ref, wt_ref, o_ref, h_scr, *, n_main):
    j = pl.program_id(1)

    @pl.when(j == 0)
    def _():
        x = x_ref[...]
        ms = jnp.mean(x * x, axis=-1, keepdims=True)
        y = x * lax.rsqrt(ms + RMS_EPS) * g_ref[...]
        h_scr[...] = (y * (1.0 + sc_ref[0]) + sh_ref[0]).astype(BF16)

    @pl.when(j < n_main)
    def _():
        o_ref[...] = jnp.dot(h_scr[...], wm_ref[0].astype(BF16), preferred_element_type=F32)

    @pl.when(j >= n_main)
    def _():
        o_ref[...] = jnp.dot(h_scr[...], wt_ref[...], preferred_element_type=F32)


def _inproj(x2, g, sc, sh, w_in, layer, w_tail, seq):
    M, D = x2.shape
    tm, tn = min(2048, seq), COL_TILE
    n_main = N_MAIN // tn
    return pl.pallas_call(
        functools.partial(_inproj_kernel, n_main=n_main), name="inproj",
        out_shape=jax.ShapeDtypeStruct((M, N_PACK), F32),
        grid=(M // tm, N_PACK // tn),
        in_specs=[
            pl.BlockSpec((tm, D), lambda i, j: (i, 0), pipeline_mode=pl.Buffered(1)),
            pl.BlockSpec((1, D), lambda i, j: (0, 0)),
            pl.BlockSpec((1, 1, D), lambda i, j: (i * tm // seq, 0, 0)),
            pl.BlockSpec((1, 1, D), lambda i, j: (i * tm // seq, 0, 0)),
            pl.BlockSpec((1, D, tn), lambda i, j: (layer, 0, jnp.minimum(j, n_main - 1))),
            pl.BlockSpec((D, tn), lambda i, j: (0, jnp.maximum(j - n_main, 0))),
        ],
        out_specs=pl.BlockSpec((tm, tn), lambda i, j: (i, j)),
        scratch_shapes=[pltpu.VMEM((tm, D), BF16)],
        compiler_params=_cparams(("parallel", "arbitrary")),
    )(x2, g, sc, sh, w_in, w_tail)


def _rwkv_prep_kernel(zr_ref, zk_ref, zv_ref, zl_ref, pr_ref, pk_ref, pv_ref, pl_ref,
                      mu_ref, mul_ref, w0_ref, w2_ref, a0_ref, a2_ref, g2_ref, kk_ref, ka_ref,
                      r_out, lw_out, k_out, v_out, kk_out, a_out, g_out, *, tm, seq):
    first = (pl.program_id(0) * tm) % seq == 0

    def shift(z_ref, p_ref, mu):
        z = z_ref[...]
        prev_last = jnp.where(first, 0.0, p_ref[7:8, :])
        zp = pltpu.roll(z, 1, 0)
        row = lax.broadcasted_iota(jnp.int32, z.shape, 0)
        zp = jnp.where(row == 0, prev_last, zp)
        return z + (zp - z) * mu

    r = shift(zr_ref, pr_ref, mu_ref[0:1, :])
    k = shift(zk_ref, pk_ref, mu_ref[1:2, :])
    v = shift(zv_ref, pv_ref, mu_ref[2:3, :])
    lo = shift(zl_ref, pl_ref, mul_ref[...])
    wd = lo[:, 0:DECAY_LORA]
    ad = lo[:, DECAY_LORA:DECAY_LORA + AAA_LORA]
    gd = lo[:, DECAY_LORA + AAA_LORA:]

    dec = w0_ref[...] + jnp.dot(jnp.tanh(wd).astype(BF16), w2_ref[...], preferred_element_type=F32)
    w_log = -_softplus(-dec) - 0.5
    lw = -jnp.exp(w_log)
    a = _sigmoid(a0_ref[...] + jnp.dot(ad.astype(BF16), a2_ref[...], preferred_element_type=F32))
    g = jnp.dot(_sigmoid(gd).astype(BF16), g2_ref[...], preferred_element_type=F32)
    kk = k * kk_ref[...]
    km = k * (1.0 + (a - 1.0) * ka_ref[...])

    for h in range(RW_H):
        sl = slice(h * RW_HD, (h + 1) * RW_HD)
        r_out[h] = r[:, sl]
        lw_out[h] = lw[:, sl]
        k_out[h] = km[:, sl]
        v_out[h] = v[:, sl]
        kk_out[h] = kk[:, sl]
        a_out[h] = a[:, sl]
        g_out[h] = g[:, sl]


def _rwkv_prep(hz, mu3, mul, w0, w2, a0, a2, g2, k_k, k_a, seq):
    M = hz.shape[0]
    tm = 256
    prev = lambda i: jnp.maximum(i * (tm // 8) - 1, 0)
    row = lambda n: pl.BlockSpec((1, n), lambda i: (0, 0))
    in_specs = [
        pl.BlockSpec((tm, RW_W), lambda i: (i, P_R // RW_W)),
        pl.BlockSpec((tm, RW_W), lambda i: (i, P_K // RW_W)),
        pl.BlockSpec((tm, RW_W), lambda i: (i, P_V // RW_W)),
        pl.BlockSpec((tm, 256), lambda i: (i, P_LORA // 256)),
        pl.BlockSpec((8, RW_W), lambda i: (prev(i), P_R // RW_W)),
        pl.BlockSpec((8, RW_W), lambda i: (prev(i), P_K // RW_W)),
        pl.BlockSpec((8, RW_W), lambda i: (prev(i), P_V // RW_W)),
        pl.BlockSpec((8, 256), lambda i: (prev(i), P_LORA // 256)),
        pl.BlockSpec((3, RW_W), lambda i: (0, 0)),
        row(256), row(RW_W),
        pl.BlockSpec((DECAY_LORA, RW_W), lambda i: (0, 0)),
        row(RW_W),
        pl.BlockSpec((AAA_LORA, RW_W), lambda i: (0, 0)),
        pl.BlockSpec((GATE_LORA, RW_W), lambda i: (0, 0)),
        row(RW_W), row(RW_W),
    ]
    hm = jax.ShapeDtypeStruct((RW_H, M, RW_HD), F32)
    hm_spec = pl.BlockSpec((RW_H, tm, RW_HD), lambda i: (0, i, 0))
    return pl.pallas_call(
        functools.partial(_rwkv_prep_kernel, tm=tm, seq=seq), name="rwkv_prep",
        out_shape=[hm] * 7,
        grid=(M // tm,),
        in_specs=in_specs,
        out_specs=[hm_spec] * 7,
        compiler_params=_cparams(("parallel",)),
    )(hz, hz, hz, hz, hz, hz, hz, hz, mu3, mul, w0, w2, a0, a2, g2, k_k, k_a)


def _bmm(a, b, ca, cb):
    return lax.dot_general(a.astype(BF16), b.astype(BF16), (((ca,), (cb,)), ((0,), (0,))),
                           preferred_element_type=F32)


def _split3(x):
    hi = x.astype(BF16)
    r1 = x - hi.astype(F32)
    mid = r1.astype(BF16)
    lo = (r1 - mid.astype(F32)).astype(BF16)
    return hi, mid, lo


def _rwkv_scan_kernel(r_ref, lw_ref, k_ref, v_ref, kk_ref, a_ref, g_ref, rk_ref, lng_ref, lnb_ref,
                      o_ref, s_scr, *, L, nC):
    H, K = RW_H, RW_HD
    n = H * nC

    @pl.when(pl.program_id(1) == 0)
    def _():
        s_scr[...] = jnp.zeros_like(s_scr)

    ld = lambda ref: ref[...].reshape(n, L, K)
    r, lw, k, v, kk, a = ld(r_ref), ld(lw_ref), ld(k_ref), ld(v_ref), ld(kk_ref), ld(a_ref)

    kk = kk / jnp.maximum(jnp.sqrt(jnp.sum(kk * kk, axis=-1, keepdims=True)), 1e-12)
    b = kk * a

    ti = lax.broadcasted_iota(jnp.int32, (L, L), 0)
    si = lax.broadcasted_iota(jnp.int32, (L, L), 1)
    tri_incl = jnp.broadcast_to((ti >= si).astype(BF16), (n, L, L))
    cw = sum(_bmm(tri_incl, piece, 2, 1) for piece in reversed(_split3(lw)))
    cw_last = cw[:, L - 1:L, :]
    e_in = jnp.exp(cw)
    e_out = jnp.exp(-cw)
    e_ex = jnp.exp(cw - lw)
    e_end = jnp.exp(cw_last - cw)
    w_end = jnp.exp(cw_last)

    at = -kk * e_ex
    rt = r * e_in
    bb = b * e_out
    kb = k * e_out
    bh = b * e_end
    kh = k * e_end

    atrt = jnp.concatenate([at, rt], axis=1)
    t2 = lax.broadcasted_iota(jnp.int32, (2 * L, L), 0)
    s2 = lax.broadcasted_iota(jnp.int32, (2 * L, L), 1)
    causal = (((t2 & (L - 1)) - s2 + jnp.where(t2 < L, 0, 1)) > 0)[None]
    pb = jnp.where(causal, _bmm(atrt, bb, 2, 2), 0.0)
    pk = jnp.where(causal, _bmm(atrt, kb, 2, 2), 0.0)
    a_ab, b_rb = pb[:, :L], pb[:, L:]
    a_ak, b_rk = pk[:, :L], pk[:, L:]

    x = jnp.broadcast_to((ti == si).astype(F32), (n, L, L))
    size = 1
    while size < L:
        sh = size.bit_length() - 1
        m = (((ti >> (sh + 1)) == (si >> (sh + 1))) & (((ti >> sh) & 1) == 1) & (((si >> sh) & 1) == 0))[None]
        x = x + _bmm(_bmm(x, jnp.where(m, a_ab, 0.0), 2, 1), x, 2, 1)
        size *= 2

    xu = _bmm(x, jnp.concatenate([at, _bmm(a_ak, v, 2, 1)], axis=-1), 2, 1)
    w = _bmm(b_rb, xu, 2, 1)
    rp = rt + w[..., :K]
    y0 = w[..., K:] + _bmm(b_rk, v, 2, 1)
    big = _bmm(xu, bh, 1, 1)
    gp = big[:, :K]
    cc = big[:, K:] + _bmm(v, kh, 1, 1)

    c4 = lambda t: t.reshape((H, nC) + t.shape[1:])
    rp, y0, gp, cc, w_end = c4(rp), c4(y0), c4(gp), c4(cc), c4(w_end)
    s = s_scr[...]
    ys = []
    for c in range(nC):
        ys.append(_bmm(rp[:, c], s, 2, 2) + y0[:, c])
        s = s * w_end[:, c] + _bmm(s, gp[:, c], 2, 1) + cc[:, c]
    s_scr[...] = s
    y = ys[0] if nC == 1 else jnp.concatenate(ys, axis=1)

    mu = jnp.mean(y, axis=-1, keepdims=True)
    yc = y - mu
    var = jnp.mean(yc * yc, axis=-1, keepdims=True)
    yn = yc * lax.rsqrt(var + LNX_EPS) * lng_ref[...] + lnb_ref[...]
    r3, k3, v3 = r_ref[...], k_ref[...], v_ref[...]
    bonus = jnp.sum(r3 * k3 * rk_ref[...], axis=-1, keepdims=True) * v3
    out = (yn + bonus) * g_ref[...]
    for p in range(H // 2):
        pair = jnp.concatenate([out[2 * p], out[2 * p + 1]], axis=-1)
        o_ref[:, p * LANE:(p + 1) * LANE] = pair.astype(o_ref.dtype)


def _rwkv_scan(r, lw, k, v, kk, a, g, r_k, lnx_g, lnx_b, batch, seq):
    M = r.shape[1]
    L, nC = 64, 2
    tb = L * nC
    nj = seq // tb
    hm_spec = pl.BlockSpec((RW_H, tb, RW_HD), lambda b, j: (0, b * nj + j, 0))
    par = pl.BlockSpec((RW_H, 1, RW_HD), lambda b, j: (0, 0, 0))
    return pl.pallas_call(
        functools.partial(_rwkv_scan_kernel, L=L, nC=nC), name="rwkv_scan",
        out_shape=jax.ShapeDtypeStruct((M, RW_W), BF16),
        grid=(batch, nj),
        in_specs=[hm_spec] * 7 + [par] * 3,
        out_specs=pl.BlockSpec((tb, RW_W), lambda b, j: (b * nj + j, 0)),
        scratch_shapes=[pltpu.VMEM((RW_H, RW_HD, RW_HD), F32)],
        compiler_params=_cparams(("parallel", "arbitrary")),
    )(r, lw, k, v, kk, a, g, r_k, lnx_g, lnx_b)


FOX_XD = 2 * FOX_HD


def _fox_prep_kernel(*refs, tm, seq):
    q_refs, k_refs, v_refs = refs[0:3], refs[3:6], refs[6:9]
    f_ref, qg_ref, kg_ref, fb_ref, kx_ref, qxT_ref, vT_ref, carry = refs[9:]

    @pl.when((pl.program_id(0) * tm) % seq == 0)
    def _():
        carry[...] = jnp.zeros_like(carry)

    logf = -_softplus(-(f_ref[...] + fb_ref[...]))
    ti = lax.broadcasted_iota(jnp.int32, (tm, tm), 0)
    si = lax.broadcasted_iota(jnp.int32, (tm, tm), 1)
    tri = (ti >= si).astype(BF16)
    cum = carry[...]
    for piece in reversed(_split3(logf)):
        cum = cum + jnp.dot(tri, piece, preferred_element_type=F32)
    carry[...] = cum[tm - 1:tm, :]
    cum_t = cum.T

    scale = FOX_HD ** -0.5
    lane = lax.broadcasted_iota(jnp.int32, (tm, FOX_HD), 1)
    sub = lax.broadcasted_iota(jnp.int32, (FOX_HD, tm), 0)
    pieces = lambda x: [t.astype(F32) for t in _split3(x)]
    for h in range(FOX_H):
        blk, half = h // 2, slice((h % 2) * FOX_HD, (h % 2 + 1) * FOX_HD)
        q = q_refs[blk][:, half]
        k = k_refs[blk][:, half]
        qn = q * lax.rsqrt(jnp.mean(q * q, axis=-1, keepdims=True) + RMS_EPS) * qg_ref[...]
        kn = k * lax.rsqrt(jnp.mean(k * k, axis=-1, keepdims=True) + RMS_EPS) * kg_ref[...]

        c_hi, c_mid, c_lo = pieces(cum[:, h:h + 1])
        k_extra = jnp.where(lane < 3, 1.0, 0.0)
        k_extra = jnp.where(lane == 3, -c_hi, k_extra)
        k_extra = jnp.where(lane == 4, -c_mid, k_extra)
        k_extra = jnp.where(lane == 5, -c_lo, k_extra)
        kx_ref[:, h * FOX_XD:h * FOX_XD + FOX_HD] = kn.astype(BF16)
        kx_ref[:, h * FOX_XD + FOX_HD:(h + 1) * FOX_XD] = k_extra.astype(BF16)

        r_hi, r_mid, r_lo = pieces(cum_t[h:h + 1, :])
        q_extra = jnp.where(sub < 6, 1.0, 0.0)
        q_extra = jnp.where(sub == 0, r_hi, q_extra)
        q_extra = jnp.where(sub == 1, r_mid, q_extra)
        q_extra = jnp.where(sub == 2, r_lo, q_extra)
        qxT_ref[h * FOX_XD:h * FOX_XD + FOX_HD, :] = (qn * scale).T.astype(BF16)
        qxT_ref[h * FOX_XD + FOX_HD:(h + 1) * FOX_XD, :] = q_extra.astype(BF16)

        vT_ref[h * FOX_HD:(h + 1) * FOX_HD, :] = v_refs[blk][:, half].T.astype(BF16)


def _fox_prep(hz, qn_g, kn_g, fb128, seq):
    M = hz.shape[0]
    tm = 256
    row = pl.BlockSpec((1, LANE), lambda i: (0, 0))
    pieces = lambda start: [_col_spec(tm, COL_TILE, start + c * COL_TILE) for c in range(FOX_W // COL_TILE)]
    return pl.pallas_call(
        functools.partial(_fox_prep_kernel, tm=tm, seq=seq), name="fox_prep",
        out_shape=[jax.ShapeDtypeStruct((M, FOX_H * FOX_XD), BF16),
                   jax.ShapeDtypeStruct((FOX_H * FOX_XD, M), BF16),
                   jax.ShapeDtypeStruct((FOX_W, M), BF16)],
        grid=(M // tm,),
        in_specs=pieces(P_Q) + pieces(P_FK) + pieces(P_FV) + [_col_spec(tm, LANE, P_FF), row, row, row],
        out_specs=[pl.BlockSpec((tm, FOX_H * FOX_XD), lambda i: (i, 0)),
                   pl.BlockSpec((FOX_H * FOX_XD, tm), lambda i: (0, i)),
                   pl.BlockSpec((FOX_W, tm), lambda i: (0, i))],
        scratch_shapes=[pltpu.VMEM((1, LANE), F32)],
        compiler_params=_cparams(("arbitrary",)),
    )(*([hz] * 10), qn_g, kn_g, fb128)


def _fox_attn_kernel(kx_ref, qxT_ref, vT_ref, o_ref, m_scr, l_scr, acc_scr, *, t):
    qi = pl.program_id(1)
    ki = pl.program_id(2)

    @pl.when(ki == 0)
    def _():
        m_scr[...] = jnp.full_like(m_scr, -1e30)
        l_scr[...] = jnp.zeros_like(l_scr)
        acc_scr[...] = jnp.zeros_like(acc_scr)

    def step(diagonal):
        if diagonal:
            keep = (lax.broadcasted_iota(jnp.int32, (t, t), 1)
                    >= lax.broadcasted_iota(jnp.int32, (t, t), 0))
        for h in range(FOX_H):
            xs = slice(h * FOX_XD, (h + 1) * FOX_XD)
            hs = slice(h * FOX_HD, (h + 1) * FOX_HD)
            s = jnp.dot(kx_ref[:, xs], qxT_ref[xs, :], preferred_element_type=F32)
            if diagonal:
                s = jnp.where(keep, s, -1e30)
            m_prev = m_scr[h]
            m_new = jnp.maximum(m_prev, jnp.max(s, axis=0, keepdims=True))
            alpha = jnp.exp(m_prev - m_new)
            p = jnp.exp(s - m_new)
            l_new = alpha * l_scr[h] + jnp.sum(p, axis=0, keepdims=True)
            acc = alpha * acc_scr[hs, :] + jnp.dot(vT_ref[hs, :], p.astype(BF16),
                                                   preferred_element_type=F32)
            if diagonal:
                o_ref[:, hs] = (acc / l_new).T.astype(o_ref.dtype)
            else:
                m_scr[h] = m_new
                l_scr[h] = l_new
                acc_scr[hs, :] = acc

    pl.when(ki < qi)(lambda: step(False))
    pl.when(ki == qi)(lambda: step(True))


def _fox_attn(kx, qxT, vT, batch, seq):
    M = kx.shape[0]
    t = 512
    nt = seq // t
    kidx = lambda qi, ki: jnp.minimum(ki, qi)
    return pl.pallas_call(
        functools.partial(_fox_attn_kernel, t=t), name="fox_attn",
        out_shape=jax.ShapeDtypeStruct((M, FOX_W), BF16),
        grid=(batch, nt, nt),
        in_specs=[
            pl.BlockSpec((t, FOX_H * FOX_XD), lambda b, qi, ki: (b * nt + kidx(qi, ki), 0)),
            pl.BlockSpec((FOX_H * FOX_XD, t), lambda b, qi, ki: (0, b * nt + qi)),
            pl.BlockSpec((FOX_W, t), lambda b, qi, ki: (0, b * nt + kidx(qi, ki))),
        ],
        out_specs=pl.BlockSpec((t, FOX_W), lambda b, qi, ki: (b * nt + qi, 0)),
        scratch_shapes=[pltpu.VMEM((FOX_H, 1, t), F32), pltpu.VMEM((FOX_H, 1, t), F32),
                        pltpu.VMEM((FOX_W, t), F32)],
        compiler_params=_cparams(("parallel", "parallel", "arbitrary")),
    )(kx, qxT, vT)


def _sgu_kernel(u0_ref, u1_ref, v0_ref, v1_ref, lg_ref, lb_ref, ws_ref, sb_ref, o_ref, *, tm):
    ti = lax.broadcasted_iota(jnp.int32, (SG_CHUNK, SG_CHUNK), 0)
    si = lax.broadcasted_iota(jnp.int32, (SG_CHUNK, SG_CHUNK), 1)
    u_refs, v_refs = (u0_ref, u1_ref), (v0_ref, v1_ref)
    for g in range(SG_G):
        sl = slice(g * SG_GD, (g + 1) * SG_GD)
        half = slice((g % 2) * SG_GD, (g % 2 + 1) * SG_GD)
        u = _gelu_tanh(u_refs[g // 2][:, half])
        v = _gelu_tanh(v_refs[g // 2][:, half])
        mu = jnp.mean(v, axis=-1, keepdims=True)
        vc = v - mu
        var = jnp.mean(vc * vc, axis=-1, keepdims=True)
        vn = (vc * lax.rsqrt(var + LN_EPS) * lg_ref[g:g + 1, :] + lb_ref[g:g + 1, :]).astype(BF16)
        ws = jnp.where(ti >= si, ws_ref[g], 0.0).astype(BF16)
        bias = sb_ref[:, g:g + 1]
        for c in range(tm // SG_CHUNK):
            rows = slice(c * SG_CHUNK, (c + 1) * SG_CHUNK)
            mixed = jnp.dot(ws, vn[rows], preferred_element_type=F32) + bias
            o_ref[rows, sl] = (u[rows] * mixed).astype(o_ref.dtype)


def _sgu(hz, ln_g, ln_b, ws, sb_t):
    M = hz.shape[0]
    tm = 512
    return pl.pallas_call(
        functools.partial(_sgu_kernel, tm=tm), name="sgu",
        out_shape=jax.ShapeDtypeStruct((M, SG_W), BF16),
        grid=(M // tm,),
        in_specs=[
            _col_spec(tm, COL_TILE, P_U), _col_spec(tm, COL_TILE, P_U + COL_TILE),
            _col_spec(tm, COL_TILE, P_SV), _col_spec(tm, COL_TILE, P_SV + COL_TILE),
            pl.BlockSpec((SG_G, SG_GD), lambda i: (0, 0)),
            pl.BlockSpec((SG_G, SG_GD), lambda i: (0, 0)),
            pl.BlockSpec((SG_G, SG_CHUNK, SG_CHUNK), lambda i: (0, 0, 0)),
            pl.BlockSpec((SG_CHUNK, SG_G), lambda i: (0, 0)),
        ],
        out_specs=pl.BlockSpec((tm, SG_W), lambda i: (i, 0)),
        compiler_params=_cparams(("parallel",)),
    )(hz, hz, hz, hz, ln_g, ln_b, ws, sb_t)


def _outproj_kernel(oa_ref, ob_ref, oc_ref, w_ref, x_ref, g1_ref, ng_ref, sc_ref, sh_ref,
                    xo_ref, h_ref):
    mix = jnp.dot(oa_ref[...], w_ref[0:RW_W, :], preferred_element_type=F32)
    mix += jnp.dot(ob_ref[...], w_ref[RW_W:RW_W + FOX_W, :], preferred_element_type=F32)
    mix += jnp.dot(oc_ref[...], w_ref[RW_W + FOX_W:, :], preferred_element_type=F32)
    x = x_ref[...] + g1_ref[0] * mix
    xo_ref[...] = x
    ms = jnp.mean(x * x, axis=-1, keepdims=True)
    y = x * lax.rsqrt(ms + RMS_EPS) * ng_ref[...]
    h_ref[...] = (y * (1.0 + sc_ref[0]) + sh_ref[0]).astype(h_ref.dtype)


def _outproj(oa, ob, oc, w, x2, g1, ng, sc, sh, seq, h_dtype):
    M, D = x2.shape
    tm = 512
    bidx = lambda i: (i * tm // seq, 0, 0)
    mod = pl.BlockSpec((1, 1, D), bidx)
    return pl.pallas_call(
        _outproj_kernel, name="outproj",
        out_shape=[jax.ShapeDtypeStruct((M, D), F32), jax.ShapeDtypeStruct((M, D), h_dtype)],
        grid=(M // tm,),
        in_specs=[
            pl.BlockSpec((tm, RW_W), lambda i: (i, 0)),
            pl.BlockSpec((tm, FOX_W), lambda i: (i, 0)),
            pl.BlockSpec((tm, SG_W), lambda i: (i, 0)),
            pl.BlockSpec((D, D), lambda i: (0, 0)),
            pl.BlockSpec((tm, D), lambda i: (i, 0)),
            mod,
            pl.BlockSpec((1, D), lambda i: (0, 0)),
            mod, mod,
        ],
        out_specs=[pl.BlockSpec((tm, D), lambda i: (i, 0)), pl.BlockSpec((tm, D), lambda i: (i, 0))],
        compiler_params=_cparams(("parallel",)),
    )(oa, ob, oc, w, x2, g1, ng, sc, sh)


def _swiglu_partial(h, w1_ref, w3_ref, w2_ref):
    a = jnp.dot(h, w1_ref[0].astype(BF16), preferred_element_type=F32)
    b = jnp.dot(h, w3_ref[0].astype(BF16), preferred_element_type=F32)
    return jnp.dot((_silu(a) * b).astype(BF16), w2_ref[0].astype(BF16), preferred_element_type=F32)


def _ffn_kernel(h_ref, w1_ref, w3_ref, w2_ref, x_ref, g2_ref, o_ref):
    j = pl.program_id(1)
    part = _swiglu_partial(h_ref[...], w1_ref, w3_ref, w2_ref)

    @pl.when(j == 0)
    def _():
        o_ref[...] = part

    @pl.when(j > 0)
    def _():
        o_ref[...] += part

    @pl.when(j == pl.num_programs(1) - 1)
    def _():
        o_ref[...] = x_ref[...] + g2_ref[0] * o_ref[...]


def _ffn(h2, w1, w3, w2, layer, x2, g2, seq):
    M, D = x2.shape
    F = w1.shape[2]
    tm, tf = 1024, 256
    once = dict(pipeline_mode=pl.Buffered(1))
    mod = pl.BlockSpec((1, 1, D), lambda i, j: (i * tm // seq, 0, 0))
    return pl.pallas_call(
        _ffn_kernel, name="ffn",
        out_shape=jax.ShapeDtypeStruct((M, D), F32),
        grid=(M // tm, F // tf),
        in_specs=[
            pl.BlockSpec((tm, D), lambda i, j: (i, 0), **once),
            pl.BlockSpec((1, D, tf), lambda i, j: (layer, 0, j)),
            pl.BlockSpec((1, D, tf), lambda i, j: (layer, 0, j)),
            pl.BlockSpec((1, tf, D), lambda i, j: (layer, j, 0)),
            pl.BlockSpec((tm, D), lambda i, j: (i, 0), **once),
            mod,
        ],
        out_specs=pl.BlockSpec((tm, D), lambda i, j: (i, 0)),
        compiler_params=_cparams(("parallel", "arbitrary")),
    )(h2, w1, w3, w2, x2, g2)


def _router_kernel(h_ref, w_ref, b_ref, info_ref, cnt_ref, carry, *, tm):
    @pl.when(pl.program_id(0) == 0)
    def _():
        carry[...] = jnp.zeros_like(carry)

    logits = jnp.dot(h_ref[...].astype(BF16), w_ref[...], preferred_element_type=F32) + b_ref[...]
    lane = lax.broadcasted_iota(jnp.int32, logits.shape, 1)
    neg = -1e30
    logits = jnp.where(lane < N_EXPERTS, logits, neg)
    m1 = jnp.max(logits, axis=-1, keepdims=True)
    i1 = jnp.min(jnp.where(logits == m1, lane, LANE), axis=-1, keepdims=True)
    rest = jnp.where(lane == i1, neg, logits)
    m2 = jnp.max(rest, axis=-1, keepdims=True)
    i2 = jnp.min(jnp.where(rest == m2, lane, LANE), axis=-1, keepdims=True)
    e2 = jnp.exp(m2 - m1)
    p1 = 1.0 / (1.0 + e2)
    p2 = e2 / (1.0 + e2)

    oh1 = (lane == i1).astype(F32)
    oh2 = (lane == i2).astype(F32)
    both = oh1 + oh2
    ti = lax.broadcasted_iota(jnp.int32, (tm, tm), 0)
    si = lax.broadcasted_iota(jnp.int32, (tm, tm), 1)
    strict = (ti > si).astype(BF16)
    before = jnp.dot(strict, both.astype(BF16), preferred_element_type=F32) + carry[...]
    rank1 = jnp.sum(oh1 * before, axis=-1, keepdims=True)
    rank2 = jnp.sum(oh2 * before, axis=-1, keepdims=True)
    total = carry[...] + jnp.sum(both, axis=0, keepdims=True)
    carry[...] = total
    cnt_ref[...] = total

    info = jnp.where(lane == 0, i1.astype(F32), 0.0)
    info = jnp.where(lane == 1, i2.astype(F32), info)
    info = jnp.where(lane == 2, rank1, info)
    info = jnp.where(lane == 3, rank2, info)
    info = jnp.where(lane == 4, p1, info)
    info = jnp.where(lane == 5, p2, info)
    info_ref[...] = info


def _router(h2, rw, rb):
    M, D = h2.shape
    tm = 512
    return pl.pallas_call(
        functools.partial(_router_kernel, tm=tm), name="router",
        out_shape=[jax.ShapeDtypeStruct((M, LANE), F32), jax.ShapeDtypeStruct((1, LANE), F32)],
        grid=(M // tm,),
        in_specs=[
            pl.BlockSpec((tm, D), lambda i: (i, 0)),
            pl.BlockSpec((D, LANE), lambda i: (0, 0)),
            pl.BlockSpec((1, LANE), lambda i: (0, 0)),
        ],
        out_specs=[pl.BlockSpec((tm, LANE), lambda i: (i, 0)), pl.BlockSpec((1, LANE), lambda i: (0, 0))],
        scratch_shapes=[pltpu.VMEM((1, LANE), F32)],
        compiler_params=_cparams(("arbitrary",)),
    )(h2, rw, rb)


MOE_TILE = 512


def _moe_dispatch_kernel(s1_ref, s2_ref, h_ref, xs_in_ref, xs_ref, sem, *, tm):
    del xs_in_ref
    base = pl.program_id(0) * tm

    def copies(r):
        src = h_ref.at[pl.ds(r, 1)]
        return (pltpu.make_async_copy(src, xs_ref.at[pl.ds(s1_ref[base + r], 1)], sem.at[0]),
                pltpu.make_async_copy(src, xs_ref.at[pl.ds(s2_ref[base + r], 1)], sem.at[1]))

    def start(r, carry):
        for cp in copies(r):
            cp.start()
        return carry

    def wait(r, carry):
        for cp in copies(r):
            cp.wait()
        return carry

    lax.fori_loop(0, tm, start, 0)
    lax.fori_loop(0, tm, wait, 0)


def _moe_dispatch(slot1, slot2, h2, n_rows):
    M, D = h2.shape
    tm = 256
    xs0 = jnp.zeros((n_rows, D), F32)
    return pl.pallas_call(
        functools.partial(_moe_dispatch_kernel, tm=tm), name="moe_dispatch",
        out_shape=jax.ShapeDtypeStruct((n_rows, D), F32),
        grid_spec=pltpu.PrefetchScalarGridSpec(
            num_scalar_prefetch=2,
            grid=(M // tm,),
            in_specs=[pl.BlockSpec((tm, D), lambda i, s1, s2: (i, 0)),
                      pl.BlockSpec(memory_space=pl.ANY)],
            out_specs=pl.BlockSpec(memory_space=pl.ANY),
            scratch_shapes=[pltpu.SemaphoreType.DMA((2,))],
        ),
        input_output_aliases={3: 0},
        compiler_params=_cparams(("arbitrary",)),
    )(slot1, slot2, h2, xs0)


def _moe_expert_kernel(te_ref, nu_ref, xs_ref, w1_ref, w3_ref, w2_ref, ys_ref, xb):
    i = pl.program_id(0)
    j = pl.program_id(1)

    @pl.when(j == 0)
    def _():
        ys_ref[...] = jnp.zeros_like(ys_ref)
        xb[...] = xs_ref[...].astype(BF16)

    @pl.when(i < nu_ref[0])
    def _():
        ys_ref[...] += _swiglu_partial(xb[...], w1_ref, w3_ref, w2_ref)


def _moe_experts(tile_expert, n_used, xs, w1, w3, w2):
    P, D = xs.shape
    E, _, F = w1.shape
    tm, tf = MOE_TILE, 256
    nf = F // tf
    fj = lambda i, j, nu: jnp.where(i < nu[0], j, nf - 1)
    return pl.pallas_call(
        _moe_expert_kernel, name="moe_experts",
        out_shape=jax.ShapeDtypeStruct((P, D), F32),
        grid_spec=pltpu.PrefetchScalarGridSpec(
            num_scalar_prefetch=2,
            grid=(P // tm, nf),
            in_specs=[
                pl.BlockSpec((tm, D), lambda i, j, te, nu: (i, 0)),
                pl.BlockSpec((1, D, tf), lambda i, j, te, nu: (te[i], 0, fj(i, j, nu))),
                pl.BlockSpec((1, D, tf), lambda i, j, te, nu: (te[i], 0, fj(i, j, nu))),
                pl.BlockSpec((1, tf, D), lambda i, j, te, nu: (te[i], fj(i, j, nu), 0)),
            ],
            out_specs=pl.BlockSpec((tm, D), lambda i, j, te, nu: (i, 0)),
            scratch_shapes=[pltpu.VMEM((tm, D), BF16)],
        ),
        compiler_params=_cparams(("arbitrary", "arbitrary")),
    )(tile_expert, n_used, xs, w1, w3, w2)


def _moe_combine_kernel(s1_ref, s2_ref, ys_ref, info_ref, x_ref, g2_ref, o_ref, a_buf, b_buf, sem, *, tm):
    base = pl.program_id(0) * tm

    def copies(r):
        return (pltpu.make_async_copy(ys_ref.at[pl.ds(s1_ref[base + r], 1)], a_buf.at[pl.ds(r, 1)], sem.at[0]),
                pltpu.make_async_copy(ys_ref.at[pl.ds(s2_ref[base + r], 1)], b_buf.at[pl.ds(r, 1)], sem.at[1]))

    def start(r, carry):
        for cp in copies(r):
            cp.start()
        return carry

    def wait(r, carry):
        for cp in copies(r):
            cp.wait()
        return carry

    lax.fori_loop(0, tm, start, 0)
    lax.fori_loop(0, tm, wait, 0)
    info = info_ref[...]
    p1 = info[:, 4:5]
    p2 = info[:, 5:6]
    o_ref[...] = x_ref[...] + g2_ref[0] * (p1 * a_buf[...] + p2 * b_buf[...])


def _moe_combine(slot1, slot2, ys, info, x2, g2, seq):
    M, D = x2.shape
    tm = 256
    return pl.pallas_call(
        functools.partial(_moe_combine_kernel, tm=tm), name="moe_combine",
        out_shape=jax.ShapeDtypeStruct((M, D), F32),
        grid_spec=pltpu.PrefetchScalarGridSpec(
            num_scalar_prefetch=2,
            grid=(M // tm,),
            in_specs=[
                pl.BlockSpec(memory_space=pl.ANY),
                pl.BlockSpec((tm, LANE), lambda i, s1, s2: (i, 0)),
                pl.BlockSpec((tm, D), lambda i, s1, s2: (i, 0)),
                pl.BlockSpec((1, 1, D), lambda i, s1, s2: (i * tm // seq, 0, 0)),
            ],
            out_specs=pl.BlockSpec((tm, D), lambda i, s1, s2: (i, 0)),
            scratch_shapes=[pltpu.VMEM((tm, D), F32), pltpu.VMEM((tm, D), F32),
                            pltpu.SemaphoreType.DMA((2,))],
        ),
        compiler_params=_cparams(("arbitrary",)),
    )(slot1, slot2, ys, info, x2, g2)


def _moe(h2, rw, rb, w1, w3, w2, x2, g2, seq):
    M, D = x2.shape
    E = w1.shape[0]
    T = MOE_TILE
    n_rows = 2 * M + E * T
    n_tiles = n_rows // T
    info, counts = _router(h2, rw, rb)

    e1, e2, rank1, rank2 = (info[:, c].astype(jnp.int32) for c in range(4))
    cnt = counts[0, :E].astype(jnp.int32)
    padded = (cnt + T - 1) // T * T
    ends = jnp.cumsum(padded)
    off = ends - padded
    expert_ids = jnp.arange(E, dtype=jnp.int32)
    offset_of = lambda e: jnp.sum(jnp.where(e[:, None] == expert_ids[None, :], off[None, :], 0), axis=1)
    slot1 = offset_of(e1) + rank1
    slot2 = offset_of(e2) + rank2
    n_used = (ends[E - 1] // T).astype(jnp.int32)
    tile_start = jnp.arange(n_tiles, dtype=jnp.int32) * T
    tile_start = jnp.minimum(tile_start, (n_used - 1) * T)
    tile_expert = jnp.sum(tile_start[:, None] >= ends[None, :], axis=1).astype(jnp.int32)

    xs = _moe_dispatch(slot1, slot2, h2, n_rows)
    ys = _moe_experts(tile_expert, n_used.reshape(1), xs, w1, w3, w2)
    return _moe_combine(slot1, slot2, ys, info, x2, g2, seq)


TAIL_BLOCK = N_MAIN // 2


def _w_tail_kernel(w_ref, o_ref):
    n_sg = 2 * SG_W
    t = w_ref[0][:, :n_sg + LANE]
    shifted = pltpu.roll(t, n_sg + LANE - FOX_H, 1)
    o_ref[:, :n_sg] = shifted[:, :n_sg].astype(BF16)
    lane = lax.broadcasted_iota(jnp.int32, (t.shape[0], LANE), 1)
    o_ref[:, n_sg:n_sg + LANE] = jnp.where(lane < FOX_H, t[:, :LANE], 0.0).astype(BF16)
    o_ref[:, n_sg + LANE:] = jnp.zeros((t.shape[0], N_TAIL - n_sg - LANE), BF16)


def _w_in_tail(w_in, layer):
    L, D, _ = w_in.shape
    assert 2 * TAIL_BLOCK == N_MAIN and N_IN - N_MAIN <= 2 * SG_W + LANE <= TAIL_BLOCK
    tr = 256
    return pl.pallas_call(
        _w_tail_kernel, name="w_tail",
        out_shape=jax.ShapeDtypeStruct((D, N_TAIL), BF16),
        grid=(D // tr,),
        in_specs=[pl.BlockSpec((1, tr, TAIL_BLOCK), lambda i: (layer, i, 2))],
        out_specs=pl.BlockSpec((tr, N_TAIL), lambda i: (i, 0)),
        compiler_params=_cparams(("parallel",)),
    )(w_in)


def _mixing_layer(x2, mods, p, w_in, layer, batch, seq, h_dtype):
    row = lambda t: t.reshape(1, -1)
    hz = _inproj(x2, row(p["norm1_g"]), mods["sc1"], mods["sh1"], w_in, layer,
                 _w_in_tail(w_in, layer), seq)

    mu = p["shift_mu"]
    mu3 = mu[:3 * RW_W].reshape(3, RW_W)
    mul = row(mu[3 * RW_W:])
    r, lw, k, v, kk, a, g = _rwkv_prep(
        hz, mu3, mul, row(p["rw_w0"]), p["rw_w2"].astype(BF16), row(p["rw_a0"]),
        p["rw_a2"].astype(BF16), p["rw_g2"].astype(BF16), row(p["rw_k_k"]), row(p["rw_k_a"]), seq)
    hm = lambda t: t.reshape(RW_H, 1, RW_HD)
    o_a = _rwkv_scan(r, lw, k, v, kk, a, g, hm(p["rw_r_k"]), hm(p["rw_lnx_g"]), hm(p["rw_lnx_b"]),
                     batch, seq)

    fb128 = jnp.zeros((1, LANE), F32).at[0, :FOX_H].set(p["fox_fb"])
    kx, qxT, vT = _fox_prep(hz, row(p["fox_qn_g"]), row(p["fox_kn_g"]), fb128, seq)
    o_b = _fox_attn(kx, qxT, vT, batch, seq)

    o_c = _sgu(hz, p["sg_ln_g"], p["sg_ln_b"], p["sg_ws"], jnp.transpose(p["sg_b"]))

    return _outproj(o_a, o_b, o_c, p["w_out"].astype(BF16), x2, mods["g1"], row(p["norm2_g"]),
                    mods["sc2"], mods["sh2"], seq, h_dtype)


def kernel(x, c, ada_w, ada_b, norm1_g, norm2_g, w_in, shift_mu, rw_w0, rw_w2, rw_a0, rw_a2, rw_g2, rw_k_k, rw_k_a, rw_r_k, rw_lnx_g, rw_lnx_b, fox_qn_g, fox_kn_g, fox_fb, sg_ln_g, sg_ln_b, sg_ws, sg_b, w_out, ffn_w1, ffn_w3, ffn_w2, moe_router_w, moe_router_b, moe_w1, moe_w3, moe_w2):
    B, S, D = x.shape
    L = ada_w.shape[0]
    x2 = x.reshape(B * S, D)
    c8 = jnp.zeros((8, D), F32).at[:B].set(c)
    mod = _ada_mod(c8, ada_w, ada_b.reshape(L, 1, N_MOD * D))

    layer_params = dict(
        norm1_g=norm1_g, norm2_g=norm2_g, shift_mu=shift_mu, rw_w0=rw_w0, rw_w2=rw_w2,
        rw_a0=rw_a0, rw_a2=rw_a2, rw_g2=rw_g2, rw_k_k=rw_k_k, rw_k_a=rw_k_a, rw_r_k=rw_r_k,
        rw_lnx_g=rw_lnx_g, rw_lnx_b=rw_lnx_b, fox_qn_g=fox_qn_g, fox_kn_g=fox_kn_g, fox_fb=fox_fb,
        sg_ln_g=sg_ln_g, sg_ln_b=sg_ln_b, sg_ws=sg_ws, sg_b=sg_b, w_out=w_out)

    for l in range(L):
        names = ("sh1", "sc1", "g1", "sh2", "sc2", "g2")
        mods = {n: mod[l, :B, i * D:(i + 1) * D].reshape(B, 1, D) for i, n in enumerate(names)}
        p = {n: t[l] for n, t in layer_params.items()}
        dense = l % 2 == 0
        x2, h2 = _mixing_layer(x2, mods, p, w_in, l, B, S, BF16 if dense else F32)
        j = l // 2
        if dense:
            x2 = _ffn(h2, ffn_w1, ffn_w3, ffn_w2, j, x2, mods["g2"], S)
        else:
            rw = jnp.zeros((D, LANE), BF16).at[:, :N_EXPERTS].set(moe_router_w[j].astype(BF16))
            rb = jnp.zeros((1, LANE), F32).at[0, :N_EXPERTS].set(moe_router_b[j])
            x2 = _moe(h2, rw, rb, moe_w1[j], moe_w3[j], moe_w2[j], x2, mods["g2"], S)
    return x2.reshape(B, S, D)
```

```python
import functools

import jax
import jax.numpy as jnp
from jax import lax
from jax.experimental import pallas as pl
from jax.experimental.pallas import tpu as pltpu

F32 = jnp.float32
BF16 = jnp.bfloat16

D_MODEL = 2048
DEPTH = 2
RW_HD = 64
RW_W = 768
RW_H = 12
DECAY_LORA = 64
AAA_LORA = 64
GATE_LORA = 128
FOX_HD = 128
FOX_W = 768
FOX_H = 6
SG_GD = 128
SG_W = 512
SG_G = 4
SG_CHUNK = 128
N_EXPERTS = 8
N_MOD = 6
RMS_EPS = 1e-6
LN_EPS = 1e-5
LNX_EPS = 64e-5

C_R = 0
C_RWKV_END = 3 * RW_W + DECAY_LORA + AAA_LORA + GATE_LORA
C_Q = C_RWKV_END
C_FF = C_Q + 3 * FOX_W
C_U = C_FF + FOX_H
C_SV = C_U + SG_W
N_IN = C_SV + SG_W

P_R, P_K, P_V = 0, 768, 1536
P_LORA = 2304
P_Q, P_FK, P_FV = 2560, 3328, 4096
N_MAIN = C_FF
P_U = N_MAIN
P_SV = P_U + SG_W
P_FF = P_SV + SG_W
N_PACK = 6144
N_TAIL = N_PACK - N_MAIN
COL_TILE = 256

LANE = 128
VMEM_LIMIT = 56 * 1024 * 1024


def _cparams(sem):
    return pltpu.CompilerParams(dimension_semantics=sem, vmem_limit_bytes=VMEM_LIMIT)


def _sigmoid(x):
    return 1.0 / (1.0 + jnp.exp(-x))


def _softplus(x):
    return jnp.maximum(x, 0.0) + jnp.log(1.0 + jnp.exp(-jnp.abs(x)))


def _gelu_tanh(x):
    return 0.5 * x * (1.0 + jnp.tanh(0.7978845608028654 * (x + 0.044715 * (x * x * x))))


def _silu(x):
    return x * _sigmoid(x)


def _ada_kernel(c_ref, w_ref, b_ref, o_ref):
    cs = _silu(c_ref[...]).astype(BF16)
    w = w_ref[0].astype(BF16)
    o_ref[0] = jnp.dot(cs, w, preferred_element_type=F32) + b_ref[0]


def _ada_mod(c8, ada_w, ada_b3):
    L, D, N = ada_w.shape
    tn = 1536
    return pl.pallas_call(
        _ada_kernel, name="ada_mod",
        out_shape=jax.ShapeDtypeStruct((L, 8, N), F32),
        grid=(L, N // tn),
        in_specs=[
            pl.BlockSpec((8, D), lambda l, j: (0, 0)),
            pl.BlockSpec((1, D, tn), lambda l, j: (l, 0, j)),
            pl.BlockSpec((1, 1, tn), lambda l, j: (l, 0, j)),
        ],
        out_specs=pl.BlockSpec((1, 8, tn), lambda l, j: (l, 0, j)),
        compiler_params=_cparams(("parallel", "parallel")),
    )(c8, ada_w, ada_b3)


def _col_spec(tm, width, start):
    assert start % width == 0
    return pl.BlockSpec((tm, width), lambda i, *_: (i, start // width))


def _inproj_kernel(x_ref, g_ref, sc_ref, sh_ref, wm_ref, wt_ref, o_ref, h_scr, *, n_main):
    j = pl.program_id(1)

    @pl.when(j == 0)
    def _():
        x = x_ref[...]
        ms = jnp.mean(x * x, axis=-1, keepdims=True)
        y = x * lax.rsqrt(ms + RMS_EPS) * g_ref[...]
        h_scr[...] = (y * (1.0 + sc_ref[0]) + sh_ref[0]).astype(BF16)

    @pl.when(j < n_main)
    def _():
        o_ref[...] = jnp.dot(h_scr[...], wm_ref[0].astype(BF16), preferred_element_type=F32)

    @pl.when(j >= n_main)
    def _():
        o_ref[...] = jnp.dot(h_scr[...], wt_ref[...], preferred_element_type=F32)


def _inproj(x2, g, sc, sh, w_in, layer, w_tail, seq):
    M, D = x2.shape
    tm, tn = min(2048, seq), COL_TILE
    n_main = N_MAIN // tn
    return pl.pallas_call(
        functools.partial(_inproj_kernel, n_main=n_main), name="inproj",
        out_shape=jax.ShapeDtypeStruct((M, N_PACK), F32),
        grid=(M // tm, N_PACK // tn),
        in_specs=[
            pl.BlockSpec((tm, D), lambda i, j: (i, 0), pipeline_mode=pl.Buffered(1)),
            pl.BlockSpec((1, D), lambda i, j: (0, 0)),
            pl.BlockSpec((1, 1, D), lambda i, j: (i * tm // seq, 0, 0)),
            pl.BlockSpec((1, 1, D), lambda i, j: (i * tm // seq, 0, 0)),
            pl.BlockSpec((1, D, tn), lambda i, j: (layer, 0, jnp.minimum(j, n_main - 1))),
            pl.BlockSpec((D, tn), lambda i, j: (0, jnp.maximum(j - n_main, 0))),
        ],
        out_specs=pl.BlockSpec((tm, tn), lambda i, j: (i, j)),
        scratch_shapes=[pltpu.VMEM((tm, D), BF16)],
        compiler_params=_cparams(("parallel", "arbitrary")),
    )(x2, g, sc, sh, w_in, w_tail)


def _rwkv_prep_kernel(zr_ref, zk_ref, zv_ref, zl_ref, pr_ref, pk_ref, pv_ref, pl_ref,
                      mu_ref, mul_ref, w0_ref, w2_ref, a0_ref, a2_ref, g2_ref, kk_ref, ka_ref,
                      r_out, lw_out, k_out, v_out, kk_out, a_out, g_out, *, tm, seq):
    first = (pl.program_id(0) * tm) % seq == 0

    def shift(z_ref, p_ref, mu):
        z = z_ref[...]
        prev_last = jnp.where(first, 0.0, p_ref[7:8, :])
        zp = pltpu.roll(z, 1, 0)
        row = lax.broadcasted_iota(jnp.int32, z.shape, 0)
        zp = jnp.where(row == 0, prev_last, zp)
        return z + (zp - z) * mu

    r = shift(zr_ref, pr_ref, mu_ref[0:1, :])
    k = shift(zk_ref, pk_ref, mu_ref[1:2, :])
    v = shift(zv_ref, pv_ref, mu_ref[2:3, :])
    lo = shift(zl_ref, pl_ref, mul_ref[...])
    wd = lo[:, 0:DECAY_LORA]
    ad = lo[:, DECAY_LORA:DECAY_LORA + AAA_LORA]
    gd = lo[:, DECAY_LORA + AAA_LORA:]

    dec = w0_ref[...] + jnp.dot(jnp.tanh(wd).astype(BF16), w2_ref[...], preferred_element_type=F32)
    w_log = -_softplus(-dec) - 0.5
    lw = -jnp.exp(w_log)
    a = _sigmoid(a0_ref[...] + jnp.dot(ad.astype(BF16), a2_ref[...], preferred_element_type=F32))
    g = jnp.dot(_sigmoid(gd).astype(BF16), g2_ref[...], preferred_element_type=F32)
    kk = k * kk_ref[...]
    km = k * (1.0 + (a - 1.0) * ka_ref[...])

    for h in range(RW_H):
        sl = slice(h * RW_HD, (h + 1) * RW_HD)
        r_out[h] = r[:, sl]
        lw_out[h] = lw[:, sl]
        k_out[h] = km[:, sl]
        v_out[h] = v[:, sl]
        kk_out[h] = kk[:, sl]
        a_out[h] = a[:, sl]
        g_out[h] = g[:, sl]


def _rwkv_prep(hz, mu3, mul, w0, w2, a0, a2, g2, k_k, k_a, seq):
    M = hz.shape[0]
    tm = 256
    prev = lambda i: jnp.maximum(i * (tm // 8) - 1, 0)
    row = lambda n: pl.BlockSpec((1, n), lambda i: (0, 0))
    in_specs = [
        pl.BlockSpec((tm, RW_W), lambda i: (i, P_R // RW_W)),
        pl.BlockSpec((tm, RW_W), lambda i: (i, P_K // RW_W)),
        pl.BlockSpec((tm, RW_W), lambda i: (i, P_V // RW_W)),
        pl.BlockSpec((tm, 256), lambda i: (i, P_LORA // 256)),
        pl.BlockSpec((8, RW_W), lambda i: (prev(i), P_R // RW_W)),
        pl.BlockSpec((8, RW_W), lambda i: (prev(i), P_K // RW_W)),
        pl.BlockSpec((8, RW_W), lambda i: (prev(i), P_V // RW_W)),
        pl.BlockSpec((8, 256), lambda i: (prev(i), P_LORA // 256)),
        pl.BlockSpec((3, RW_W), lambda i: (0, 0)),
        row(256), row(RW_W),
        pl.BlockSpec((DECAY_LORA, RW_W), lambda i: (0, 0)),
        row(RW_W),
        pl.BlockSpec((AAA_LORA, RW_W), lambda i: (0, 0)),
        pl.BlockSpec((GATE_LORA, RW_W), lambda i: (0, 0)),
        row(RW_W), row(RW_W),
    ]
    hm = jax.ShapeDtypeStruct((RW_H, M, RW_HD), F32)
    hm_spec = pl.BlockSpec((RW_H, tm, RW_HD), lambda i: (0, i, 0))
    return pl.pallas_call(
        functools.partial(_rwkv_prep_kernel, tm=tm, seq=seq), name="rwkv_prep",
        out_shape=[hm] * 7,
        grid=(M // tm,),
        in_specs=in_specs,
        out_specs=[hm_spec] * 7,
        compiler_params=_cparams(("parallel",)),
    )(hz, hz, hz, hz, hz, hz, hz, hz, mu3, mul, w0, w2, a0, a2, g2, k_k, k_a)


def _bmm(a, b, ca, cb):
    return lax.dot_general(a.astype(BF16), b.astype(BF16), (((ca,), (cb,)), ((0,), (0,))),
                           preferred_element_type=F32)


def _split3(x):
    hi = x.astype(BF16)
    r1 = x - hi.astype(F32)
    mid = r1.astype(BF16)
    lo = (r1 - mid.astype(F32)).astype(BF16)
    return hi, mid, lo


def _rwkv_scan_kernel(r_ref, lw_ref, k_ref, v_ref, kk_ref, a_ref, g_ref, rk_ref, lng_ref, lnb_ref,
                      o_ref, s_scr, *, L, nC):
    H, K = RW_H, RW_HD
    n = H * nC

    @pl.when(pl.program_id(1) == 0)
    def _():
        s_scr[...] = jnp.zeros_like(s_scr)

    ld = lambda ref: ref[...].reshape(n, L, K)
    r, lw, k, v, kk, a = ld(r_ref), ld(lw_ref), ld(k_ref), ld(v_ref), ld(kk_ref), ld(a_ref)

    kk = kk / jnp.maximum(jnp.sqrt(jnp.sum(kk * kk, axis=-1, keepdims=True)), 1e-12)
    b = kk * a

    ti = lax.broadcasted_iota(jnp.int32, (L, L), 0)
    si = lax.broadcasted_iota(jnp.int32, (L, L), 1)
    tri_incl = jnp.broadcast_to((ti >= si).astype(BF16), (n, L, L))
    cw = sum(_bmm(tri_incl, piece, 2, 1) for piece in reversed(_split3(lw)))
    cw_last = cw[:, L - 1:L, :]
    e_in = jnp.exp(cw)
    e_out = jnp.exp(-cw)
    e_ex = jnp.exp(cw - lw)
    e_end = jnp.exp(cw_last - cw)
    w_end = jnp.exp(cw_last)

    at = -kk * e_ex
    rt = r * e_in
    bb = b * e_out
    kb = k * e_out
    bh = b * e_end
    kh = k * e_end

    atrt = jnp.concatenate([at, rt], axis=1)
    t2 = lax.broadcasted_iota(jnp.int32, (2 * L, L), 0)
    s2 = lax.broadcasted_iota(jnp.int32, (2 * L, L), 1)
    causal = (((t2 & (L - 1)) - s2 + jnp.where(t2 < L, 0, 1)) > 0)[None]
    pb = jnp.where(causal, _bmm(atrt, bb, 2, 2), 0.0)
    pk = jnp.where(causal, _bmm(atrt, kb, 2, 2), 0.0)
    a_ab, b_rb = pb[:, :L], pb[:, L:]
    a_ak, b_rk = pk[:, :L], pk[:, L:]

    x = jnp.broadcast_to((ti == si).astype(F32), (n, L, L))
    size = 1
    while size < L:
        sh = size.bit_length() - 1
        m = (((ti >> (sh + 1)) == (si >> (sh + 1))) & (((ti >> sh) & 1) == 1) & (((si >> sh) & 1) == 0))[None]
        x = x + _bmm(_bmm(x, jnp.where(m, a_ab, 0.0), 2, 1), x, 2, 1)
        size *= 2

    xu = _bmm(x, jnp.concatenate([at, _bmm(a_ak, v, 2, 1)], axis=-1), 2, 1)
    w = _bmm(b_rb, xu, 2, 1)
    rp = rt + w[..., :K]
    y0 = w[..., K:] + _bmm(b_rk, v, 2, 1)
    big = _bmm(xu, bh, 1, 1)
    gp = big[:, :K]
    cc = big[:, K:] + _bmm(v, kh, 1, 1)

    c4 = lambda t: t.reshape((H, nC) + t.shape[1:])
    rp, y0, gp, cc, w_end = c4(rp), c4(y0), c4(gp), c4(cc), c4(w_end)
    s = s_scr[...]
    ys = []
    for c in range(nC):
        ys.append(_bmm(rp[:, c], s, 2, 2) + y0[:, c])
        s = s * w_end[:, c] + _bmm(s, gp[:, c], 2, 1) + cc[:, c]
    s_scr[...] = s
    y = ys[0] if nC == 1 else jnp.concatenate(ys, axis=1)

    mu = jnp.mean(y, axis=-1, keepdims=True)
    yc = y - mu
    var = jnp.mean(yc * yc, axis=-1, keepdims=True)
    yn = yc * lax.rsqrt(var + LNX_EPS) * lng_ref[...] + lnb_ref[...]
    r3, k3, v3 = r_ref[...], k_ref[...], v_ref[...]
    bonus = jnp.sum(r3 * k3 * rk_ref[...], axis=-1, keepdims=True) * v3
    out = (yn + bonus) * g_ref[...]
    for p in range(H // 2):
        pair = jnp.concatenate([out[2 * p], out[2 * p + 1]], axis=-1)
        o_ref[:, p * LANE:(p + 1) * LANE] = pair.astype(o_ref.dtype)


def _rwkv_scan(r, lw, k, v, kk, a, g, r_k, lnx_g, lnx_b, batch, seq):
    M = r.shape[1]
    L, nC = 64, 2
    tb = L * nC
    nj = seq // tb
    hm_spec = pl.BlockSpec((RW_H, tb, RW_HD), lambda b, j: (0, b * nj + j, 0))
    par = pl.BlockSpec((RW_H, 1, RW_HD), lambda b, j: (0, 0, 0))
    return pl.pallas_call(
        functools.partial(_rwkv_scan_kernel, L=L, nC=nC), name="rwkv_scan",
        out_shape=jax.ShapeDtypeStruct((M, RW_W), BF16),
        grid=(batch, nj),
        in_specs=[hm_spec] * 7 + [par] * 3,
        out_specs=pl.BlockSpec((tb, RW_W), lambda b, j: (b * nj + j, 0)),
        scratch_shapes=[pltpu.VMEM((RW_H, RW_HD, RW_HD), F32)],
        compiler_params=_cparams(("parallel", "arbitrary")),
    )(r, lw, k, v, kk, a, g, r_k, lnx_g, lnx_b)


FOX_XD = 2 * FOX_HD


def _fox_prep_kernel(*refs, tm, seq):
    q_refs, k_refs, v_refs = refs[0:3], refs[3:6], refs[6:9]
    f_ref, qg_ref, kg_ref, fb_ref, kx_ref, qxT_ref, vT_ref, carry = refs[9:]

    @pl.when((pl.program_id(0) * tm) % seq == 0)
    def _():
        carry[...] = jnp.zeros_like(carry)

    logf = -_softplus(-(f_ref[...] + fb_ref[...]))
    ti = lax.broadcasted_iota(jnp.int32, (tm, tm), 0)
    si = lax.broadcasted_iota(jnp.int32, (tm, tm), 1)
    tri = (ti >= si).astype(BF16)
    cum = carry[...]
    for piece in reversed(_split3(logf)):
        cum = cum + jnp.dot(tri, piece, preferred_element_type=F32)
    carry[...] = cum[tm - 1:tm, :]
    cum_t = cum.T

    scale = FOX_HD ** -0.5
    lane = lax.broadcasted_iota(jnp.int32, (tm, FOX_HD), 1)
    sub = lax.broadcasted_iota(jnp.int32, (FOX_HD, tm), 0)
    pieces = lambda x: [t.astype(F32) for t in _split3(x)]
    for h in range(FOX_H):
        blk, half = h // 2, slice((h % 2) * FOX_HD, (h % 2 + 1) * FOX_HD)
        q = q_refs[blk][:, half]
        k = k_refs[blk][:, half]
        qn = q * lax.rsqrt(jnp.mean(q * q, axis=-1, keepdims=True) + RMS_EPS) * qg_ref[...]
        kn = k * lax.rsqrt(jnp.mean(k * k, axis=-1, keepdims=True) + RMS_EPS) * kg_ref[...]

        c_hi, c_mid, c_lo = pieces(cum[:, h:h + 1])
        k_extra = jnp.where(lane < 3, 1.0, 0.0)
        k_extra = jnp.where(lane == 3, -c_hi, k_extra)
        k_extra = jnp.where(lane == 4, -c_mid, k_extra)
        k_extra = jnp.where(lane == 5, -c_lo, k_extra)
        kx_ref[:, h * FOX_XD:h * FOX_XD + FOX_HD] = kn.astype(BF16)
        kx_ref[:, h * FOX_XD + FOX_HD:(h + 1) * FOX_XD] = k_extra.astype(BF16)

        r_hi, r_mid, r_lo = pieces(cum_t[h:h + 1, :])
        q_extra = jnp.where(sub < 6, 1.0, 0.0)
        q_extra = jnp.where(sub == 0, r_hi, q_extra)
        q_extra = jnp.where(sub == 1, r_mid, q_extra)
        q_extra = jnp.where(sub == 2, r_lo, q_extra)
        qxT_ref[h * FOX_XD:h * FOX_XD + FOX_HD, :] = (qn * scale).T.astype(BF16)
        qxT_ref[h * FOX_XD + FOX_HD:(h + 1) * FOX_XD, :] = q_extra.astype(BF16)

        vT_ref[h * FOX_HD:(h + 1) * FOX_HD, :] = v_refs[blk][:, half].T.astype(BF16)


def _fox_prep(hz, qn_g, kn_g, fb128, seq):
    M = hz.shape[0]
    tm = 256
    row = pl.BlockSpec((1, LANE), lambda i: (0, 0))
    pieces = lambda start: [_col_spec(tm, COL_TILE, start + c * COL_TILE) for c in range(FOX_W // COL_TILE)]
    return pl.pallas_call(
        functools.partial(_fox_prep_kernel, tm=tm, seq=seq), name="fox_prep",
        out_shape=[jax.ShapeDtypeStruct((M, FOX_H * FOX_XD), BF16),
                   jax.ShapeDtypeStruct((FOX_H * FOX_XD, M), BF16),
                   jax.ShapeDtypeStruct((FOX_W, M), BF16)],
        grid=(M // tm,),
        in_specs=pieces(P_Q) + pieces(P_FK) + pieces(P_FV) + [_col_spec(tm, LANE, P_FF), row, row, row],
        out_specs=[pl.BlockSpec((tm, FOX_H * FOX_XD), lambda i: (i, 0)),
                   pl.BlockSpec((FOX_H * FOX_XD, tm), lambda i: (0, i)),
                   pl.BlockSpec((FOX_W, tm), lambda i: (0, i))],
        scratch_shapes=[pltpu.VMEM((1, LANE), F32)],
        compiler_params=_cparams(("arbitrary",)),
    )(*([hz] * 10), qn_g, kn_g, fb128)


def _fox_attn_kernel(kx_ref, qxT_ref, vT_ref, o_ref, m_scr, l_scr, acc_scr, *, t):
    qi = pl.program_id(1)
    ki = pl.program_id(2)

    @pl.when(ki == 0)
    def _():
        m_scr[...] = jnp.full_like(m_scr, -1e30)
        l_scr[...] = jnp.zeros_like(l_scr)
        acc_scr[...] = jnp.zeros_like(acc_scr)

    def step(diagonal):
        if diagonal:
            keep = (lax.broadcasted_iota(jnp.int32, (t, t), 1)
                    >= lax.broadcasted_iota(jnp.int32, (t, t), 0))
        for h in range(FOX_H):
            xs = slice(h * FOX_XD, (h + 1) * FOX_XD)
            hs = slice(h * FOX_HD, (h + 1) * FOX_HD)
            s = jnp.dot(kx_ref[:, xs], qxT_ref[xs, :], preferred_element_type=F32)
            if diagonal:
                s = jnp.where(keep, s, -1e30)
            m_prev = m_scr[h]
            m_new = jnp.maximum(m_prev, jnp.max(s, axis=0, keepdims=True))
            alpha = jnp.exp(m_prev - m_new)
            p = jnp.exp(s - m_new)
            l_new = alpha * l_scr[h] + jnp.sum(p, axis=0, keepdims=True)
            acc = alpha * acc_scr[hs, :] + jnp.dot(vT_ref[hs, :], p.astype(BF16),
                                                   preferred_element_type=F32)
            if diagonal:
                o_ref[:, hs] = (acc / l_new).T.astype(o_ref.dtype)
            else:
                m_scr[h] = m_new
                l_scr[h] = l_new
                acc_scr[hs, :] = acc

    pl.when(ki < qi)(lambda: step(False))
    pl.when(ki == qi)(lambda: step(True))


def _fox_attn(kx, qxT, vT, batch, seq):
    M = kx.shape[0]
    t = 512
    nt = seq // t
    kidx = lambda qi, ki: jnp.minimum(ki, qi)
    return pl.pallas_call(
        functools.partial(_fox_attn_kernel, t=t), name="fox_attn",
        out_shape=jax.ShapeDtypeStruct((M, FOX_W), BF16),
        grid=(batch, nt, nt),
        in_specs=[
            pl.BlockSpec((t, FOX_H * FOX_XD), lambda b, qi, ki: (b * nt + kidx(qi, ki), 0)),
            pl.BlockSpec((FOX_H * FOX_XD, t), lambda b, qi, ki: (0, b * nt + qi)),
            pl.BlockSpec((FOX_W, t), lambda b, qi, ki: (0, b * nt + kidx(qi, ki))),
        ],
        out_specs=pl.BlockSpec((t, FOX_W), lambda b, qi, ki: (b * nt + qi, 0)),
        scratch_shapes=[pltpu.VMEM((FOX_H, 1, t), F32), pltpu.VMEM((FOX_H, 1, t), F32),
                        pltpu.VMEM((FOX_W, t), F32)],
        compiler_params=_cparams(("parallel", "parallel", "arbitrary")),
    )(kx, qxT, vT)


def _sgu_kernel(u0_ref, u1_ref, v0_ref, v1_ref, lg_ref, lb_ref, ws_ref, sb_ref, o_ref, *, tm):
    ti = lax.broadcasted_iota(jnp.int32, (SG_CHUNK, SG_CHUNK), 0)
    si = lax.broadcasted_iota(jnp.int32, (SG_CHUNK, SG_CHUNK), 1)
    u_refs, v_refs = (u0_ref, u1_ref), (v0_ref, v1_ref)
    for g in range(SG_G):
        sl = slice(g * SG_GD, (g + 1) * SG_GD)
        half = slice((g % 2) * SG_GD, (g % 2 + 1) * SG_GD)
        u = _gelu_tanh(u_refs[g // 2][:, half])
        v = _gelu_tanh(v_refs[g // 2][:, half])
        mu = jnp.mean(v, axis=-1, keepdims=True)
        vc = v - mu
        var = jnp.mean(vc * vc, axis=-1, keepdims=True)
        vn = (vc * lax.rsqrt(var + LN_EPS) * lg_ref[g:g + 1, :] + lb_ref[g:g + 1, :]).astype(BF16)
        ws = jnp.where(ti >= si, ws_ref[g], 0.0).astype(BF16)
        bias = sb_ref[:, g:g + 1]
        for c in range(tm // SG_CHUNK):
            rows = slice(c * SG_CHUNK, (c + 1) * SG_CHUNK)
            mixed = jnp.dot(ws, vn[rows], preferred_element_type=F32) + bias
            o_ref[rows, sl] = (u[rows] * mixed).astype(o_ref.dtype)


def _sgu(hz, ln_g, ln_b, ws, sb_t):
    M = hz.shape[0]
    tm = 512
    return pl.pallas_call(
        functools.partial(_sgu_kernel, tm=tm), name="sgu",
        out_shape=jax.ShapeDtypeStruct((M, SG_W), BF16),
        grid=(M // tm,),
        in_specs=[
            _col_spec(tm, COL_TILE, P_U), _col_spec(tm, COL_TILE, P_U + COL_TILE),
            _col_spec(tm, COL_TILE, P_SV), _col_spec(tm, COL_TILE, P_SV + COL_TILE),
            pl.BlockSpec((SG_G, SG_GD), lambda i: (0, 0)),
            pl.BlockSpec((SG_G, SG_GD), lambda i: (0, 0)),
            pl.BlockSpec((SG_G, SG_CHUNK, SG_CHUNK), lambda i: (0, 0, 0)),
            pl.BlockSpec((SG_CHUNK, SG_G), lambda i: (0, 0)),
        ],
        out_specs=pl.BlockSpec((tm, SG_W), lambda i: (i, 0)),
        compiler_params=_cparams(("parallel",)),
    )(hz, hz, hz, hz, ln_g, ln_b, ws, sb_t)


def _outproj_kernel(oa_ref, ob_ref, oc_ref, w_ref, x_ref, g1_ref, ng_ref, sc_ref, sh_ref,
                    xo_ref, h_ref):
    mix = jnp.dot(oa_ref[...], w_ref[0:RW_W, :], preferred_element_type=F32)
    mix += jnp.dot(ob_ref[...], w_ref[RW_W:RW_W + FOX_W, :], preferred_element_type=F32)
    mix += jnp.dot(oc_ref[...], w_ref[RW_W + FOX_W:, :], preferred_element_type=F32)
    x = x_ref[...] + g1_ref[0] * mix
    xo_ref[...] = x
    ms = jnp.mean(x * x, axis=-1, keepdims=True)
    y = x * lax.rsqrt(ms + RMS_EPS) * ng_ref[...]
    h_ref[...] = (y * (1.0 + sc_ref[0]) + sh_ref[0]).astype(h_ref.dtype)


def _outproj(oa, ob, oc, w, x2, g1, ng, sc, sh, seq, h_dtype):
    M, D = x2.shape
    tm = 512
    bidx = lambda i: (i * tm // seq, 0, 0)
    mod = pl.BlockSpec((1, 1, D), bidx)
    return pl.pallas_call(
        _outproj_kernel, name="outproj",
        out_shape=[jax.ShapeDtypeStruct((M, D), F32), jax.ShapeDtypeStruct((M, D), h_dtype)],
        grid=(M // tm,),
        in_specs=[
            pl.BlockSpec((tm, RW_W), lambda i: (i, 0)),
            pl.BlockSpec((tm, FOX_W), lambda i: (i, 0)),
            pl.BlockSpec((tm, SG_W), lambda i: (i, 0)),
            pl.BlockSpec((D, D), lambda i: (0, 0)),
            pl.BlockSpec((tm, D), lambda i: (i, 0)),
            mod,
            pl.BlockSpec((1, D), lambda i: (0, 0)),
            mod, mod,
        ],
        out_specs=[pl.BlockSpec((tm, D), lambda i: (i, 0)), pl.BlockSpec((tm, D), lambda i: (i, 0))],
        compiler_params=_cparams(("parallel",)),
    )(oa, ob, oc, w, x2, g1, ng, sc, sh)


def _swiglu_accumulate(h_ref, w1_ref, w3_ref, w2_ref, acc_ref, rows):
    h = h_ref[0:rows, :]
    a = jnp.dot(h, w1_ref[0].astype(BF16), preferred_element_type=F32)
    b = jnp.dot(h, w3_ref[0].astype(BF16), preferred_element_type=F32)
    acc_ref[0:rows, :] += jnp.dot((_silu(a) * b).astype(BF16), w2_ref[0].astype(BF16),
                                  preferred_element_type=F32)


def _ffn_kernel(h_ref, w1_ref, w3_ref, w2_ref, x_ref, g2_ref, o_ref):
    j = pl.program_id(1)

    @pl.when(j == 0)
    def _():
        o_ref[...] = jnp.zeros_like(o_ref)

    _swiglu_accumulate(h_ref, w1_ref, w3_ref, w2_ref, o_ref, h_ref.shape[0])

    @pl.when(j == pl.num_programs(1) - 1)
    def _():
        o_ref[...] = x_ref[...] + g2_ref[0] * o_ref[...]


def _ffn(h2, w1, w3, w2, layer, x2, g2, seq):
    M, D = x2.shape
    F = w1.shape[2]
    tm, tf = 1024, 512
    once = dict(pipeline_mode=pl.Buffered(1))
    mod = pl.BlockSpec((1, 1, D), lambda i, j: (i * tm // seq, 0, 0))
    return pl.pallas_call(
        _ffn_kernel, name="ffn",
        out_shape=jax.ShapeDtypeStruct((M, D), F32),
        grid=(M // tm, F // tf),
        in_specs=[
            pl.BlockSpec((tm, D), lambda i, j: (i, 0), **once),
            pl.BlockSpec((1, D, tf), lambda i, j: (layer, 0, j)),
            pl.BlockSpec((1, D, tf), lambda i, j: (layer, 0, j)),
            pl.BlockSpec((1, tf, D), lambda i, j: (layer, j, 0)),
            pl.BlockSpec((tm, D), lambda i, j: (i, 0), **once),
            mod,
        ],
        out_specs=pl.BlockSpec((tm, D), lambda i, j: (i, 0), **once),
        compiler_params=_cparams(("parallel", "arbitrary")),
    )(h2, w1, w3, w2, x2, g2)


def _router_kernel(h_ref, w_ref, b_ref, info_ref, cnt_ref, carry, *, tm):
    @pl.when(pl.program_id(0) == 0)
    def _():
        carry[...] = jnp.zeros_like(carry)

    logits = jnp.dot(h_ref[...].astype(BF16), w_ref[...], preferred_element_type=F32) + b_ref[...]
    lane = lax.broadcasted_iota(jnp.int32, logits.shape, 1)
    neg = -1e30
    logits = jnp.where(lane < N_EXPERTS, logits, neg)
    m1 = jnp.max(logits, axis=-1, keepdims=True)
    i1 = jnp.min(jnp.where(logits == m1, lane, LANE), axis=-1, keepdims=True)
    rest = jnp.where(lane == i1, neg, logits)
    m2 = jnp.max(rest, axis=-1, keepdims=True)
    i2 = jnp.min(jnp.where(rest == m2, lane, LANE), axis=-1, keepdims=True)
    e2 = jnp.exp(m2 - m1)
    p1 = 1.0 / (1.0 + e2)
    p2 = e2 / (1.0 + e2)

    oh1 = (lane == i1).astype(F32)
    oh2 = (lane == i2).astype(F32)
    both = oh1 + oh2
    ti = lax.broadcasted_iota(jnp.int32, (tm, tm), 0)
    si = lax.broadcasted_iota(jnp.int32, (tm, tm), 1)
    strict = (ti > si).astype(BF16)
    before = jnp.dot(strict, both.astype(BF16), preferred_element_type=F32) + carry[...]
    rank1 = jnp.sum(oh1 * before, axis=-1, keepdims=True)
    rank2 = jnp.sum(oh2 * before, axis=-1, keepdims=True)
    total = carry[...] + jnp.sum(both, axis=0, keepdims=True)
    carry[...] = total
    cnt_ref[...] = total

    info = jnp.where(lane == 0, i1.astype(F32), 0.0)
    info = jnp.where(lane == 1, i2.astype(F32), info)
    info = jnp.where(lane == 2, rank1, info)
    info = jnp.where(lane == 3, rank2, info)
    info = jnp.where(lane == 4, p1, info)
    info = jnp.where(lane == 5, p2, info)
    info_ref[...] = info


def _router(h2, rw, rb):
    M, D = h2.shape
    tm = 512
    return pl.pallas_call(
        functools.partial(_router_kernel, tm=tm), name="router",
        out_shape=[jax.ShapeDtypeStruct((M, LANE), F32), jax.ShapeDtypeStruct((1, LANE), F32)],
        grid=(M // tm,),
        in_specs=[
            pl.BlockSpec((tm, D), lambda i: (i, 0)),
            pl.BlockSpec((D, LANE), lambda i: (0, 0)),
            pl.BlockSpec((1, LANE), lambda i: (0, 0)),
        ],
        out_specs=[pl.BlockSpec((tm, LANE), lambda i: (i, 0)), pl.BlockSpec((1, LANE), lambda i: (0, 0))],
        scratch_shapes=[pltpu.VMEM((1, LANE), F32)],
        compiler_params=_cparams(("arbitrary",)),
    )(h2, rw, rb)


MOE_TILE = 1024


def _moe_dispatch_kernel(s1_ref, s2_ref, h_ref, xs_in_ref, xs_ref, sem, *, tm):
    del xs_in_ref
    base = pl.program_id(0) * tm

    def copies(r):
        src = h_ref.at[pl.ds(r, 1)]
        return (pltpu.make_async_copy(src, xs_ref.at[pl.ds(s1_ref[base + r], 1)], sem.at[0]),
                pltpu.make_async_copy(src, xs_ref.at[pl.ds(s2_ref[base + r], 1)], sem.at[1]))

    def start(r, carry):
        for cp in copies(r):
            cp.start()
        return carry

    def wait(r, carry):
        for cp in copies(r):
            cp.wait()
        return carry

    lax.fori_loop(0, tm, start, 0, unroll=8)
    lax.fori_loop(0, tm, wait, 0, unroll=8)


def _moe_dispatch(slot1, slot2, h2, n_rows):
    M, D = h2.shape
    tm = 256
    xs0 = jnp.zeros((n_rows, D), F32)
    return pl.pallas_call(
        functools.partial(_moe_dispatch_kernel, tm=tm), name="moe_dispatch",
        out_shape=jax.ShapeDtypeStruct((n_rows, D), F32),
        grid_spec=pltpu.PrefetchScalarGridSpec(
            num_scalar_prefetch=2,
            grid=(M // tm,),
            in_specs=[pl.BlockSpec((tm, D), lambda i, s1, s2: (i, 0)),
                      pl.BlockSpec(memory_space=pl.ANY)],
            out_specs=pl.BlockSpec(memory_space=pl.ANY),
            scratch_shapes=[pltpu.SemaphoreType.DMA((2,))],
        ),
        input_output_aliases={3: 0},
        compiler_params=_cparams(("arbitrary",)),
    )(slot1, slot2, h2, xs0)


def _moe_expert_kernel(te_ref, nv_ref, xs_ref, w1_ref, w3_ref, w2_ref, ys_ref, xb):
    j = pl.program_id(1)
    nv = nv_ref[pl.program_id(0)]
    half = xb.shape[0] // 2

    @pl.when(j == 0)
    def _():
        ys_ref[...] = jnp.zeros_like(ys_ref)
        xb[...] = xs_ref[...].astype(BF16)

    @pl.when(nv > half)
    def _():
        _swiglu_accumulate(xb, w1_ref, w3_ref, w2_ref, ys_ref, 2 * half)

    @pl.when((nv > 0) & (nv <= half))
    def _():
        _swiglu_accumulate(xb, w1_ref, w3_ref, w2_ref, ys_ref, half)


def _moe_experts(tile_expert, n_valid, xs, w1, w3, w2):
    P, D = xs.shape
    E, _, F = w1.shape
    tm, tf = MOE_TILE, 256
    nf = F // tf
    fj = lambda i, j, nv: jnp.where(nv[i] > 0, j, nf - 1)
    return pl.pallas_call(
        _moe_expert_kernel, name="moe_experts",
        out_shape=jax.ShapeDtypeStruct((P, D), F32),
        grid_spec=pltpu.PrefetchScalarGridSpec(
            num_scalar_prefetch=2,
            grid=(P // tm, nf),
            in_specs=[
                pl.BlockSpec((tm, D), lambda i, j, te, nv: (i, 0)),
                pl.BlockSpec((1, D, tf), lambda i, j, te, nv: (te[i], 0, fj(i, j, nv))),
                pl.BlockSpec((1, D, tf), lambda i, j, te, nv: (te[i], 0, fj(i, j, nv))),
                pl.BlockSpec((1, tf, D), lambda i, j, te, nv: (te[i], fj(i, j, nv), 0)),
            ],
            out_specs=pl.BlockSpec((tm, D), lambda i, j, te, nv: (i, 0)),
            scratch_shapes=[pltpu.VMEM((tm, D), BF16)],
        ),
        compiler_params=_cparams(("arbitrary", "arbitrary")),
    )(tile_expert, n_valid, xs, w1, w3, w2)


def _moe_combine_kernel(s1_ref, s2_ref, ys_ref, info_ref, x_ref, g2_ref, o_ref, a_buf, b_buf, sem, *, tm):
    base = pl.program_id(0) * tm

    def copies(r):
        return (pltpu.make_async_copy(ys_ref.at[pl.ds(s1_ref[base + r], 1)], a_buf.at[pl.ds(r, 1)], sem.at[0]),
                pltpu.make_async_copy(ys_ref.at[pl.ds(s2_ref[base + r], 1)], b_buf.at[pl.ds(r, 1)], sem.at[1]))

    def start(r, carry):
        for cp in copies(r):
            cp.start()
        return carry

    def wait(r, carry):
        for cp in copies(r):
            cp.wait()
        return carry

    lax.fori_loop(0, tm, start, 0, unroll=8)
    lax.fori_loop(0, tm, wait, 0, unroll=8)
    info = info_ref[...]
    p1 = info[:, 4:5]
    p2 = info[:, 5:6]
    o_ref[...] = x_ref[...] + g2_ref[0] * (p1 * a_buf[...] + p2 * b_buf[...])


def _moe_combine(slot1, slot2, ys, info, x2, g2, seq):
    M, D = x2.shape
    tm = 256
    return pl.pallas_call(
        functools.partial(_moe_combine_kernel, tm=tm), name="moe_combine",
        out_shape=jax.ShapeDtypeStruct((M, D), F32),
        grid_spec=pltpu.PrefetchScalarGridSpec(
            num_scalar_prefetch=2,
            grid=(M // tm,),
            in_specs=[
                pl.BlockSpec(memory_space=pl.ANY),
                pl.BlockSpec((tm, LANE), lambda i, s1, s2: (i, 0)),
                pl.BlockSpec((tm, D), lambda i, s1, s2: (i, 0)),
                pl.BlockSpec((1, 1, D), lambda i, s1, s2: (i * tm // seq, 0, 0)),
            ],
            out_specs=pl.BlockSpec((tm, D), lambda i, s1, s2: (i, 0)),
            scratch_shapes=[pltpu.VMEM((tm, D), F32), pltpu.VMEM((tm, D), F32),
                            pltpu.SemaphoreType.DMA((2,))],
        ),
        compiler_params=_cparams(("arbitrary",)),
    )(slot1, slot2, ys, info, x2, g2)


def _moe(h2, rw, rb, w1, w3, w2, x2, g2, seq):
    M, D = x2.shape
    E = w1.shape[0]
    T = MOE_TILE
    n_rows = 2 * M + E * T
    n_tiles = n_rows // T
    info, counts = _router(h2, rw, rb)

    e1, e2, rank1, rank2 = (info[:, c].astype(jnp.int32) for c in range(4))
    cnt = counts[0, :E].astype(jnp.int32)
    padded = (cnt + T - 1) // T * T
    ends = jnp.cumsum(padded)
    off = ends - padded
    expert_ids = jnp.arange(E, dtype=jnp.int32)
    offset_of = lambda e: jnp.sum(jnp.where(e[:, None] == expert_ids[None, :], off[None, :], 0), axis=1)
    slot1 = offset_of(e1) + rank1
    slot2 = offset_of(e2) + rank2
    tile_start = jnp.arange(n_tiles, dtype=jnp.int32) * T
    in_use = tile_start < ends[E - 1]
    clamped = jnp.minimum(tile_start, ends[E - 1] - T)
    tile_expert = jnp.sum(clamped[:, None] >= ends[None, :], axis=1).astype(jnp.int32)
    is_e = tile_expert[:, None] == expert_ids[None, :]
    tokens_end = jnp.sum(jnp.where(is_e, (off + cnt)[None, :], 0), axis=1)
    n_valid = jnp.where(in_use, jnp.clip(tokens_end - tile_start, 0, T), 0).astype(jnp.int32)

    xs = _moe_dispatch(slot1, slot2, h2, n_rows)
    ys = _moe_experts(tile_expert, n_valid, xs, w1, w3, w2)
    return _moe_combine(slot1, slot2, ys, info, x2, g2, seq)


TAIL_BLOCK = N_MAIN // 2


def _w_tail_kernel(w_ref, o_ref):
    n_sg = 2 * SG_W
    t = w_ref[0][:, :n_sg + LANE]
    shifted = pltpu.roll(t, n_sg + LANE - FOX_H, 1)
    o_ref[:, :n_sg] = shifted[:, :n_sg].astype(BF16)
    lane = lax.broadcasted_iota(jnp.int32, (t.shape[0], LANE), 1)
    o_ref[:, n_sg:n_sg + LANE] = jnp.where(lane < FOX_H, t[:, :LANE], 0.0).astype(BF16)
    o_ref[:, n_sg + LANE:] = jnp.zeros((t.shape[0], N_TAIL - n_sg - LANE), BF16)


def _w_in_tail(w_in, layer):
    L, D, _ = w_in.shape
    assert 2 * TAIL_BLOCK == N_MAIN and N_IN - N_MAIN <= 2 * SG_W + LANE <= TAIL_BLOCK
    tr = 256
    return pl.pallas_call(
        _w_tail_kernel, name="w_tail",
        out_shape=jax.ShapeDtypeStruct((D, N_TAIL), BF16),
        grid=(D // tr,),
        in_specs=[pl.BlockSpec((1, tr, TAIL_BLOCK), lambda i: (layer, i, 2))],
        out_specs=pl.BlockSpec((tr, N_TAIL), lambda i: (i, 0)),
        compiler_params=_cparams(("parallel",)),
    )(w_in)


def _mixing_layer(x2, mods, p, w_in, layer, batch, seq, h_dtype):
    row = lambda t: t.reshape(1, -1)
    hz = _inproj(x2, row(p["norm1_g"]), mods["sc1"], mods["sh1"], w_in, layer,
                 _w_in_tail(w_in, layer), seq)

    mu = p["shift_mu"]
    mu3 = mu[:3 * RW_W].reshape(3, RW_W)
    mul = row(mu[3 * RW_W:])
    r, lw, k, v, kk, a, g = _rwkv_prep(
        hz, mu3, mul, row(p["rw_w0"]), p["rw_w2"].astype(BF16), row(p["rw_a0"]),
        p["rw_a2"].astype(BF16), p["rw_g2"].astype(BF16), row(p["rw_k_k"]), row(p["rw_k_a"]), seq)
    hm = lambda t: t.reshape(RW_H, 1, RW_HD)
    o_a = _rwkv_scan(r, lw, k, v, kk, a, g, hm(p["rw_r_k"]), hm(p["rw_lnx_g"]), hm(p["rw_lnx_b"]),
                     batch, seq)

    fb128 = jnp.zeros((1, LANE), F32).at[0, :FOX_H].set(p["fox_fb"])
    kx, qxT, vT = _fox_prep(hz, row(p["fox_qn_g"]), row(p["fox_kn_g"]), fb128, seq)
    o_b = _fox_attn(kx, qxT, vT, batch, seq)

    o_c = _sgu(hz, p["sg_ln_g"], p["sg_ln_b"], p["sg_ws"], jnp.transpose(p["sg_b"]))

    return _outproj(o_a, o_b, o_c, p["w_out"].astype(BF16), x2, mods["g1"], row(p["norm2_g"]),
                    mods["sc2"], mods["sh2"], seq, h_dtype)


def kernel(x, c, ada_w, ada_b, norm1_g, norm2_g, w_in, shift_mu, rw_w0, rw_w2, rw_a0, rw_a2, rw_g2, rw_k_k, rw_k_a, rw_r_k, rw_lnx_g, rw_lnx_b, fox_qn_g, fox_kn_g, fox_fb, sg_ln_g, sg_ln_b, sg_ws, sg_b, w_out, ffn_w1, ffn_w3, ffn_w2, moe_router_w, moe_router_b, moe_w1, moe_w3, moe_w2):
    B, S, D = x.shape
    L = ada_w.shape[0]
    x2 = x.reshape(B * S, D)
    c8 = jnp.zeros((8, D), F32).at[:B].set(c)
    mod = _ada_mod(c8, ada_w, ada_b.reshape(L, 1, N_MOD * D))

    layer_params = dict(
        norm1_g=norm1_g, norm2_g=norm2_g, shift_mu=shift_mu, rw_w0=rw_w0, rw_w2=rw_w2,
        rw_a0=rw_a0, rw_a2=rw_a2, rw_g2=rw_g2, rw_k_k=rw_k_k, rw_k_a=rw_k_a, rw_r_k=rw_r_k,
        rw_lnx_g=rw_lnx_g, rw_lnx_b=rw_lnx_b, fox_qn_g=fox_qn_g, fox_kn_g=fox_kn_g, fox_fb=fox_fb,
        sg_ln_g=sg_ln_g, sg_ln_b=sg_ln_b, sg_ws=sg_ws, sg_b=sg_b, w_out=w_out)

    for l in range(L):
        names = ("sh1", "sc1", "g1", "sh2", "sc2", "g2")
        mods = {n: mod[l, :B, i * D:(i + 1) * D].reshape(B, 1, D) for i, n in enumerate(names)}
        p = {n: t[l] for n, t in layer_params.items()}
        dense = l % 2 == 0
        x2, h2 = _mixing_layer(x2, mods, p, w_in, l, B, S, BF16 if dense else F32)
        j = l // 2
        if dense:
            x2 = _ffn(h2, ffn_w1, ffn_w3, ffn_w2, j, x2, mods["g2"], S)
        else:
            rw = jnp.zeros((D, LANE), BF16).at[:, :N_EXPERTS].set(moe_router_w[j].astype(BF16))
            rb = jnp.zeros((1, LANE), F32).at[0, :N_EXPERTS].set(moe_router_b[j])
            x2 = _moe(h2, rw, rb, moe_w1[j], moe_w3[j], moe_w2[j], x2, mods["g2"], S)
    return x2.reshape(B, S, D)
```

```python
import functools

import jax
import jax.numpy as jnp
from jax import lax
from jax.experimental import pallas as pl
from jax.experimental.pallas import tpu as pltpu

F32 = jnp.float32
BF16 = jnp.bfloat16

D_MODEL = 2048
DEPTH = 2
RW_HD = 64
RW_W = 768
RW_H = 12
DECAY_LORA = 64
AAA_LORA = 64
GATE_LORA = 128
FOX_HD = 128
FOX_W = 768
FOX_H = 6
SG_GD = 128
SG_W = 512
SG_G = 4
SG_CHUNK = 128
N_EXPERTS = 8
N_MOD = 6
RMS_EPS = 1e-6
LN_EPS = 1e-5
LNX_EPS = 64e-5

C_R = 0
C_RWKV_END = 3 * RW_W + DECAY_LORA + AAA_LORA + GATE_LORA
C_Q = C_RWKV_END
C_FF = C_Q + 3 * FOX_W
C_U = C_FF + FOX_H
C_SV = C_U + SG_W
N_IN = C_SV + SG_W

P_R, P_K, P_V = 0, 768, 1536
P_LORA = 2304
P_Q, P_FK, P_FV = 2560, 3328, 4096
P_FF = C_FF
P_U = 5120
P_SV = P_U + SG_W
N_PACK = P_SV + SG_W
COL_TILE = 256

LANE = 128
VMEM_LIMIT = 56 * 1024 * 1024


def _cparams(sem):
    return pltpu.CompilerParams(dimension_semantics=sem, vmem_limit_bytes=VMEM_LIMIT)


def _sigmoid(x):
    return 1.0 / (1.0 + jnp.exp(-x))


def _softplus(x):
    return jnp.maximum(x, 0.0) + jnp.log(1.0 + jnp.exp(-jnp.abs(x)))


def _gelu_tanh(x):
    return 0.5 * x * (1.0 + jnp.tanh(0.7978845608028654 * (x + 0.044715 * (x * x * x))))


def _silu(x):
    return x * _sigmoid(x)


def _ada_kernel(c_ref, w_ref, b_ref, o_ref):
    cs = _silu(c_ref[...]).astype(BF16)
    w = w_ref[0].astype(BF16)
    o_ref[0] = jnp.dot(cs, w, preferred_element_type=F32) + b_ref[0]


def _ada_mod(c8, ada_w, ada_b3):
    L, D, N = ada_w.shape
    tn = 1536
    return pl.pallas_call(
        _ada_kernel, name="ada_mod",
        out_shape=jax.ShapeDtypeStruct((L, 8, N), F32),
        grid=(L, N // tn),
        in_specs=[
            pl.BlockSpec((8, D), lambda l, j: (0, 0)),
            pl.BlockSpec((1, D, tn), lambda l, j: (l, 0, j)),
            pl.BlockSpec((1, 1, tn), lambda l, j: (l, 0, j)),
        ],
        out_specs=pl.BlockSpec((1, 8, tn), lambda l, j: (l, 0, j)),
        compiler_params=_cparams(("parallel", "parallel")),
    )(c8, ada_w, ada_b3)


def _col_spec(tm, width, start):
    assert start % width == 0
    return pl.BlockSpec((tm, width), lambda i, *_: (i, start // width))


def _inproj_kernel(x_ref, g_ref, sc_ref, sh_ref, w_hbm, o_ref, h_scr, wbuf, sem, *, layer, tn):
    i, j = pl.program_id(0), pl.program_id(1)
    ni, nj = pl.num_programs(0), pl.num_programs(1)
    step = i * nj + j
    slot = step % 2

    def weight_copy(jj, s):
        first_row = jnp.where(jj < nj - 2, jj * tn, N_IN - (nj - jj) * tn)
        return pltpu.make_async_copy(w_hbm.at[pl.ds(first_row, tn), layer], wbuf.at[s], sem.at[s])

    @pl.when(step == 0)
    def _():
        weight_copy(j, slot).start()

    @pl.when(step + 1 < ni * nj)
    def _():
        weight_copy(jnp.where(j + 1 < nj, j + 1, 0), 1 - slot).start()

    @pl.when(j == 0)
    def _():
        chunk = 512
        for c in range(x_ref.shape[0] // chunk):
            rows = pl.ds(c * chunk, chunk)
            x = x_ref[rows, :]
            ms = jnp.mean(x * x, axis=-1, keepdims=True)
            y = x * lax.rsqrt(ms + RMS_EPS) * g_ref[...]
            h_scr[rows, :] = (y * (1.0 + sc_ref[0]) + sh_ref[0]).astype(BF16)

    weight_copy(j, slot).wait()
    o_ref[...] = lax.dot_general(h_scr[...], wbuf[slot].astype(BF16), (((1,), (1,)), ((), ())),
                                 preferred_element_type=F32)


def _inproj(x2, g, sc, sh, w_t, layer, seq):
    M, D = x2.shape
    tm, tn = min(2048, seq), SG_W
    nj = N_PACK // tn
    assert (nj - 2) * tn == P_U and P_FF + FOX_H <= P_U
    return pl.pallas_call(
        functools.partial(_inproj_kernel, layer=layer, tn=tn), name="inproj",
        out_shape=jax.ShapeDtypeStruct((M, N_PACK), F32),
        grid=(M // tm, nj),
        in_specs=[
            pl.BlockSpec((tm, D), lambda i, j: (i, 0), pipeline_mode=pl.Buffered(1)),
            pl.BlockSpec((1, D), lambda i, j: (0, 0)),
            pl.BlockSpec((1, 1, D), lambda i, j: (i * tm // seq, 0, 0)),
            pl.BlockSpec((1, 1, D), lambda i, j: (i * tm // seq, 0, 0)),
            pl.BlockSpec(memory_space=pl.ANY),
        ],
        out_specs=pl.BlockSpec((tm, tn), lambda i, j: (i, j)),
        scratch_shapes=[pltpu.VMEM((tm, D), BF16), pltpu.VMEM((2, tn, D), F32),
                        pltpu.SemaphoreType.DMA((2,))],
        compiler_params=_cparams(("arbitrary", "arbitrary")),
    )(x2, g, sc, sh, w_t)


def _rwkv_prep_kernel(zr_ref, zk_ref, zv_ref, zl_ref, pr_ref, pk_ref, pv_ref, pl_ref,
                      mu_ref, mul_ref, w0_ref, w2_ref, a0_ref, a2_ref, g2_ref, kk_ref, ka_ref,
                      r_out, lw_out, k_out, v_out, kk_out, a_out, g_out, *, tm, seq):
    first = (pl.program_id(0) * tm) % seq == 0

    def shift(z_ref, p_ref, mu):
        z = z_ref[...]
        prev_last = jnp.where(first, 0.0, p_ref[7:8, :])
        zp = pltpu.roll(z, 1, 0)
        row = lax.broadcasted_iota(jnp.int32, z.shape, 0)
        zp = jnp.where(row == 0, prev_last, zp)
        return z + (zp - z) * mu

    r = shift(zr_ref, pr_ref, mu_ref[0:1, :])
    k = shift(zk_ref, pk_ref, mu_ref[1:2, :])
    v = shift(zv_ref, pv_ref, mu_ref[2:3, :])
    lo = shift(zl_ref, pl_ref, mul_ref[...])
    wd = lo[:, 0:DECAY_LORA]
    ad = lo[:, DECAY_LORA:DECAY_LORA + AAA_LORA]
    gd = lo[:, DECAY_LORA + AAA_LORA:]

    dec = w0_ref[...] + jnp.dot(jnp.tanh(wd).astype(BF16), w2_ref[...], preferred_element_type=F32)
    w_log = -_softplus(-dec) - 0.5
    lw = -jnp.exp(w_log)
    a = _sigmoid(a0_ref[...] + jnp.dot(ad.astype(BF16), a2_ref[...], preferred_element_type=F32))
    g = jnp.dot(_sigmoid(gd).astype(BF16), g2_ref[...], preferred_element_type=F32)
    kk = k * kk_ref[...]
    km = k * (1.0 + (a - 1.0) * ka_ref[...])

    for h in range(RW_H):
        sl = slice(h * RW_HD, (h + 1) * RW_HD)
        r_out[h] = r[:, sl]
        lw_out[h] = lw[:, sl]
        k_out[h] = km[:, sl]
        v_out[h] = v[:, sl]
        kk_out[h] = kk[:, sl]
        a_out[h] = a[:, sl]
        g_out[h] = g[:, sl]


def _rwkv_prep(hz, mu3, mul, w0, w2, a0, a2, g2, k_k, k_a, seq):
    M = hz.shape[0]
    tm = 256
    prev = lambda i: jnp.maximum(i * (tm // 8) - 1, 0)
    row = lambda n: pl.BlockSpec((1, n), lambda i: (0, 0))
    in_specs = [
        pl.BlockSpec((tm, RW_W), lambda i: (i, P_R // RW_W)),
        pl.BlockSpec((tm, RW_W), lambda i: (i, P_K // RW_W)),
        pl.BlockSpec((tm, RW_W), lambda i: (i, P_V // RW_W)),
        pl.BlockSpec((tm, 256), lambda i: (i, P_LORA // 256)),
        pl.BlockSpec((8, RW_W), lambda i: (prev(i), P_R // RW_W)),
        pl.BlockSpec((8, RW_W), lambda i: (prev(i), P_K // RW_W)),
        pl.BlockSpec((8, RW_W), lambda i: (prev(i), P_V // RW_W)),
        pl.BlockSpec((8, 256), lambda i: (prev(i), P_LORA // 256)),
        pl.BlockSpec((3, RW_W), lambda i: (0, 0)),
        row(256), row(RW_W),
        pl.BlockSpec((DECAY_LORA, RW_W), lambda i: (0, 0)),
        row(RW_W),
        pl.BlockSpec((AAA_LORA, RW_W), lambda i: (0, 0)),
        pl.BlockSpec((GATE_LORA, RW_W), lambda i: (0, 0)),
        row(RW_W), row(RW_W),
    ]
    hm = jax.ShapeDtypeStruct((RW_H, M, RW_HD), F32)
    hm_spec = pl.BlockSpec((RW_H, tm, RW_HD), lambda i: (0, i, 0))
    return pl.pallas_call(
        functools.partial(_rwkv_prep_kernel, tm=tm, seq=seq), name="rwkv_prep",
        out_shape=[hm] * 7,
        grid=(M // tm,),
        in_specs=in_specs,
        out_specs=[hm_spec] * 7,
        compiler_params=_cparams(("parallel",)),
    )(hz, hz, hz, hz, hz, hz, hz, hz, mu3, mul, w0, w2, a0, a2, g2, k_k, k_a)


def _bmm(a, b, ca, cb):
    return lax.dot_general(a.astype(BF16), b.astype(BF16), (((ca,), (cb,)), ((0,), (0,))),
                           preferred_element_type=F32)


def _split3(x):
    hi = x.astype(BF16)
    r1 = x - hi.astype(F32)
    mid = r1.astype(BF16)
    lo = (r1 - mid.astype(F32)).astype(BF16)
    return hi, mid, lo


def _rwkv_scan_kernel(r_ref, lw_ref, k_ref, v_ref, kk_ref, a_ref, g_ref, rk_ref, lng_ref, lnb_ref,
                      o_ref, s_scr, *, L, nC):
    H, K = RW_H, RW_HD
    n = H * nC

    @pl.when(pl.program_id(1) == 0)
    def _():
        s_scr[...] = jnp.zeros_like(s_scr)

    ld = lambda ref: ref[...].reshape(n, L, K)
    r, lw, k, v, kk, a = ld(r_ref), ld(lw_ref), ld(k_ref), ld(v_ref), ld(kk_ref), ld(a_ref)

    kk = kk / jnp.maximum(jnp.sqrt(jnp.sum(kk * kk, axis=-1, keepdims=True)), 1e-12)
    b = kk * a

    ti = lax.broadcasted_iota(jnp.int32, (L, L), 0)
    si = lax.broadcasted_iota(jnp.int32, (L, L), 1)
    tri_incl = jnp.broadcast_to((ti >= si).astype(BF16), (n, L, L))
    cw = sum(_bmm(tri_incl, piece, 2, 1) for piece in reversed(_split3(lw)))
    cw_last = cw[:, L - 1:L, :]
    e_in = jnp.exp(cw)
    e_out = jnp.exp(-cw)
    e_ex = jnp.exp(cw - lw)
    e_end = jnp.exp(cw_last - cw)
    w_end = jnp.exp(cw_last)

    at = -kk * e_ex
    rt = r * e_in
    bb = b * e_out
    kb = k * e_out
    bh = b * e_end
    kh = k * e_end

    atrt = jnp.concatenate([at, rt], axis=1)
    t2 = lax.broadcasted_iota(jnp.int32, (2 * L, L), 0)
    s2 = lax.broadcasted_iota(jnp.int32, (2 * L, L), 1)
    causal = (((t2 & (L - 1)) - s2 + jnp.where(t2 < L, 0, 1)) > 0)[None]
    pb = jnp.where(causal, _bmm(atrt, bb, 2, 2), 0.0)
    pk = jnp.where(causal, _bmm(atrt, kb, 2, 2), 0.0)
    a_ab, b_rb = pb[:, :L], pb[:, L:]
    a_ak, b_rk = pk[:, :L], pk[:, L:]

    x = jnp.broadcast_to((ti == si).astype(F32), (n, L, L))
    size = 1
    while size < L:
        sh = size.bit_length() - 1
        m = (((ti >> (sh + 1)) == (si >> (sh + 1))) & (((ti >> sh) & 1) == 1) & (((si >> sh) & 1) == 0))[None]
        x = x + _bmm(_bmm(x, jnp.where(m, a_ab, 0.0), 2, 1), x, 2, 1)
        size *= 2

    xu = _bmm(x, jnp.concatenate([at, _bmm(a_ak, v, 2, 1)], axis=-1), 2, 1)
    w = _bmm(b_rb, xu, 2, 1)
    rp = rt + w[..., :K]
    y0 = w[..., K:] + _bmm(b_rk, v, 2, 1)
    big = _bmm(xu, bh, 1, 1)
    gp = big[:, :K]
    cc = big[:, K:] + _bmm(v, kh, 1, 1)

    c4 = lambda t: t.reshape((H, nC) + t.shape[1:])
    rp, y0, gp, cc, w_end = c4(rp), c4(y0), c4(gp), c4(cc), c4(w_end)
    s = s_scr[...]
    ys = []
    for c in range(nC):
        ys.append(_bmm(rp[:, c], s, 2, 2) + y0[:, c])
        s = s * w_end[:, c] + _bmm(s, gp[:, c], 2, 1) + cc[:, c]
    s_scr[...] = s
    y = ys[0] if nC == 1 else jnp.concatenate(ys, axis=1)

    mu = jnp.mean(y, axis=-1, keepdims=True)
    yc = y - mu
    var = jnp.mean(yc * yc, axis=-1, keepdims=True)
    yn = yc * lax.rsqrt(var + LNX_EPS) * lng_ref[...] + lnb_ref[...]
    r3, k3, v3 = r_ref[...], k_ref[...], v_ref[...]
    bonus = jnp.sum(r3 * k3 * rk_ref[...], axis=-1, keepdims=True) * v3
    out = (yn + bonus) * g_ref[...]
    for p in range(H // 2):
        pair = jnp.concatenate([out[2 * p], out[2 * p + 1]], axis=-1)
        o_ref[:, p * LANE:(p + 1) * LANE] = pair.astype(o_ref.dtype)


def _rwkv_scan(r, lw, k, v, kk, a, g, r_k, lnx_g, lnx_b, batch, seq):
    M = r.shape[1]
    L, nC = 64, 2
    tb = L * nC
    nj = seq // tb
    hm_spec = pl.BlockSpec((RW_H, tb, RW_HD), lambda b, j: (0, b * nj + j, 0))
    par = pl.BlockSpec((RW_H, 1, RW_HD), lambda b, j: (0, 0, 0))
    return pl.pallas_call(
        functools.partial(_rwkv_scan_kernel, L=L, nC=nC), name="rwkv_scan",
        out_shape=jax.ShapeDtypeStruct((M, RW_W), BF16),
        grid=(batch, nj),
        in_specs=[hm_spec] * 7 + [par] * 3,
        out_specs=pl.BlockSpec((tb, RW_W), lambda b, j: (b * nj + j, 0)),
        scratch_shapes=[pltpu.VMEM((RW_H, RW_HD, RW_HD), F32)],
        compiler_params=_cparams(("parallel", "arbitrary")),
    )(r, lw, k, v, kk, a, g, r_k, lnx_g, lnx_b)


FOX_XD = 2 * FOX_HD


def _fox_prep_kernel(*refs, tm, seq):
    q_refs, k_refs, v_refs = refs[0:3], refs[3:6], refs[6:9]
    f_ref, qg_ref, kg_ref, fb_ref, kx_ref, qxT_ref, vT_ref, carry = refs[9:]

    @pl.when((pl.program_id(0) * tm) % seq == 0)
    def _():
        carry[...] = jnp.zeros_like(carry)

    logf = -_softplus(-(f_ref[...] + fb_ref[...]))
    ti = lax.broadcasted_iota(jnp.int32, (tm, tm), 0)
    si = lax.broadcasted_iota(jnp.int32, (tm, tm), 1)
    tri = (ti >= si).astype(BF16)
    cum = carry[...]
    for piece in reversed(_split3(logf)):
        cum = cum + jnp.dot(tri, piece, preferred_element_type=F32)
    carry[...] = cum[tm - 1:tm, :]
    cum_t = cum.T

    scale = FOX_HD ** -0.5
    lane = lax.broadcasted_iota(jnp.int32, (tm, FOX_HD), 1)
    sub = lax.broadcasted_iota(jnp.int32, (FOX_HD, tm), 0)
    pieces = lambda x: [t.astype(F32) for t in _split3(x)]
    for h in range(FOX_H):
        blk, half = h // 2, slice((h % 2) * FOX_HD, (h % 2 + 1) * FOX_HD)
        q = q_refs[blk][:, half]
        k = k_refs[blk][:, half]
        qn = q * lax.rsqrt(jnp.mean(q * q, axis=-1, keepdims=True) + RMS_EPS) * qg_ref[...]
        kn = k * lax.rsqrt(jnp.mean(k * k, axis=-1, keepdims=True) + RMS_EPS) * kg_ref[...]

        c_hi, c_mid, c_lo = pieces(cum[:, h:h + 1])
        k_extra = jnp.where(lane < 3, 1.0, 0.0)
        k_extra = jnp.where(lane == 3, -c_hi, k_extra)
        k_extra = jnp.where(lane == 4, -c_mid, k_extra)
        k_extra = jnp.where(lane == 5, -c_lo, k_extra)
        kx_ref[:, h * FOX_XD:h * FOX_XD + FOX_HD] = kn.astype(BF16)
        kx_ref[:, h * FOX_XD + FOX_HD:(h + 1) * FOX_XD] = k_extra.astype(BF16)

        r_hi, r_mid, r_lo = pieces(cum_t[h:h + 1, :])
        q_extra = jnp.where(sub < 6, 1.0, 0.0)
        q_extra = jnp.where(sub == 0, r_hi, q_extra)
        q_extra = jnp.where(sub == 1, r_mid, q_extra)
        q_extra = jnp.where(sub == 2, r_lo, q_extra)
        qxT_ref[h * FOX_XD:h * FOX_XD + FOX_HD, :] = (qn * scale).T.astype(BF16)
        qxT_ref[h * FOX_XD + FOX_HD:(h + 1) * FOX_XD, :] = q_extra.astype(BF16)

        vT_ref[h * FOX_HD:(h + 1) * FOX_HD, :] = v_refs[blk][:, half].T.astype(BF16)


def _fox_prep(hz, qn_g, kn_g, fb128, seq):
    M = hz.shape[0]
    tm = 256
    row = pl.BlockSpec((1, LANE), lambda i: (0, 0))
    pieces = lambda start: [_col_spec(tm, COL_TILE, start + c * COL_TILE) for c in range(FOX_W // COL_TILE)]
    return pl.pallas_call(
        functools.partial(_fox_prep_kernel, tm=tm, seq=seq), name="fox_prep",
        out_shape=[jax.ShapeDtypeStruct((M, FOX_H * FOX_XD), BF16),
                   jax.ShapeDtypeStruct((FOX_H * FOX_XD, M), BF16),
                   jax.ShapeDtypeStruct((FOX_W, M), BF16)],
        grid=(M // tm,),
        in_specs=pieces(P_Q) + pieces(P_FK) + pieces(P_FV) + [_col_spec(tm, LANE, P_FF), row, row, row],
        out_specs=[pl.BlockSpec((tm, FOX_H * FOX_XD), lambda i: (i, 0)),
                   pl.BlockSpec((FOX_H * FOX_XD, tm), lambda i: (0, i)),
                   pl.BlockSpec((FOX_W, tm), lambda i: (0, i))],
        scratch_shapes=[pltpu.VMEM((1, LANE), F32)],
        compiler_params=_cparams(("arbitrary",)),
    )(*([hz] * 10), qn_g, kn_g, fb128)


def _fox_attn_kernel(kx_ref, qxT_ref, vT_ref, o_ref, m_scr, l_scr, acc_scr, *, t):
    qi = pl.program_id(1)
    ki = pl.program_id(2)

    @pl.when(ki == 0)
    def _():
        m_scr[...] = jnp.full_like(m_scr, -1e30)
        l_scr[...] = jnp.zeros_like(l_scr)
        acc_scr[...] = jnp.zeros_like(acc_scr)

    def step(diagonal):
        if diagonal:
            keep = (lax.broadcasted_iota(jnp.int32, (t, t), 1)
                    >= lax.broadcasted_iota(jnp.int32, (t, t), 0))
        for h in range(FOX_H):
            xs = slice(h * FOX_XD, (h + 1) * FOX_XD)
            hs = slice(h * FOX_HD, (h + 1) * FOX_HD)
            s = jnp.dot(kx_ref[:, xs], qxT_ref[xs, :], preferred_element_type=F32)
            if diagonal:
                s = jnp.where(keep, s, -1e30)
            m_prev = m_scr[h]
            m_new = jnp.maximum(m_prev, jnp.max(s, axis=0, keepdims=True))
            alpha = jnp.exp(m_prev - m_new)
            p = jnp.exp(s - m_new)
            l_new = alpha * l_scr[h] + jnp.sum(p, axis=0, keepdims=True)
            acc = alpha * acc_scr[hs, :] + jnp.dot(vT_ref[hs, :], p.astype(BF16),
                                                   preferred_element_type=F32)
            if diagonal:
                o_ref[:, hs] = (acc / l_new).T.astype(o_ref.dtype)
            else:
                m_scr[h] = m_new
                l_scr[h] = l_new
                acc_scr[hs, :] = acc

    pl.when(ki < qi)(lambda: step(False))
    pl.when(ki == qi)(lambda: step(True))


def _fox_attn(kx, qxT, vT, batch, seq):
    M = kx.shape[0]
    t = 512
    nt = seq // t
    kidx = lambda qi, ki: jnp.minimum(ki, qi)
    return pl.pallas_call(
        functools.partial(_fox_attn_kernel, t=t), name="fox_attn",
        out_shape=jax.ShapeDtypeStruct((M, FOX_W), BF16),
        grid=(batch, nt, nt),
        in_specs=[
            pl.BlockSpec((t, FOX_H * FOX_XD), lambda b, qi, ki: (b * nt + kidx(qi, ki), 0)),
            pl.BlockSpec((FOX_H * FOX_XD, t), lambda b, qi, ki: (0, b * nt + qi)),
            pl.BlockSpec((FOX_W, t), lambda b, qi, ki: (0, b * nt + kidx(qi, ki))),
        ],
        out_specs=pl.BlockSpec((t, FOX_W), lambda b, qi, ki: (b * nt + qi, 0)),
        scratch_shapes=[pltpu.VMEM((FOX_H, 1, t), F32), pltpu.VMEM((FOX_H, 1, t), F32),
                        pltpu.VMEM((FOX_W, t), F32)],
        compiler_params=_cparams(("parallel", "parallel", "arbitrary")),
    )(kx, qxT, vT)


def _sgu_kernel(u_ref, v_ref, lg_ref, lb_ref, ws_ref, sb_ref, o_ref, *, tm):
    ti = lax.broadcasted_iota(jnp.int32, (SG_CHUNK, SG_CHUNK), 0)
    si = lax.broadcasted_iota(jnp.int32, (SG_CHUNK, SG_CHUNK), 1)
    for g in range(SG_G):
        sl = slice(g * SG_GD, (g + 1) * SG_GD)
        u = _gelu_tanh(u_ref[:, sl])
        v = _gelu_tanh(v_ref[:, sl])
        mu = jnp.mean(v, axis=-1, keepdims=True)
        vc = v - mu
        var = jnp.mean(vc * vc, axis=-1, keepdims=True)
        vn = (vc * lax.rsqrt(var + LN_EPS) * lg_ref[g:g + 1, :] + lb_ref[g:g + 1, :]).astype(BF16)
        ws = jnp.where(ti >= si, ws_ref[g], 0.0).astype(BF16)
        bias = sb_ref[:, g:g + 1]
        for c in range(tm // SG_CHUNK):
            rows = slice(c * SG_CHUNK, (c + 1) * SG_CHUNK)
            mixed = jnp.dot(ws, vn[rows], preferred_element_type=F32) + bias
            o_ref[rows, sl] = (u[rows] * mixed).astype(o_ref.dtype)


def _sgu(hz, ln_g, ln_b, ws, sb_t):
    M = hz.shape[0]
    tm = 512
    return pl.pallas_call(
        functools.partial(_sgu_kernel, tm=tm), name="sgu",
        out_shape=jax.ShapeDtypeStruct((M, SG_W), BF16),
        grid=(M // tm,),
        in_specs=[
            _col_spec(tm, SG_W, P_U), _col_spec(tm, SG_W, P_SV),
            pl.BlockSpec((SG_G, SG_GD), lambda i: (0, 0)),
            pl.BlockSpec((SG_G, SG_GD), lambda i: (0, 0)),
            pl.BlockSpec((SG_G, SG_CHUNK, SG_CHUNK), lambda i: (0, 0, 0)),
            pl.BlockSpec((SG_CHUNK, SG_G), lambda i: (0, 0)),
        ],
        out_specs=pl.BlockSpec((tm, SG_W), lambda i: (i, 0)),
        compiler_params=_cparams(("parallel",)),
    )(hz, hz, ln_g, ln_b, ws, sb_t)


def _outproj_kernel(oa_ref, ob_ref, oc_ref, w_ref, x_ref, g1_ref, ng_ref, sc_ref, sh_ref,
                    xo_ref, h_ref):
    mix = jnp.dot(oa_ref[...], w_ref[0:RW_W, :], preferred_element_type=F32)
    mix += jnp.dot(ob_ref[...], w_ref[RW_W:RW_W + FOX_W, :], preferred_element_type=F32)
    mix += jnp.dot(oc_ref[...], w_ref[RW_W + FOX_W:, :], preferred_element_type=F32)
    x = x_ref[...] + g1_ref[0] * mix
    xo_ref[...] = x
    ms = jnp.mean(x * x, axis=-1, keepdims=True)
    y = x * lax.rsqrt(ms + RMS_EPS) * ng_ref[...]
    h_ref[...] = (y * (1.0 + sc_ref[0]) + sh_ref[0]).astype(h_ref.dtype)


def _outproj(oa, ob, oc, w, x2, g1, ng, sc, sh, seq, h_dtype):
    M, D = x2.shape
    tm = 512
    bidx = lambda i: (i * tm // seq, 0, 0)
    mod = pl.BlockSpec((1, 1, D), bidx)
    return pl.pallas_call(
        _outproj_kernel, name="outproj",
        out_shape=[jax.ShapeDtypeStruct((M, D), F32), jax.ShapeDtypeStruct((M, D), h_dtype)],
        grid=(M // tm,),
        in_specs=[
            pl.BlockSpec((tm, RW_W), lambda i: (i, 0)),
            pl.BlockSpec((tm, FOX_W), lambda i: (i, 0)),
            pl.BlockSpec((tm, SG_W), lambda i: (i, 0)),
            pl.BlockSpec((D, D), lambda i: (0, 0)),
            pl.BlockSpec((tm, D), lambda i: (i, 0)),
            mod,
            pl.BlockSpec((1, D), lambda i: (0, 0)),
            mod, mod,
        ],
        out_specs=[pl.BlockSpec((tm, D), lambda i: (i, 0)), pl.BlockSpec((tm, D), lambda i: (i, 0))],
        compiler_params=_cparams(("parallel",)),
    )(oa, ob, oc, w, x2, g1, ng, sc, sh)


def _swiglu_accumulate(h_ref, w1_ref, w3_ref, w2_ref, acc_ref, rows):
    h = h_ref[0:rows, :]
    a = jnp.dot(h, w1_ref[0].astype(BF16), preferred_element_type=F32)
    b = jnp.dot(h, w3_ref[0].astype(BF16), preferred_element_type=F32)
    acc_ref[0:rows, :] += jnp.dot((_silu(a) * b).astype(BF16), w2_ref[0].astype(BF16),
                                  preferred_element_type=F32)


def _ffn_kernel(h_ref, w1_ref, w3_ref, w2_ref, x_ref, g2_ref, o_ref):
    j = pl.program_id(1)

    @pl.when(j == 0)
    def _():
        o_ref[...] = jnp.zeros_like(o_ref)

    _swiglu_accumulate(h_ref, w1_ref, w3_ref, w2_ref, o_ref, h_ref.shape[0])

    @pl.when(j == pl.num_programs(1) - 1)
    def _():
        o_ref[...] = x_ref[...] + g2_ref[0] * o_ref[...]


def _ffn(h2, w1, w3, w2, layer, x2, g2, seq):
    M, D = x2.shape
    F = w1.shape[2]
    tm, tf = 1024, 512
    once = dict(pipeline_mode=pl.Buffered(1))
    mod = pl.BlockSpec((1, 1, D), lambda i, j: (i * tm // seq, 0, 0))
    return pl.pallas_call(
        _ffn_kernel, name="ffn",
        out_shape=jax.ShapeDtypeStruct((M, D), F32),
        grid=(M // tm, F // tf),
        in_specs=[
            pl.BlockSpec((tm, D), lambda i, j: (i, 0), **once),
            pl.BlockSpec((1, D, tf), lambda i, j: (layer, 0, j)),
            pl.BlockSpec((1, D, tf), lambda i, j: (layer, 0, j)),
            pl.BlockSpec((1, tf, D), lambda i, j: (layer, j, 0)),
            pl.BlockSpec((tm, D), lambda i, j: (i, 0), **once),
            mod,
        ],
        out_specs=pl.BlockSpec((tm, D), lambda i, j: (i, 0), **once),
        compiler_params=_cparams(("parallel", "arbitrary")),
    )(h2, w1, w3, w2, x2, g2)


def _router_kernel(h_ref, w_ref, b_ref, info_ref, cnt_ref, carry, *, tm):
    @pl.when(pl.program_id(0) == 0)
    def _():
        carry[...] = jnp.zeros_like(carry)

    logits = jnp.dot(h_ref[...].astype(BF16), w_ref[...], preferred_element_type=F32) + b_ref[...]
    lane = lax.broadcasted_iota(jnp.int32, logits.shape, 1)
    neg = -1e30
    logits = jnp.where(lane < N_EXPERTS, logits, neg)
    m1 = jnp.max(logits, axis=-1, keepdims=True)
    i1 = jnp.min(jnp.where(logits == m1, lane, LANE), axis=-1, keepdims=True)
    rest = jnp.where(lane == i1, neg, logits)
    m2 = jnp.max(rest, axis=-1, keepdims=True)
    i2 = jnp.min(jnp.where(rest == m2, lane, LANE), axis=-1, keepdims=True)
    e2 = jnp.exp(m2 - m1)
    p1 = 1.0 / (1.0 + e2)
    p2 = e2 / (1.0 + e2)

    oh1 = (lane == i1).astype(F32)
    oh2 = (lane == i2).astype(F32)
    both = oh1 + oh2
    ti = lax.broadcasted_iota(jnp.int32, (tm, tm), 0)
    si = lax.broadcasted_iota(jnp.int32, (tm, tm), 1)
    strict = (ti > si).astype(BF16)
    before = jnp.dot(strict, both.astype(BF16), preferred_element_type=F32) + carry[...]
    rank1 = jnp.sum(oh1 * before, axis=-1, keepdims=True)
    rank2 = jnp.sum(oh2 * before, axis=-1, keepdims=True)
    total = carry[...] + jnp.sum(both, axis=0, keepdims=True)
    carry[...] = total
    cnt_ref[...] = total

    info = jnp.where(lane == 0, i1.astype(F32), 0.0)
    info = jnp.where(lane == 1, i2.astype(F32), info)
    info = jnp.where(lane == 2, rank1, info)
    info = jnp.where(lane == 3, rank2, info)
    info = jnp.where(lane == 4, p1, info)
    info = jnp.where(lane == 5, p2, info)
    info_ref[...] = info


def _router(h2, rw, rb):
    M, D = h2.shape
    tm = 512
    return pl.pallas_call(
        functools.partial(_router_kernel, tm=tm), name="router",
        out_shape=[jax.ShapeDtypeStruct((M, LANE), F32), jax.ShapeDtypeStruct((1, LANE), F32)],
        grid=(M // tm,),
        in_specs=[
            pl.BlockSpec((tm, D), lambda i: (i, 0)),
            pl.BlockSpec((D, LANE), lambda i: (0, 0)),
            pl.BlockSpec((1, LANE), lambda i: (0, 0)),
        ],
        out_specs=[pl.BlockSpec((tm, LANE), lambda i: (i, 0)), pl.BlockSpec((1, LANE), lambda i: (0, 0))],
        scratch_shapes=[pltpu.VMEM((1, LANE), F32)],
        compiler_params=_cparams(("arbitrary",)),
    )(h2, rw, rb)


MOE_TILE = 1024


def _moe_dispatch_kernel(s1_ref, s2_ref, h_ref, xs_in_ref, xs_ref, sem, *, tm):
    del xs_in_ref
    base = pl.program_id(0) * tm

    def copies(r):
        src = h_ref.at[pl.ds(r, 1)]
        return (pltpu.make_async_copy(src, xs_ref.at[pl.ds(s1_ref[base + r], 1)], sem.at[0]),
                pltpu.make_async_copy(src, xs_ref.at[pl.ds(s2_ref[base + r], 1)], sem.at[1]))

    def start(r, carry):
        for cp in copies(r):
            cp.start()
        return carry

    def wait(r, carry):
        for cp in copies(r):
            cp.wait()
        return carry

    lax.fori_loop(0, tm, start, 0, unroll=8)
    lax.fori_loop(0, tm, wait, 0, unroll=8)


def _moe_dispatch(slot1, slot2, h2, n_rows):
    M, D = h2.shape
    tm = 256
    xs0 = jnp.zeros((n_rows, D), F32)
    return pl.pallas_call(
        functools.partial(_moe_dispatch_kernel, tm=tm), name="moe_dispatch",
        out_shape=jax.ShapeDtypeStruct((n_rows, D), F32),
        grid_spec=pltpu.PrefetchScalarGridSpec(
            num_scalar_prefetch=2,
            grid=(M // tm,),
            in_specs=[pl.BlockSpec((tm, D), lambda i, s1, s2: (i, 0)),
                      pl.BlockSpec(memory_space=pl.ANY)],
            out_specs=pl.BlockSpec(memory_space=pl.ANY),
            scratch_shapes=[pltpu.SemaphoreType.DMA((2,))],
        ),
        input_output_aliases={3: 0},
        compiler_params=_cparams(("arbitrary",)),
    )(slot1, slot2, h2, xs0)


def _moe_expert_kernel(te_ref, nv_ref, xs_ref, w1_ref, w3_ref, w2_ref, ys_ref, xb):
    j = pl.program_id(1)
    nv = nv_ref[pl.program_id(0)]
    half = xb.shape[0] // 2

    @pl.when(j == 0)
    def _():
        ys_ref[...] = jnp.zeros_like(ys_ref)
        xb[...] = xs_ref[...].astype(BF16)

    @pl.when(nv > half)
    def _():
        _swiglu_accumulate(xb, w1_ref, w3_ref, w2_ref, ys_ref, 2 * half)

    @pl.when((nv > 0) & (nv <= half))
    def _():
        _swiglu_accumulate(xb, w1_ref, w3_ref, w2_ref, ys_ref, half)


def _moe_experts(tile_expert, n_valid, xs, w1, w3, w2):
    P, D = xs.shape
    E, _, F = w1.shape
    tm, tf = MOE_TILE, 256
    nf = F // tf
    fj = lambda i, j, nv: jnp.where(nv[i] > 0, j, nf - 1)
    return pl.pallas_call(
        _moe_expert_kernel, name="moe_experts",
        out_shape=jax.ShapeDtypeStruct((P, D), F32),
        grid_spec=pltpu.PrefetchScalarGridSpec(
            num_scalar_prefetch=2,
            grid=(P // tm, nf),
            in_specs=[
                pl.BlockSpec((tm, D), lambda i, j, te, nv: (i, 0)),
                pl.BlockSpec((1, D, tf), lambda i, j, te, nv: (te[i], 0, fj(i, j, nv))),
                pl.BlockSpec((1, D, tf), lambda i, j, te, nv: (te[i], 0, fj(i, j, nv))),
                pl.BlockSpec((1, tf, D), lambda i, j, te, nv: (te[i], fj(i, j, nv), 0)),
            ],
            out_specs=pl.BlockSpec((tm, D), lambda i, j, te, nv: (i, 0)),
            scratch_shapes=[pltpu.VMEM((tm, D), BF16)],
        ),
        compiler_params=_cparams(("arbitrary", "arbitrary")),
    )(tile_expert, n_valid, xs, w1, w3, w2)


def _moe_combine_kernel(s1_ref, s2_ref, ys_ref, info_ref, x_ref, g2_ref, o_ref, a_buf, b_buf, sem, *, tm):
    base = pl.program_id(0) * tm

    def copies(r):
        return (pltpu.make_async_copy(ys_ref.at[pl.ds(s1_ref[base + r], 1)], a_buf.at[pl.ds(r, 1)], sem.at[0]),
                pltpu.make_async_copy(ys_ref.at[pl.ds(s2_ref[base + r], 1)], b_buf.at[pl.ds(r, 1)], sem.at[1]))

    def start(r, carry):
        for cp in copies(r):
            cp.start()
        return carry

    def wait(r, carry):
        for cp in copies(r):
            cp.wait()
        return carry

    lax.fori_loop(0, tm, start, 0, unroll=8)
    lax.fori_loop(0, tm, wait, 0, unroll=8)
    info = info_ref[...]
    p1 = info[:, 4:5]
    p2 = info[:, 5:6]
    o_ref[...] = x_ref[...] + g2_ref[0] * (p1 * a_buf[...] + p2 * b_buf[...])


def _moe_combine(slot1, slot2, ys, info, x2, g2, seq):
    M, D = x2.shape
    tm = 256
    return pl.pallas_call(
        functools.partial(_moe_combine_kernel, tm=tm), name="moe_combine",
        out_shape=jax.ShapeDtypeStruct((M, D), F32),
        grid_spec=pltpu.PrefetchScalarGridSpec(
            num_scalar_prefetch=2,
            grid=(M // tm,),
            in_specs=[
                pl.BlockSpec(memory_space=pl.ANY),
                pl.BlockSpec((tm, LANE), lambda i, s1, s2: (i, 0)),
                pl.BlockSpec((tm, D), lambda i, s1, s2: (i, 0)),
                pl.BlockSpec((1, 1, D), lambda i, s1, s2: (i * tm // seq, 0, 0)),
            ],
            out_specs=pl.BlockSpec((tm, D), lambda i, s1, s2: (i, 0)),
            scratch_shapes=[pltpu.VMEM((tm, D), F32), pltpu.VMEM((tm, D), F32),
                            pltpu.SemaphoreType.DMA((2,))],
        ),
        compiler_params=_cparams(("arbitrary",)),
    )(slot1, slot2, ys, info, x2, g2)


def _moe(h2, rw, rb, w1, w3, w2, x2, g2, seq):
    M, D = x2.shape
    E = w1.shape[0]
    T = MOE_TILE
    n_rows = 2 * M + E * T
    n_tiles = n_rows // T
    info, counts = _router(h2, rw, rb)

    e1, e2, rank1, rank2 = (info[:, c].astype(jnp.int32) for c in range(4))
    cnt = counts[0, :E].astype(jnp.int32)
    padded = (cnt + T - 1) // T * T
    ends = jnp.cumsum(padded)
    off = ends - padded
    expert_ids = jnp.arange(E, dtype=jnp.int32)
    offset_of = lambda e: jnp.sum(jnp.where(e[:, None] == expert_ids[None, :], off[None, :], 0), axis=1)
    slot1 = offset_of(e1) + rank1
    slot2 = offset_of(e2) + rank2
    tile_start = jnp.arange(n_tiles, dtype=jnp.int32) * T
    in_use = tile_start < ends[E - 1]
    clamped = jnp.minimum(tile_start, ends[E - 1] - T)
    tile_expert = jnp.sum(clamped[:, None] >= ends[None, :], axis=1).astype(jnp.int32)
    is_e = tile_expert[:, None] == expert_ids[None, :]
    tokens_end = jnp.sum(jnp.where(is_e, (off + cnt)[None, :], 0), axis=1)
    n_valid = jnp.where(in_use, jnp.clip(tokens_end - tile_start, 0, T), 0).astype(jnp.int32)

    xs = _moe_dispatch(slot1, slot2, h2, n_rows)
    ys = _moe_experts(tile_expert, n_valid, xs, w1, w3, w2)
    return _moe_combine(slot1, slot2, ys, info, x2, g2, seq)


def _mixing_layer(x2, mods, p, w_t, layer, batch, seq, h_dtype):
    row = lambda t: t.reshape(1, -1)
    hz = _inproj(x2, row(p["norm1_g"]), mods["sc1"], mods["sh1"], w_t, layer, seq)

    mu = p["shift_mu"]
    mu3 = mu[:3 * RW_W].reshape(3, RW_W)
    mul = row(mu[3 * RW_W:])
    r, lw, k, v, kk, a, g = _rwkv_prep(
        hz, mu3, mul, row(p["rw_w0"]), p["rw_w2"].astype(BF16), row(p["rw_a0"]),
        p["rw_a2"].astype(BF16), p["rw_g2"].astype(BF16), row(p["rw_k_k"]), row(p["rw_k_a"]), seq)
    hm = lambda t: t.reshape(RW_H, 1, RW_HD)
    o_a = _rwkv_scan(r, lw, k, v, kk, a, g, hm(p["rw_r_k"]), hm(p["rw_lnx_g"]), hm(p["rw_lnx_b"]),
                     batch, seq)

    fb128 = jnp.zeros((1, LANE), F32).at[0, :FOX_H].set(p["fox_fb"])
    kx, qxT, vT = _fox_prep(hz, row(p["fox_qn_g"]), row(p["fox_kn_g"]), fb128, seq)
    o_b = _fox_attn(kx, qxT, vT, batch, seq)

    o_c = _sgu(hz, p["sg_ln_g"], p["sg_ln_b"], p["sg_ws"], jnp.transpose(p["sg_b"]))

    return _outproj(o_a, o_b, o_c, p["w_out"].astype(BF16), x2, mods["g1"], row(p["norm2_g"]),
                    mods["sc2"], mods["sh2"], seq, h_dtype)


def kernel(x, c, ada_w, ada_b, norm1_g, norm2_g, w_in, shift_mu, rw_w0, rw_w2, rw_a0, rw_a2, rw_g2, rw_k_k, rw_k_a, rw_r_k, rw_lnx_g, rw_lnx_b, fox_qn_g, fox_kn_g, fox_fb, sg_ln_g, sg_ln_b, sg_ws, sg_b, w_out, ffn_w1, ffn_w3, ffn_w2, moe_router_w, moe_router_b, moe_w1, moe_w3, moe_w2):
    B, S, D = x.shape
    L = ada_w.shape[0]
    x2 = x.reshape(B * S, D)
    c8 = jnp.zeros((8, D), F32).at[:B].set(c)
    mod = _ada_mod(c8, ada_w, ada_b.reshape(L, 1, N_MOD * D))

    layer_params = dict(
        norm1_g=norm1_g, norm2_g=norm2_g, shift_mu=shift_mu, rw_w0=rw_w0, rw_w2=rw_w2,
        rw_a0=rw_a0, rw_a2=rw_a2, rw_g2=rw_g2, rw_k_k=rw_k_k, rw_k_a=rw_k_a, rw_r_k=rw_r_k,
        rw_lnx_g=rw_lnx_g, rw_lnx_b=rw_lnx_b, fox_qn_g=fox_qn_g, fox_kn_g=fox_kn_g, fox_fb=fox_fb,
        sg_ln_g=sg_ln_g, sg_ln_b=sg_ln_b, sg_ws=sg_ws, sg_b=sg_b, w_out=w_out)

    w_t = jnp.transpose(w_in, (2, 0, 1))
    for l in range(L):
        names = ("sh1", "sc1", "g1", "sh2", "sc2", "g2")
        mods = {n: mod[l, :B, i * D:(i + 1) * D].reshape(B, 1, D) for i, n in enumerate(names)}
        p = {n: t[l] for n, t in layer_params.items()}
        dense = l % 2 == 0
        x2, h2 = _mixing_layer(x2, mods, p, w_t, l, B, S, BF16 if dense else F32)
        j = l // 2
        if dense:
            x2 = _ffn(h2, ffn_w1, ffn_w3, ffn_w2, j, x2, mods["g2"], S)
        else:
            rw = jnp.zeros((D, LANE), BF16).at[:, :N_EXPERTS].set(moe_router_w[j].astype(BF16))
            rb = jnp.zeros((1, LANE), F32).at[0, :N_EXPERTS].set(moe_router_b[j])
            x2 = _moe(h2, rw, rb, moe_w1[j], moe_w3[j], moe_w2[j], x2, mods["g2"], S)
    return x2.reshape(B, S, D)
```

```python
import functools

import jax
import jax.numpy as jnp
from jax import lax
from jax.experimental import pallas as pl
from jax.experimental.pallas import tpu as pltpu

F32 = jnp.float32
BF16 = jnp.bfloat16

D_MODEL = 2048
DEPTH = 2
RW_HD = 64
RW_W = 768
RW_H = 12
DECAY_LORA = 64
AAA_LORA = 64
GATE_LORA = 128
FOX_HD = 128
FOX_W = 768
FOX_H = 6
SG_GD = 128
SG_W = 512
SG_G = 4
SG_CHUNK = 128
N_EXPERTS = 8
N_MOD = 6
RMS_EPS = 1e-6
LN_EPS = 1e-5
LNX_EPS = 64e-5

C_R = 0
C_RWKV_END = 3 * RW_W + DECAY_LORA + AAA_LORA + GATE_LORA
C_Q = C_RWKV_END
C_FF = C_Q + 3 * FOX_W
C_U = C_FF + FOX_H
C_SV = C_U + SG_W
N_IN = C_SV + SG_W

P_R, P_K, P_V = 0, 768, 1536
P_LORA = 2304
P_Q, P_FK, P_FV = 2560, 3328, 4096
P_FF = C_FF
P_U = 5120
P_SV = P_U + SG_W
N_PACK = P_SV + SG_W
COL_TILE = 256

LANE = 128
VMEM_LIMIT = 56 * 1024 * 1024


def _cparams(sem):
    return pltpu.CompilerParams(dimension_semantics=sem, vmem_limit_bytes=VMEM_LIMIT)


def _sigmoid(x):
    return 1.0 / (1.0 + jnp.exp(-x))


def _softplus(x):
    return jnp.maximum(x, 0.0) + jnp.log(1.0 + jnp.exp(-jnp.abs(x)))


def _gelu_tanh(x):
    return 0.5 * x * (1.0 + jnp.tanh(0.7978845608028654 * (x + 0.044715 * (x * x * x))))


def _silu(x):
    return x * _sigmoid(x)


def _ada_kernel(c_ref, w_ref, b_ref, o_ref):
    cs = _silu(c_ref[...]).astype(BF16)
    w = w_ref[0].astype(BF16)
    o_ref[0] = jnp.dot(cs, w, preferred_element_type=F32) + b_ref[0]


def _ada_mod(c8, ada_w, ada_b3):
    L, D, N = ada_w.shape
    tn = 1536
    return pl.pallas_call(
        _ada_kernel, name="ada_mod",
        out_shape=jax.ShapeDtypeStruct((L, 8, N), F32),
        grid=(L, N // tn),
        in_specs=[
            pl.BlockSpec((8, D), lambda l, j: (0, 0)),
            pl.BlockSpec((1, D, tn), lambda l, j: (l, 0, j)),
            pl.BlockSpec((1, 1, tn), lambda l, j: (l, 0, j)),
        ],
        out_specs=pl.BlockSpec((1, 8, tn), lambda l, j: (l, 0, j)),
        compiler_params=_cparams(("parallel", "parallel")),
    )(c8, ada_w, ada_b3)


def _col_spec(tm, width, start):
    assert start % width == 0
    return pl.BlockSpec((tm, width), lambda i, *_: (i, start // width))


def _inproj_kernel(x_ref, g_ref, sc_ref, sh_ref, w_hbm, o_ref, h_scr, wbuf, sem, *, layer, tn):
    i, j = pl.program_id(0), pl.program_id(1)
    ni, nj = pl.num_programs(0), pl.num_programs(1)
    step = i * nj + j
    slot = step % 2

    def weight_copy(jj, s):
        first_row = jnp.where(jj < nj - 2, jj * tn, N_IN - (nj - jj) * tn)
        return pltpu.make_async_copy(w_hbm.at[pl.ds(first_row, tn), layer], wbuf.at[s], sem.at[s])

    @pl.when(step == 0)
    def _():
        weight_copy(j, slot).start()

    @pl.when(step + 1 < ni * nj)
    def _():
        weight_copy(jnp.where(j + 1 < nj, j + 1, 0), 1 - slot).start()

    @pl.when(j == 0)
    def _():
        chunk = 512
        for c in range(x_ref.shape[0] // chunk):
            rows = pl.ds(c * chunk, chunk)
            x = x_ref[rows, :]
            ms = jnp.mean(x * x, axis=-1, keepdims=True)
            y = x * lax.rsqrt(ms + RMS_EPS) * g_ref[...]
            h_scr[rows, :] = (y * (1.0 + sc_ref[0]) + sh_ref[0]).astype(BF16)

    weight_copy(j, slot).wait()
    o_ref[...] = lax.dot_general(h_scr[...], wbuf[slot].astype(BF16), (((1,), (1,)), ((), ())),
                                 preferred_element_type=F32)


def _inproj(x2, g, sc, sh, w_t, layer, seq):
    M, D = x2.shape
    tm, tn = min(2048, seq), SG_W
    nj = N_PACK // tn
    assert (nj - 2) * tn == P_U and P_FF + FOX_H <= P_U
    return pl.pallas_call(
        functools.partial(_inproj_kernel, layer=layer, tn=tn), name="inproj",
        out_shape=jax.ShapeDtypeStruct((M, N_PACK), F32),
        grid=(M // tm, nj),
        in_specs=[
            pl.BlockSpec((tm, D), lambda i, j: (i, 0), pipeline_mode=pl.Buffered(1)),
            pl.BlockSpec((1, D), lambda i, j: (0, 0)),
            pl.BlockSpec((1, 1, D), lambda i, j: (i * tm // seq, 0, 0)),
            pl.BlockSpec((1, 1, D), lambda i, j: (i * tm // seq, 0, 0)),
            pl.BlockSpec(memory_space=pl.ANY),
        ],
        out_specs=pl.BlockSpec((tm, tn), lambda i, j: (i, j)),
        scratch_shapes=[pltpu.VMEM((tm, D), BF16), pltpu.VMEM((2, tn, D), F32),
                        pltpu.SemaphoreType.DMA((2,))],
        compiler_params=_cparams(("arbitrary", "arbitrary")),
    )(x2, g, sc, sh, w_t)


def _rwkv_prep_kernel(zr_ref, zk_ref, zv_ref, zl_ref, pr_ref, pk_ref, pv_ref, pl_ref,
                      mu_ref, mul_ref, w0_ref, w2_ref, a0_ref, a2_ref, g2_ref, kk_ref, ka_ref,
                      r_out, lw_out, k_out, v_out, kk_out, a_out, g_out, *, tm, seq):
    first = (pl.program_id(0) * tm) % seq == 0

    def shift(z_ref, p_ref, mu):
        z = z_ref[...]
        prev_last = jnp.where(first, 0.0, p_ref[7:8, :])
        zp = pltpu.roll(z, 1, 0)
        row = lax.broadcasted_iota(jnp.int32, z.shape, 0)
        zp = jnp.where(row == 0, prev_last, zp)
        return z + (zp - z) * mu

    r = shift(zr_ref, pr_ref, mu_ref[0:1, :])
    k = shift(zk_ref, pk_ref, mu_ref[1:2, :])
    v = shift(zv_ref, pv_ref, mu_ref[2:3, :])
    lo = shift(zl_ref, pl_ref, mul_ref[...])
    wd = lo[:, 0:DECAY_LORA]
    ad = lo[:, DECAY_LORA:DECAY_LORA + AAA_LORA]
    gd = lo[:, DECAY_LORA + AAA_LORA:]

    dec = w0_ref[...] + jnp.dot(jnp.tanh(wd).astype(BF16), w2_ref[...], preferred_element_type=F32)
    w_log = -_softplus(-dec) - 0.5
    lw = -jnp.exp(w_log)
    a = _sigmoid(a0_ref[...] + jnp.dot(ad.astype(BF16), a2_ref[...], preferred_element_type=F32))
    g = jnp.dot(_sigmoid(gd).astype(BF16), g2_ref[...], preferred_element_type=F32)
    kk = k * kk_ref[...]
    km = k * (1.0 + (a - 1.0) * ka_ref[...])

    for h in range(RW_H):
        sl = slice(h * RW_HD, (h + 1) * RW_HD)
        r_out[h] = r[:, sl]
        lw_out[h] = lw[:, sl]
        k_out[h] = km[:, sl]
        v_out[h] = v[:, sl]
        kk_out[h] = kk[:, sl]
        a_out[h] = a[:, sl]
        g_out[h] = g[:, sl]


def _rwkv_prep(hz, mu3, mul, w0, w2, a0, a2, g2, k_k, k_a, seq):
    M = hz.shape[0]
    tm = 256
    prev = lambda i: jnp.maximum(i * (tm // 8) - 1, 0)
    row = lambda n: pl.BlockSpec((1, n), lambda i: (0, 0))
    in_specs = [
        pl.BlockSpec((tm, RW_W), lambda i: (i, P_R // RW_W)),
        pl.BlockSpec((tm, RW_W), lambda i: (i, P_K // RW_W)),
        pl.BlockSpec((tm, RW_W), lambda i: (i, P_V // RW_W)),
        pl.BlockSpec((tm, 256), lambda i: (i, P_LORA // 256)),
        pl.BlockSpec((8, RW_W), lambda i: (prev(i), P_R // RW_W)),
        pl.BlockSpec((8, RW_W), lambda i: (prev(i), P_K // RW_W)),
        pl.BlockSpec((8, RW_W), lambda i: (prev(i), P_V // RW_W)),
        pl.BlockSpec((8, 256), lambda i: (prev(i), P_LORA // 256)),
        pl.BlockSpec((3, RW_W), lambda i: (0, 0)),
        row(256), row(RW_W),
        pl.BlockSpec((DECAY_LORA, RW_W), lambda i: (0, 0)),
        row(RW_W),
        pl.BlockSpec((AAA_LORA, RW_W), lambda i: (0, 0)),
        pl.BlockSpec((GATE_LORA, RW_W), lambda i: (0, 0)),
        row(RW_W), row(RW_W),
    ]
    hm = jax.ShapeDtypeStruct((RW_H, M, RW_HD), F32)
    hm_spec = pl.BlockSpec((RW_H, tm, RW_HD), lambda i: (0, i, 0))
    return pl.pallas_call(
        functools.partial(_rwkv_prep_kernel, tm=tm, seq=seq), name="rwkv_prep",
        out_shape=[hm] * 7,
        grid=(M // tm,),
        in_specs=in_specs,
        out_specs=[hm_spec] * 7,
        compiler_params=_cparams(("parallel",)),
    )(hz, hz, hz, hz, hz, hz, hz, hz, mu3, mul, w0, w2, a0, a2, g2, k_k, k_a)


def _bmm(a, b, ca, cb):
    return lax.dot_general(a.astype(BF16), b.astype(BF16), (((ca,), (cb,)), ((0,), (0,))),
                           preferred_element_type=F32)


def _split3(x):
    hi = x.astype(BF16)
    r1 = x - hi.astype(F32)
    mid = r1.astype(BF16)
    lo = (r1 - mid.astype(F32)).astype(BF16)
    return hi, mid, lo


def _rwkv_scan_kernel(r_ref, lw_ref, k_ref, v_ref, kk_ref, a_ref, g_ref, rk_ref, lng_ref, lnb_ref,
                      o_ref, s_scr, *, L, nC):
    H, K = RW_H, RW_HD
    n = H * nC

    @pl.when(pl.program_id(1) == 0)
    def _():
        s_scr[...] = jnp.zeros_like(s_scr)

    ld = lambda ref: ref[...].reshape(n, L, K)
    r, lw, k, v, kk, a = ld(r_ref), ld(lw_ref), ld(k_ref), ld(v_ref), ld(kk_ref), ld(a_ref)

    kk = kk / jnp.maximum(jnp.sqrt(jnp.sum(kk * kk, axis=-1, keepdims=True)), 1e-12)
    b = kk * a

    ti = lax.broadcasted_iota(jnp.int32, (L, L), 0)
    si = lax.broadcasted_iota(jnp.int32, (L, L), 1)
    tri_incl = jnp.broadcast_to((ti >= si).astype(BF16), (n, L, L))
    cw = sum(_bmm(tri_incl, piece, 2, 1) for piece in reversed(_split3(lw)))
    cw_last = cw[:, L - 1:L, :]
    e_in = jnp.exp(cw)
    e_out = jnp.exp(-cw)
    e_ex = jnp.exp(cw - lw)
    e_end = jnp.exp(cw_last - cw)
    w_end = jnp.exp(cw_last)

    at = -kk * e_ex
    rt = r * e_in
    bb = b * e_out
    kb = k * e_out
    bh = b * e_end
    kh = k * e_end

    atrt = jnp.concatenate([at, rt], axis=1)
    t2 = lax.broadcasted_iota(jnp.int32, (2 * L, L), 0)
    s2 = lax.broadcasted_iota(jnp.int32, (2 * L, L), 1)
    causal = (((t2 & (L - 1)) - s2 + jnp.where(t2 < L, 0, 1)) > 0)[None]
    pb = jnp.where(causal, _bmm(atrt, bb, 2, 2), 0.0)
    pk = jnp.where(causal, _bmm(atrt, kb, 2, 2), 0.0)
    a_ab, b_rb = pb[:, :L], pb[:, L:]
    a_ak, b_rk = pk[:, :L], pk[:, L:]

    x = jnp.broadcast_to((ti == si).astype(F32), (n, L, L))
    size = 1
    while size < L:
        sh = size.bit_length() - 1
        m = (((ti >> (sh + 1)) == (si >> (sh + 1))) & (((ti >> sh) & 1) == 1) & (((si >> sh) & 1) == 0))[None]
        x = x + _bmm(_bmm(x, jnp.where(m, a_ab, 0.0), 2, 1), x, 2, 1)
        size *= 2

    xu = _bmm(x, jnp.concatenate([at, _bmm(a_ak, v, 2, 1)], axis=-1), 2, 1)
    w = _bmm(b_rb, xu, 2, 1)
    rp = rt + w[..., :K]
    y0 = w[..., K:] + _bmm(b_rk, v, 2, 1)
    big = _bmm(xu, bh, 1, 1)
    gp = big[:, :K]
    cc = big[:, K:] + _bmm(v, kh, 1, 1)

    c4 = lambda t: t.reshape((H, nC) + t.shape[1:])
    rp, y0, gp, cc, w_end = c4(rp), c4(y0), c4(gp), c4(cc), c4(w_end)
    s = s_scr[...]
    ys = []
    for c in range(nC):
        ys.append(_bmm(rp[:, c], s, 2, 2) + y0[:, c])
        s = s * w_end[:, c] + _bmm(s, gp[:, c], 2, 1) + cc[:, c]
    s_scr[...] = s
    y = ys[0] if nC == 1 else jnp.concatenate(ys, axis=1)

    mu = jnp.mean(y, axis=-1, keepdims=True)
    yc = y - mu
    var = jnp.mean(yc * yc, axis=-1, keepdims=True)
    yn = yc * lax.rsqrt(var + LNX_EPS) * lng_ref[...] + lnb_ref[...]
    r3, k3, v3 = r_ref[...], k_ref[...], v_ref[...]
    bonus = jnp.sum(r3 * k3 * rk_ref[...], axis=-1, keepdims=True) * v3
    out = (yn + bonus) * g_ref[...]
    for p in range(H // 2):
        pair = jnp.concatenate([out[2 * p], out[2 * p + 1]], axis=-1)
        o_ref[:, p * LANE:(p + 1) * LANE] = pair.astype(o_ref.dtype)


def _rwkv_scan(r, lw, k, v, kk, a, g, r_k, lnx_g, lnx_b, batch, seq):
    M = r.shape[1]
    L, nC = 64, 2
    tb = L * nC
    nj = seq // tb
    hm_spec = pl.BlockSpec((RW_H, tb, RW_HD), lambda b, j: (0, b * nj + j, 0))
    par = pl.BlockSpec((RW_H, 1, RW_HD), lambda b, j: (0, 0, 0))
    return pl.pallas_call(
        functools.partial(_rwkv_scan_kernel, L=L, nC=nC), name="rwkv_scan",
        out_shape=jax.ShapeDtypeStruct((M, RW_W), BF16),
        grid=(batch, nj),
        in_specs=[hm_spec] * 7 + [par] * 3,
        out_specs=pl.BlockSpec((tb, RW_W), lambda b, j: (b * nj + j, 0)),
        scratch_shapes=[pltpu.VMEM((RW_H, RW_HD, RW_HD), F32)],
        compiler_params=_cparams(("parallel", "arbitrary")),
    )(r, lw, k, v, kk, a, g, r_k, lnx_g, lnx_b)


FOX_XD = 2 * FOX_HD


def _fox_prep_kernel(*refs, tm, seq):
    q_refs, k_refs, v_refs = refs[0:3], refs[3:6], refs[6:9]
    f_ref, qg_ref, kg_ref, fb_ref, kx_ref, qxT_ref, vT_ref, carry = refs[9:]

    @pl.when((pl.program_id(0) * tm) % seq == 0)
    def _():
        carry[...] = jnp.zeros_like(carry)

    logf = -_softplus(-(f_ref[...] + fb_ref[...]))
    ti = lax.broadcasted_iota(jnp.int32, (tm, tm), 0)
    si = lax.broadcasted_iota(jnp.int32, (tm, tm), 1)
    tri = (ti >= si).astype(BF16)
    cum = carry[...]
    for piece in reversed(_split3(logf)):
        cum = cum + jnp.dot(tri, piece, preferred_element_type=F32)
    carry[...] = cum[tm - 1:tm, :]
    cum_t = cum.T

    scale = FOX_HD ** -0.5
    lane = lax.broadcasted_iota(jnp.int32, (tm, FOX_HD), 1)
    sub = lax.broadcasted_iota(jnp.int32, (FOX_HD, tm), 0)
    pieces = lambda x: [t.astype(F32) for t in _split3(x)]
    for h in range(FOX_H):
        blk, half = h // 2, slice((h % 2) * FOX_HD, (h % 2 + 1) * FOX_HD)
        q = q_refs[blk][:, half]
        k = k_refs[blk][:, half]
        qn = q * lax.rsqrt(jnp.mean(q * q, axis=-1, keepdims=True) + RMS_EPS) * qg_ref[...]
        kn = k * lax.rsqrt(jnp.mean(k * k, axis=-1, keepdims=True) + RMS_EPS) * kg_ref[...]

        c_hi, c_mid, c_lo = pieces(cum[:, h:h + 1])
        k_extra = jnp.where(lane < 3, 1.0, 0.0)
        k_extra = jnp.where(lane == 3, -c_hi, k_extra)
        k_extra = jnp.where(lane == 4, -c_mid, k_extra)
        k_extra = jnp.where(lane == 5, -c_lo, k_extra)
        kx_ref[:, h * FOX_XD:h * FOX_XD + FOX_HD] = kn.astype(BF16)
        kx_ref[:, h * FOX_XD + FOX_HD:(h + 1) * FOX_XD] = k_extra.astype(BF16)

        r_hi, r_mid, r_lo = pieces(cum_t[h:h + 1, :])
        q_extra = jnp.where(sub < 6, 1.0, 0.0)
        q_extra = jnp.where(sub == 0, r_hi, q_extra)
        q_extra = jnp.where(sub == 1, r_mid, q_extra)
        q_extra = jnp.where(sub == 2, r_lo, q_extra)
        qxT_ref[h * FOX_XD:h * FOX_XD + FOX_HD, :] = (qn * scale).T.astype(BF16)
        qxT_ref[h * FOX_XD + FOX_HD:(h + 1) * FOX_XD, :] = q_extra.astype(BF16)

        vT_ref[h * FOX_HD:(h + 1) * FOX_HD, :] = v_refs[blk][:, half].T.astype(BF16)


def _fox_prep(hz, qn_g, kn_g, fb128, seq):
    M = hz.shape[0]
    tm = 256
    row = pl.BlockSpec((1, LANE), lambda i: (0, 0))
    pieces = lambda start: [_col_spec(tm, COL_TILE, start + c * COL_TILE) for c in range(FOX_W // COL_TILE)]
    return pl.pallas_call(
        functools.partial(_fox_prep_kernel, tm=tm, seq=seq), name="fox_prep",
        out_shape=[jax.ShapeDtypeStruct((M, FOX_H * FOX_XD), BF16),
                   jax.ShapeDtypeStruct((FOX_H * FOX_XD, M), BF16),
                   jax.ShapeDtypeStruct((FOX_W, M), BF16)],
        grid=(M // tm,),
        in_specs=pieces(P_Q) + pieces(P_FK) + pieces(P_FV) + [_col_spec(tm, LANE, P_FF), row, row, row],
        out_specs=[pl.BlockSpec((tm, FOX_H * FOX_XD), lambda i: (i, 0)),
                   pl.BlockSpec((FOX_H * FOX_XD, tm), lambda i: (0, i)),
                   pl.BlockSpec((FOX_W, tm), lambda i: (0, i))],
        scratch_shapes=[pltpu.VMEM((1, LANE), F32)],
        compiler_params=_cparams(("arbitrary",)),
    )(*([hz] * 10), qn_g, kn_g, fb128)


def _fox_attn_kernel(qt_ref, kt_ref, kx_ref, qxT_ref, vT_ref, o_ref, m_scr, l_scr, acc_scr, *, tq, tk):
    step_id = pl.program_id(1)
    qi, ki = qt_ref[step_id], kt_ref[step_id]
    ratio = tq // tk
    first_diag = ratio * qi
    last = ki == first_diag + ratio - 1

    @pl.when(ki == 0)
    def _():
        m_scr[...] = jnp.full_like(m_scr, -1e30)
        l_scr[...] = jnp.zeros_like(l_scr)
        acc_scr[...] = jnp.zeros_like(acc_scr)

    def step(diagonal):
        if diagonal:
            keep = (lax.broadcasted_iota(jnp.int32, (tk, tq), 1) - lax.broadcasted_iota(jnp.int32, (tk, tq), 0)
                    >= ki * tk - qi * tq)
        for h in range(FOX_H):
            xs = slice(h * FOX_XD, (h + 1) * FOX_XD)
            hs = slice(h * FOX_HD, (h + 1) * FOX_HD)
            s = jnp.dot(kx_ref[:, xs], qxT_ref[xs, :], preferred_element_type=F32)
            if diagonal:
                s = jnp.where(keep, s, -1e30)
            m_prev = m_scr[h]
            m_new = jnp.maximum(m_prev, jnp.max(s, axis=0, keepdims=True))
            alpha = jnp.exp(m_prev - m_new)
            p = jnp.exp(s - m_new)
            m_scr[h] = m_new
            l_scr[h] = alpha * l_scr[h] + jnp.sum(p, axis=0, keepdims=True)
            acc_scr[hs, :] = alpha * acc_scr[hs, :] + jnp.dot(vT_ref[hs, :], p.astype(BF16),
                                                              preferred_element_type=F32)

    pl.when(ki < first_diag)(lambda: step(False))
    pl.when(ki >= first_diag)(lambda: step(True))

    @pl.when(last)
    def _():
        for h in range(FOX_H):
            hs = slice(h * FOX_HD, (h + 1) * FOX_HD)
            o_ref[:, hs] = (acc_scr[hs, :] / l_scr[h]).T.astype(o_ref.dtype)


def _fox_attn(kx, qxT, vT, batch, seq):
    M = kx.shape[0]
    tq, tk = min(1024, seq), 512
    nq, nk, ratio = seq // tq, seq // tk, tq // tk
    pairs = [(q, k) for q in range(nq) for k in range(ratio * (q + 1))]
    q_tab = jnp.asarray([q for q, _ in pairs], jnp.int32)
    k_tab = jnp.asarray([k for _, k in pairs], jnp.int32)
    return pl.pallas_call(
        functools.partial(_fox_attn_kernel, tq=tq, tk=tk), name="fox_attn",
        out_shape=jax.ShapeDtypeStruct((M, FOX_W), BF16),
        grid_spec=pltpu.PrefetchScalarGridSpec(
            num_scalar_prefetch=2,
            grid=(batch, len(pairs)),
            in_specs=[
                pl.BlockSpec((tk, FOX_H * FOX_XD), lambda b, s, qt, kt: (b * nk + kt[s], 0)),
                pl.BlockSpec((FOX_H * FOX_XD, tq), lambda b, s, qt, kt: (0, b * nq + qt[s])),
                pl.BlockSpec((FOX_W, tk), lambda b, s, qt, kt: (0, b * nk + kt[s])),
            ],
            out_specs=pl.BlockSpec((tq, FOX_W), lambda b, s, qt, kt: (b * nq + qt[s], 0)),
            scratch_shapes=[pltpu.VMEM((FOX_H, 1, tq), F32), pltpu.VMEM((FOX_H, 1, tq), F32),
                            pltpu.VMEM((FOX_W, tq), F32)],
        ),
        compiler_params=_cparams(("parallel", "arbitrary")),
    )(q_tab, k_tab, kx, qxT, vT)


def _sgu_kernel(u_ref, v_ref, lg_ref, lb_ref, ws_ref, sb_ref, o_ref, *, tm):
    ti = lax.broadcasted_iota(jnp.int32, (SG_CHUNK, SG_CHUNK), 0)
    si = lax.broadcasted_iota(jnp.int32, (SG_CHUNK, SG_CHUNK), 1)
    for g in range(SG_G):
        sl = slice(g * SG_GD, (g + 1) * SG_GD)
        u = _gelu_tanh(u_ref[:, sl])
        v = _gelu_tanh(v_ref[:, sl])
        mu = jnp.mean(v, axis=-1, keepdims=True)
        vc = v - mu
        var = jnp.mean(vc * vc, axis=-1, keepdims=True)
        vn = (vc * lax.rsqrt(var + LN_EPS) * lg_ref[g:g + 1, :] + lb_ref[g:g + 1, :]).astype(BF16)
        ws = jnp.where(ti >= si, ws_ref[g], 0.0).astype(BF16)
        bias = sb_ref[:, g:g + 1]
        for c in range(tm // SG_CHUNK):
            rows = slice(c * SG_CHUNK, (c + 1) * SG_CHUNK)
            mixed = jnp.dot(ws, vn[rows], preferred_element_type=F32) + bias
            o_ref[rows, sl] = (u[rows] * mixed).astype(o_ref.dtype)


def _sgu(hz, ln_g, ln_b, ws, sb_t):
    M = hz.shape[0]
    tm = 512
    return pl.pallas_call(
        functools.partial(_sgu_kernel, tm=tm), name="sgu",
        out_shape=jax.ShapeDtypeStruct((M, SG_W), BF16),
        grid=(M // tm,),
        in_specs=[
            _col_spec(tm, SG_W, P_U), _col_spec(tm, SG_W, P_SV),
            pl.BlockSpec((SG_G, SG_GD), lambda i: (0, 0)),
            pl.BlockSpec((SG_G, SG_GD), lambda i: (0, 0)),
            pl.BlockSpec((SG_G, SG_CHUNK, SG_CHUNK), lambda i: (0, 0, 0)),
            pl.BlockSpec((SG_CHUNK, SG_G), lambda i: (0, 0)),
        ],
        out_specs=pl.BlockSpec((tm, SG_W), lambda i: (i, 0)),
        compiler_params=_cparams(("parallel",)),
    )(hz, hz, ln_g, ln_b, ws, sb_t)


def _outproj_kernel(oa_ref, ob_ref, oc_ref, w_ref, x_ref, g1_ref, ng_ref, sc_ref, sh_ref,
                    xo_ref, h_ref):
    mix = jnp.dot(oa_ref[...], w_ref[0:RW_W, :], preferred_element_type=F32)
    mix += jnp.dot(ob_ref[...], w_ref[RW_W:RW_W + FOX_W, :], preferred_element_type=F32)
    mix += jnp.dot(oc_ref[...], w_ref[RW_W + FOX_W:, :], preferred_element_type=F32)
    x = x_ref[...] + g1_ref[0] * mix
    xo_ref[...] = x
    ms = jnp.mean(x * x, axis=-1, keepdims=True)
    y = x * lax.rsqrt(ms + RMS_EPS) * ng_ref[...]
    h_ref[...] = (y * (1.0 + sc_ref[0]) + sh_ref[0]).astype(h_ref.dtype)


def _outproj(oa, ob, oc, w, x2, g1, ng, sc, sh, seq, h_dtype):
    M, D = x2.shape
    tm = 512
    bidx = lambda i: (i * tm // seq, 0, 0)
    mod = pl.BlockSpec((1, 1, D), bidx)
    return pl.pallas_call(
        _outproj_kernel, name="outproj",
        out_shape=[jax.ShapeDtypeStruct((M, D), F32), jax.ShapeDtypeStruct((M, D), h_dtype)],
        grid=(M // tm,),
        in_specs=[
            pl.BlockSpec((tm, RW_W), lambda i: (i, 0)),
            pl.BlockSpec((tm, FOX_W), lambda i: (i, 0)),
            pl.BlockSpec((tm, SG_W), lambda i: (i, 0)),
            pl.BlockSpec((D, D), lambda i: (0, 0)),
            pl.BlockSpec((tm, D), lambda i: (i, 0)),
            mod,
            pl.BlockSpec((1, D), lambda i: (0, 0)),
            mod, mod,
        ],
        out_specs=[pl.BlockSpec((tm, D), lambda i: (i, 0)), pl.BlockSpec((tm, D), lambda i: (i, 0))],
        compiler_params=_cparams(("parallel",)),
    )(oa, ob, oc, w, x2, g1, ng, sc, sh)


def _swiglu_accumulate(h_ref, w1_ref, w3_ref, w2_ref, acc_ref, rows):
    h = h_ref[0:rows, :]
    a = jnp.dot(h, w1_ref[0].astype(BF16), preferred_element_type=F32)
    b = jnp.dot(h, w3_ref[0].astype(BF16), preferred_element_type=F32)
    acc_ref[0:rows, :] += jnp.dot((_silu(a) * b).astype(BF16), w2_ref[0].astype(BF16),
                                  preferred_element_type=F32)


def _ffn_kernel(h_ref, w1_ref, w3_ref, w2_ref, x_ref, g2_ref, o_ref):
    j = pl.program_id(1)

    @pl.when(j == 0)
    def _():
        o_ref[...] = jnp.zeros_like(o_ref)

    _swiglu_accumulate(h_ref, w1_ref, w3_ref, w2_ref, o_ref, h_ref.shape[0])

    @pl.when(j == pl.num_programs(1) - 1)
    def _():
        o_ref[...] = x_ref[...] + g2_ref[0] * o_ref[...]


def _ffn(h2, w1, w3, w2, layer, x2, g2, seq):
    M, D = x2.shape
    F = w1.shape[2]
    tm, tf = 1024, 512
    once = dict(pipeline_mode=pl.Buffered(1))
    mod = pl.BlockSpec((1, 1, D), lambda i, j: (i * tm // seq, 0, 0))
    return pl.pallas_call(
        _ffn_kernel, name="ffn",
        out_shape=jax.ShapeDtypeStruct((M, D), F32),
        grid=(M // tm, F // tf),
        in_specs=[
            pl.BlockSpec((tm, D), lambda i, j: (i, 0), **once),
            pl.BlockSpec((1, D, tf), lambda i, j: (layer, 0, j)),
            pl.BlockSpec((1, D, tf), lambda i, j: (layer, 0, j)),
            pl.BlockSpec((1, tf, D), lambda i, j: (layer, j, 0)),
            pl.BlockSpec((tm, D), lambda i, j: (i, 0), **once),
            mod,
        ],
        out_specs=pl.BlockSpec((tm, D), lambda i, j: (i, 0), **once),
        compiler_params=_cparams(("parallel", "arbitrary")),
    )(h2, w1, w3, w2, x2, g2)


def _router_kernel(h_ref, w_ref, b_ref, info_ref, cnt_ref, carry, *, tm):
    @pl.when(pl.program_id(0) == 0)
    def _():
        carry[...] = jnp.zeros_like(carry)

    logits = jnp.dot(h_ref[...].astype(BF16), w_ref[...], preferred_element_type=F32) + b_ref[...]
    lane = lax.broadcasted_iota(jnp.int32, logits.shape, 1)
    neg = -1e30
    logits = jnp.where(lane < N_EXPERTS, logits, neg)
    m1 = jnp.max(logits, axis=-1, keepdims=True)
    i1 = jnp.min(jnp.where(logits == m1, lane, LANE), axis=-1, keepdims=True)
    rest = jnp.where(lane == i1, neg, logits)
    m2 = jnp.max(rest, axis=-1, keepdims=True)
    i2 = jnp.min(jnp.where(rest == m2, lane, LANE), axis=-1, keepdims=True)
    e2 = jnp.exp(m2 - m1)
    p1 = 1.0 / (1.0 + e2)
    p2 = e2 / (1.0 + e2)

    oh1 = (lane == i1).astype(F32)
    oh2 = (lane == i2).astype(F32)
    both = oh1 + oh2
    ti = lax.broadcasted_iota(jnp.int32, (tm, tm), 0)
    si = lax.broadcasted_iota(jnp.int32, (tm, tm), 1)
    strict = (ti > si).astype(BF16)
    before = jnp.dot(strict, both.astype(BF16), preferred_element_type=F32) + carry[...]
    rank1 = jnp.sum(oh1 * before, axis=-1, keepdims=True)
    rank2 = jnp.sum(oh2 * before, axis=-1, keepdims=True)
    total = carry[...] + jnp.sum(both, axis=0, keepdims=True)
    carry[...] = total
    cnt_ref[...] = total

    info = jnp.where(lane == 0, i1.astype(F32), 0.0)
    info = jnp.where(lane == 1, i2.astype(F32), info)
    info = jnp.where(lane == 2, rank1, info)
    info = jnp.where(lane == 3, rank2, info)
    info = jnp.where(lane == 4, p1, info)
    info = jnp.where(lane == 5, p2, info)
    info_ref[...] = info


def _router(h2, rw, rb):
    M, D = h2.shape
    tm = 512
    return pl.pallas_call(
        functools.partial(_router_kernel, tm=tm), name="router",
        out_shape=[jax.ShapeDtypeStruct((M, LANE), F32), jax.ShapeDtypeStruct((1, LANE), F32)],
        grid=(M // tm,),
        in_specs=[
            pl.BlockSpec((tm, D), lambda i: (i, 0)),
            pl.BlockSpec((D, LANE), lambda i: (0, 0)),
            pl.BlockSpec((1, LANE), lambda i: (0, 0)),
        ],
        out_specs=[pl.BlockSpec((tm, LANE), lambda i: (i, 0)), pl.BlockSpec((1, LANE), lambda i: (0, 0))],
        scratch_shapes=[pltpu.VMEM((1, LANE), F32)],
        compiler_params=_cparams(("arbitrary",)),
    )(h2, rw, rb)


MOE_TILE = 1024


def _moe_dispatch_kernel(s1_ref, s2_ref, h_ref, xs_in_ref, xs_ref, sem, *, tm):
    del xs_in_ref
    base = pl.program_id(0) * tm

    def copies(r):
        src = h_ref.at[pl.ds(r, 1)]
        return (pltpu.make_async_copy(src, xs_ref.at[pl.ds(s1_ref[base + r], 1)], sem.at[0]),
                pltpu.make_async_copy(src, xs_ref.at[pl.ds(s2_ref[base + r], 1)], sem.at[1]))

    def start(r, carry):
        for cp in copies(r):
            cp.start()
        return carry

    def wait(r, carry):
        for cp in copies(r):
            cp.wait()
        return carry

    lax.fori_loop(0, tm, start, 0, unroll=8)
    lax.fori_loop(0, tm, wait, 0, unroll=8)


def _moe_dispatch(slot1, slot2, h2, n_rows):
    M, D = h2.shape
    tm = 256
    xs0 = jnp.zeros((n_rows, D), F32)
    return pl.pallas_call(
        functools.partial(_moe_dispatch_kernel, tm=tm), name="moe_dispatch",
        out_shape=jax.ShapeDtypeStruct((n_rows, D), F32),
        grid_spec=pltpu.PrefetchScalarGridSpec(
            num_scalar_prefetch=2,
            grid=(M // tm,),
            in_specs=[pl.BlockSpec((tm, D), lambda i, s1, s2: (i, 0)),
                      pl.BlockSpec(memory_space=pl.ANY)],
            out_specs=pl.BlockSpec(memory_space=pl.ANY),
            scratch_shapes=[pltpu.SemaphoreType.DMA((2,))],
        ),
        input_output_aliases={3: 0},
        compiler_params=_cparams(("arbitrary",)),
    )(slot1, slot2, h2, xs0)


def _moe_expert_kernel(te_ref, nv_ref, xs_ref, w1_ref, w3_ref, w2_ref, ys_ref, xb):
    j = pl.program_id(1)
    nv = nv_ref[pl.program_id(0)]
    half = xb.shape[0] // 2

    @pl.when(j == 0)
    def _():
        ys_ref[...] = jnp.zeros_like(ys_ref)
        xb[...] = xs_ref[...].astype(BF16)

    @pl.when(nv > half)
    def _():
        _swiglu_accumulate(xb, w1_ref, w3_ref, w2_ref, ys_ref, 2 * half)

    @pl.when((nv > 0) & (nv <= half))
    def _():
        _swiglu_accumulate(xb, w1_ref, w3_ref, w2_ref, ys_ref, half)


def _moe_experts(tile_expert, n_valid, xs, w1, w3, w2):
    P, D = xs.shape
    E, _, F = w1.shape
    tm, tf = MOE_TILE, 256
    nf = F // tf
    fj = lambda i, j, nv: jnp.where(nv[i] > 0, j, nf - 1)
    return pl.pallas_call(
        _moe_expert_kernel, name="moe_experts",
        out_shape=jax.ShapeDtypeStruct((P, D), F32),
        grid_spec=pltpu.PrefetchScalarGridSpec(
            num_scalar_prefetch=2,
            grid=(P // tm, nf),
            in_specs=[
                pl.BlockSpec((tm, D), lambda i, j, te, nv: (i, 0)),
                pl.BlockSpec((1, D, tf), lambda i, j, te, nv: (te[i], 0, fj(i, j, nv))),
                pl.BlockSpec((1, D, tf), lambda i, j, te, nv: (te[i], 0, fj(i, j, nv))),
                pl.BlockSpec((1, tf, D), lambda i, j, te, nv: (te[i], fj(i, j, nv), 0)),
            ],
            out_specs=pl.BlockSpec((tm, D), lambda i, j, te, nv: (i, 0)),
            scratch_shapes=[pltpu.VMEM((tm, D), BF16)],
        ),
        compiler_params=_cparams(("arbitrary", "arbitrary")),
    )(tile_expert, n_valid, xs, w1, w3, w2)


def _moe_combine_kernel(s1_ref, s2_ref, ys_ref, info_ref, x_ref, g2_ref, o_ref, a_buf, b_buf, sem, *, tm):
    base = pl.program_id(0) * tm

    def copies(r):
        return (pltpu.make_async_copy(ys_ref.at[pl.ds(s1_ref[base + r], 1)], a_buf.at[pl.ds(r, 1)], sem.at[0]),
                pltpu.make_async_copy(ys_ref.at[pl.ds(s2_ref[base + r], 1)], b_buf.at[pl.ds(r, 1)], sem.at[1]))

    def start(r, carry):
        for cp in copies(r):
            cp.start()
        return carry

    def wait(r, carry):
        for cp in copies(r):
            cp.wait()
        return carry

    lax.fori_loop(0, tm, start, 0, unroll=8)
    lax.fori_loop(0, tm, wait, 0, unroll=8)
    info = info_ref[...]
    p1 = info[:, 4:5]
    p2 = info[:, 5:6]
    o_ref[...] = x_ref[...] + g2_ref[0] * (p1 * a_buf[...] + p2 * b_buf[...])


def _moe_combine(slot1, slot2, ys, info, x2, g2, seq):
    M, D = x2.shape
    tm = 256
    return pl.pallas_call(
        functools.partial(_moe_combine_kernel, tm=tm), name="moe_combine",
        out_shape=jax.ShapeDtypeStruct((M, D), F32),
        grid_spec=pltpu.PrefetchScalarGridSpec(
            num_scalar_prefetch=2,
            grid=(M // tm,),
            in_specs=[
                pl.BlockSpec(memory_space=pl.ANY),
                pl.BlockSpec((tm, LANE), lambda i, s1, s2: (i, 0)),
                pl.BlockSpec((tm, D), lambda i, s1, s2: (i, 0)),
                pl.BlockSpec((1, 1, D), lambda i, s1, s2: (i * tm // seq, 0, 0)),
            ],
            out_specs=pl.BlockSpec((tm, D), lambda i, s1, s2: (i, 0)),
            scratch_shapes=[pltpu.VMEM((tm, D), F32), pltpu.VMEM((tm, D), F32),
                            pltpu.SemaphoreType.DMA((2,))],
        ),
        compiler_params=_cparams(("arbitrary",)),
    )(slot1, slot2, ys, info, x2, g2)


def _moe(h2, rw, rb, w1, w3, w2, x2, g2, seq):
    M, D = x2.shape
    E = w1.shape[0]
    T = MOE_TILE
    n_rows = 2 * M + E * T
    n_tiles = n_rows // T
    info, counts = _router(h2, rw, rb)

    e1, e2, rank1, rank2 = (info[:, c].astype(jnp.int32) for c in range(4))
    cnt = counts[0, :E].astype(jnp.int32)
    padded = (cnt + T - 1) // T * T
    ends = jnp.cumsum(padded)
    off = ends - padded
    expert_ids = jnp.arange(E, dtype=jnp.int32)
    offset_of = lambda e: jnp.sum(jnp.where(e[:, None] == expert_ids[None, :], off[None, :], 0), axis=1)
    slot1 = offset_of(e1) + rank1
    slot2 = offset_of(e2) + rank2
    tile_start = jnp.arange(n_tiles, dtype=jnp.int32) * T
    in_use = tile_start < ends[E - 1]
    clamped = jnp.minimum(tile_start, ends[E - 1] - T)
    tile_expert = jnp.sum(clamped[:, None] >= ends[None, :], axis=1).astype(jnp.int32)
    is_e = tile_expert[:, None] == expert_ids[None, :]
    tokens_end = jnp.sum(jnp.where(is_e, (off + cnt)[None, :], 0), axis=1)
    n_valid = jnp.where(in_use, jnp.clip(tokens_end - tile_start, 0, T), 0).astype(jnp.int32)

    xs = _moe_dispatch(slot1, slot2, h2, n_rows)
    ys = _moe_experts(tile_expert, n_valid, xs, w1, w3, w2)
    return _moe_combine(slot1, slot2, ys, info, x2, g2, seq)


def _mixing_layer(x2, mods, p, w_t, layer, batch, seq, h_dtype):
    row = lambda t: t.reshape(1, -1)
    hz = _inproj(x2, row(p["norm1_g"]), mods["sc1"], mods["sh1"], w_t, layer, seq)

    mu = p["shift_mu"]
    mu3 = mu[:3 * RW_W].reshape(3, RW_W)
    mul = row(mu[3 * RW_W:])
    r, lw, k, v, kk, a, g = _rwkv_prep(
        hz, mu3, mul, row(p["rw_w0"]), p["rw_w2"].astype(BF16), row(p["rw_a0"]),
        p["rw_a2"].astype(BF16), p["rw_g2"].astype(BF16), row(p["rw_k_k"]), row(p["rw_k_a"]), seq)
    hm = lambda t: t.reshape(RW_H, 1, RW_HD)
    o_a = _rwkv_scan(r, lw, k, v, kk, a, g, hm(p["rw_r_k"]), hm(p["rw_lnx_g"]), hm(p["rw_lnx_b"]),
                     batch, seq)

    fb128 = jnp.zeros((1, LANE), F32).at[0, :FOX_H].set(p["fox_fb"])
    kx, qxT, vT = _fox_prep(hz, row(p["fox_qn_g"]), row(p["fox_kn_g"]), fb128, seq)
    o_b = _fox_attn(kx, qxT, vT, batch, seq)

    o_c = _sgu(hz, p["sg_ln_g"], p["sg_ln_b"], p["sg_ws"], jnp.transpose(p["sg_b"]))

    return _outproj(o_a, o_b, o_c, p["w_out"].astype(BF16), x2, mods["g1"], row(p["norm2_g"]),
                    mods["sc2"], mods["sh2"], seq, h_dtype)


def kernel(x, c, ada_w, ada_b, norm1_g, norm2_g, w_in, shift_mu, rw_w0, rw_w2, rw_a0, rw_a2, rw_g2, rw_k_k, rw_k_a, rw_r_k, rw_lnx_g, rw_lnx_b, fox_qn_g, fox_kn_g, fox_fb, sg_ln_g, sg_ln_b, sg_ws, sg_b, w_out, ffn_w1, ffn_w3, ffn_w2, moe_router_w, moe_router_b, moe_w1, moe_w3, moe_w2):
    B, S, D = x.shape
    L = ada_w.shape[0]
    x2 = x.reshape(B * S, D)
    c8 = jnp.zeros((8, D), F32).at[:B].set(c)
    mod = _ada_mod(c8, ada_w, ada_b.reshape(L, 1, N_MOD * D))

    layer_params = dict(
        norm1_g=norm1_g, norm2_g=norm2_g, shift_mu=shift_mu, rw_w0=rw_w0, rw_w2=rw_w2,
        rw_a0=rw_a0, rw_a2=rw_a2, rw_g2=rw_g2, rw_k_k=rw_k_k, rw_k_a=rw_k_a, rw_r_k=rw_r_k,
        rw_lnx_g=rw_lnx_g, rw_lnx_b=rw_lnx_b, fox_qn_g=fox_qn_g, fox_kn_g=fox_kn_g, fox_fb=fox_fb,
        sg_ln_g=sg_ln_g, sg_ln_b=sg_ln_b, sg_ws=sg_ws, sg_b=sg_b, w_out=w_out)

    w_t = jnp.transpose(w_in, (2, 0, 1))
    for l in range(L):
        names = ("sh1", "sc1", "g1", "sh2", "sc2", "g2")
        mods = {n: mod[l, :B, i * D:(i + 1) * D].reshape(B, 1, D) for i, n in enumerate(names)}
        p = {n: t[l] for n, t in layer_params.items()}
        dense = l % 2 == 0
        x2, h2 = _mixing_layer(x2, mods, p, w_t, l, B, S, BF16 if dense else F32)
        j = l // 2
        if dense:
            x2 = _ffn(h2, ffn_w1, ffn_w3, ffn_w2, j, x2, mods["g2"], S)
        else:
            rw = jnp.zeros((D, LANE), BF16).at[:, :N_EXPERTS].set(moe_router_w[j].astype(BF16))
            rb = jnp.zeros((1, LANE), F32).at[0, :N_EXPERTS].set(moe_router_b[j])
            x2 = _moe(h2, rw, rb, moe_w1[j], moe_w3[j], moe_w2[j], x2, mods["g2"], S)
    return x2.reshape(B, S, D)
```

```python
import functools

import jax
import jax.numpy as jnp
from jax import lax
from jax.experimental import pallas as pl
from jax.experimental.pallas import tpu as pltpu

F32 = jnp.float32
BF16 = jnp.bfloat16

D_MODEL = 2048
DEPTH = 2
RW_HD = 64
RW_W = 768
RW_H = 12
DECAY_LORA = 64
AAA_LORA = 64
GATE_LORA = 128
FOX_HD = 128
FOX_W = 768
FOX_H = 6
SG_GD = 128
SG_W = 512
SG_G = 4
SG_CHUNK = 128
N_EXPERTS = 8
N_MOD = 6
RMS_EPS = 1e-6
LN_EPS = 1e-5
LNX_EPS = 64e-5

C_R = 0
C_RWKV_END = 3 * RW_W + DECAY_LORA + AAA_LORA + GATE_LORA
C_Q = C_RWKV_END
C_FF = C_Q + 3 * FOX_W
C_U = C_FF + FOX_H
C_SV = C_U + SG_W
N_IN = C_SV + SG_W

P_R, P_K, P_V = 0, 768, 1536
P_LORA = 2304
P_Q, P_FK, P_FV = 2560, 3328, 4096
P_FF = C_FF
P_U = 5120
P_SV = P_U + SG_W
N_PACK = P_SV + SG_W
COL_TILE = 256

LANE = 128
VMEM_LIMIT = 56 * 1024 * 1024


def _cparams(sem):
    return pltpu.CompilerParams(dimension_semantics=sem, vmem_limit_bytes=VMEM_LIMIT)


def _sigmoid(x):
    return 1.0 / (1.0 + jnp.exp(-x))


def _softplus(x):
    return jnp.maximum(x, 0.0) + jnp.log(1.0 + jnp.exp(-jnp.abs(x)))


def _gelu_tanh(x):
    return 0.5 * x * (1.0 + jnp.tanh(0.7978845608028654 * (x + 0.044715 * (x * x * x))))


def _silu(x):
    return x * _sigmoid(x)


def _ada_kernel(c_ref, w_ref, b_ref, o_ref):
    cs = _silu(c_ref[...]).astype(BF16)
    w = w_ref[0].astype(BF16)
    o_ref[0] = jnp.dot(cs, w, preferred_element_type=F32) + b_ref[0]


def _ada_mod(c8, ada_w, ada_b3):
    L, D, N = ada_w.shape
    tn = 1536
    return pl.pallas_call(
        _ada_kernel, name="ada_mod",
        out_shape=jax.ShapeDtypeStruct((L, 8, N), F32),
        grid=(L, N // tn),
        in_specs=[
            pl.BlockSpec((8, D), lambda l, j: (0, 0)),
            pl.BlockSpec((1, D, tn), lambda l, j: (l, 0, j)),
            pl.BlockSpec((1, 1, tn), lambda l, j: (l, 0, j)),
        ],
        out_specs=pl.BlockSpec((1, 8, tn), lambda l, j: (l, 0, j)),
        compiler_params=_cparams(("parallel", "parallel")),
    )(c8, ada_w, ada_b3)


def _col_spec(tm, width, start):
    assert start % width == 0
    return pl.BlockSpec((tm, width), lambda i, *_: (i, start // width))


def _inproj_kernel(x_ref, g_ref, sc_ref, sh_ref, w_hbm, o_ref, h_scr, wbuf, sem, *, layer, tn):
    i, j = pl.program_id(0), pl.program_id(1)
    ni, nj = pl.num_programs(0), pl.num_programs(1)
    step = i * nj + j
    slot = step % 2

    def weight_copy(jj, s):
        first_row = jnp.where(jj < nj - 2, jj * tn, N_IN - (nj - jj) * tn)
        return pltpu.make_async_copy(w_hbm.at[pl.ds(first_row, tn), layer], wbuf.at[s], sem.at[s])

    @pl.when(step == 0)
    def _():
        weight_copy(j, slot).start()

    @pl.when(step + 1 < ni * nj)
    def _():
        weight_copy(jnp.where(j + 1 < nj, j + 1, 0), 1 - slot).start()

    @pl.when(j == 0)
    def _():
        chunk = 512
        for c in range(x_ref.shape[0] // chunk):
            rows = pl.ds(c * chunk, chunk)
            x = x_ref[rows, :]
            ms = jnp.mean(x * x, axis=-1, keepdims=True)
            y = x * lax.rsqrt(ms + RMS_EPS) * g_ref[...]
            h_scr[rows, :] = (y * (1.0 + sc_ref[0]) + sh_ref[0]).astype(BF16)

    weight_copy(j, slot).wait()
    o_ref[...] = lax.dot_general(h_scr[...], wbuf[slot].astype(BF16), (((1,), (1,)), ((), ())),
                                 preferred_element_type=F32)


def _inproj(x2, g, sc, sh, w_t, layer, seq):
    M, D = x2.shape
    tm, tn = min(2048, seq), SG_W
    nj = N_PACK // tn
    assert (nj - 2) * tn == P_U and P_FF + FOX_H <= P_U
    return pl.pallas_call(
        functools.partial(_inproj_kernel, layer=layer, tn=tn), name="inproj",
        out_shape=jax.ShapeDtypeStruct((M, N_PACK), F32),
        grid=(M // tm, nj),
        in_specs=[
            pl.BlockSpec((tm, D), lambda i, j: (i, 0), pipeline_mode=pl.Buffered(1)),
            pl.BlockSpec((1, D), lambda i, j: (0, 0)),
            pl.BlockSpec((1, 1, D), lambda i, j: (i * tm // seq, 0, 0)),
            pl.BlockSpec((1, 1, D), lambda i, j: (i * tm // seq, 0, 0)),
            pl.BlockSpec(memory_space=pl.ANY),
        ],
        out_specs=pl.BlockSpec((tm, tn), lambda i, j: (i, j)),
        scratch_shapes=[pltpu.VMEM((tm, D), BF16), pltpu.VMEM((2, tn, D), F32),
                        pltpu.SemaphoreType.DMA((2,))],
        compiler_params=_cparams(("arbitrary", "arbitrary")),
    )(x2, g, sc, sh, w_t)


def _rwkv_token_quantities(first, zr_ref, zk_ref, zv_ref, zl_ref, pr_ref, pk_ref, pv_ref, pl_ref,
                           mu_ref, mul_ref, w0_ref, w2_ref, a0_ref, a2_ref, g2_ref, kk_ref, ka_ref,
                           r_out, lw_out, k_out, v_out, kk_out, a_out, g_out):
    def shift(z_ref, p_ref, mu):
        z = z_ref[...]
        prev_last = jnp.where(first, 0.0, p_ref[7:8, :])
        zp = pltpu.roll(z, 1, 0)
        row = lax.broadcasted_iota(jnp.int32, z.shape, 0)
        zp = jnp.where(row == 0, prev_last, zp)
        return z + (zp - z) * mu

    r = shift(zr_ref, pr_ref, mu_ref[0:1, :])
    k = shift(zk_ref, pk_ref, mu_ref[1:2, :])
    v = shift(zv_ref, pv_ref, mu_ref[2:3, :])
    lo = shift(zl_ref, pl_ref, mul_ref[...])
    wd = lo[:, 0:DECAY_LORA]
    ad = lo[:, DECAY_LORA:DECAY_LORA + AAA_LORA]
    gd = lo[:, DECAY_LORA + AAA_LORA:]

    dec = w0_ref[...] + jnp.dot(jnp.tanh(wd).astype(BF16), w2_ref[...], preferred_element_type=F32)
    w_log = -_softplus(-dec) - 0.5
    lw = -jnp.exp(w_log)
    a = _sigmoid(a0_ref[...] + jnp.dot(ad.astype(BF16), a2_ref[...], preferred_element_type=F32))
    g = jnp.dot(_sigmoid(gd).astype(BF16), g2_ref[...], preferred_element_type=F32)
    kk = k * kk_ref[...]
    km = k * (1.0 + (a - 1.0) * ka_ref[...])

    for h in range(RW_H):
        sl = slice(h * RW_HD, (h + 1) * RW_HD)
        r_out[h] = r[:, sl]
        lw_out[h] = lw[:, sl]
        k_out[h] = km[:, sl]
        v_out[h] = v[:, sl]
        kk_out[h] = kk[:, sl]
        a_out[h] = a[:, sl]
        g_out[h] = g[:, sl]


def _bmm(a, b, ca, cb):
    return lax.dot_general(a.astype(BF16), b.astype(BF16), (((ca,), (cb,)), ((0,), (0,))),
                           preferred_element_type=F32)


def _split3(x):
    hi = x.astype(BF16)
    r1 = x - hi.astype(F32)
    mid = r1.astype(BF16)
    lo = (r1 - mid.astype(F32)).astype(BF16)
    return hi, mid, lo


def _rwkv_scan_kernel(r_ref, lw_ref, k_ref, v_ref, kk_ref, a_ref, g_ref, rk_ref, lng_ref, lnb_ref,
                      o_ref, s_scr, *, L, nC):
    H, K = RW_H, RW_HD
    n = H * nC

    @pl.when(pl.program_id(1) == 0)
    def _():
        s_scr[...] = jnp.zeros_like(s_scr)

    ld = lambda ref: ref[...].reshape(n, L, K)
    r, lw, k, v, kk, a = ld(r_ref), ld(lw_ref), ld(k_ref), ld(v_ref), ld(kk_ref), ld(a_ref)

    kk = kk / jnp.maximum(jnp.sqrt(jnp.sum(kk * kk, axis=-1, keepdims=True)), 1e-12)
    b = kk * a

    ti = lax.broadcasted_iota(jnp.int32, (L, L), 0)
    si = lax.broadcasted_iota(jnp.int32, (L, L), 1)
    tri_incl = jnp.broadcast_to((ti >= si).astype(BF16), (n, L, L))
    cw = sum(_bmm(tri_incl, piece, 2, 1) for piece in reversed(_split3(lw)))
    cw_last = cw[:, L - 1:L, :]
    e_in = jnp.exp(cw)
    e_out = jnp.exp(-cw)
    e_ex = jnp.exp(cw - lw)
    e_end = jnp.exp(cw_last - cw)
    w_end = jnp.exp(cw_last)

    at = -kk * e_ex
    rt = r * e_in
    bb = b * e_out
    kb = k * e_out
    bh = b * e_end
    kh = k * e_end

    atrt = jnp.concatenate([at, rt], axis=1)
    t2 = lax.broadcasted_iota(jnp.int32, (2 * L, L), 0)
    s2 = lax.broadcasted_iota(jnp.int32, (2 * L, L), 1)
    causal = (((t2 & (L - 1)) - s2 + jnp.where(t2 < L, 0, 1)) > 0)[None]
    pb = jnp.where(causal, _bmm(atrt, bb, 2, 2), 0.0)
    pk = jnp.where(causal, _bmm(atrt, kb, 2, 2), 0.0)
    a_ab, b_rb = pb[:, :L], pb[:, L:]
    a_ak, b_rk = pk[:, :L], pk[:, L:]

    x = jnp.broadcast_to((ti == si).astype(F32), (n, L, L))
    size = 1
    while size < L:
        sh = size.bit_length() - 1
        m = (((ti >> (sh + 1)) == (si >> (sh + 1))) & (((ti >> sh) & 1) == 1) & (((si >> sh) & 1) == 0))[None]
        x = x + _bmm(_bmm(x, jnp.where(m, a_ab, 0.0), 2, 1), x, 2, 1)
        size *= 2

    xu = _bmm(x, jnp.concatenate([at, _bmm(a_ak, v, 2, 1)], axis=-1), 2, 1)
    w = _bmm(b_rb, xu, 2, 1)
    rp = rt + w[..., :K]
    y0 = w[..., K:] + _bmm(b_rk, v, 2, 1)
    big = _bmm(xu, bh, 1, 1)
    gp = big[:, :K]
    cc = big[:, K:] + _bmm(v, kh, 1, 1)

    c4 = lambda t: t.reshape((H, nC) + t.shape[1:])
    rp, y0, gp, cc, w_end = c4(rp), c4(y0), c4(gp), c4(cc), c4(w_end)
    s = s_scr[...]
    ys = []
    for c in range(nC):
        ys.append(_bmm(rp[:, c], s, 2, 2) + y0[:, c])
        s = s * w_end[:, c] + _bmm(s, gp[:, c], 2, 1) + cc[:, c]
    s_scr[...] = s
    y = ys[0] if nC == 1 else jnp.concatenate(ys, axis=1)

    mu = jnp.mean(y, axis=-1, keepdims=True)
    yc = y - mu
    var = jnp.mean(yc * yc, axis=-1, keepdims=True)
    yn = yc * lax.rsqrt(var + LNX_EPS) * lng_ref[...] + lnb_ref[...]
    r3, k3, v3 = r_ref[...], k_ref[...], v_ref[...]
    bonus = jnp.sum(r3 * k3 * rk_ref[...], axis=-1, keepdims=True) * v3
    out = (yn + bonus) * g_ref[...]
    for p in range(H // 2):
        pair = jnp.concatenate([out[2 * p], out[2 * p + 1]], axis=-1)
        o_ref[:, p * LANE:(p + 1) * LANE] = pair.astype(o_ref.dtype)


N_TOKEN_INPUTS = 17


def _rwkv_kernel(*refs, L, nC):
    token_in = refs[:N_TOKEN_INPUTS]
    rk_ref, lng_ref, lnb_ref, o_ref, s_scr = refs[N_TOKEN_INPUTS:N_TOKEN_INPUTS + 5]
    head_major = refs[N_TOKEN_INPUTS + 5:]
    _rwkv_token_quantities(pl.program_id(1) == 0, *token_in, *head_major)
    _rwkv_scan_kernel(*head_major, rk_ref, lng_ref, lnb_ref, o_ref, s_scr, L=L, nC=nC)


def _rwkv(hz, mu3, mul, w0, w2, a0, a2, g2, k_k, k_a, r_k, lnx_g, lnx_b, batch, seq):
    M = hz.shape[0]
    L, nC = 64, 4
    tb = L * nC
    nj = seq // tb
    blk = lambda b, j: b * nj + j
    prev = lambda b, j: jnp.maximum(blk(b, j) * (tb // 8) - 1, 0)
    row = lambda n: pl.BlockSpec((1, n), lambda b, j: (0, 0))
    full = lambda r, c: pl.BlockSpec((r, c), lambda b, j: (0, 0))
    par = pl.BlockSpec((RW_H, 1, RW_HD), lambda b, j: (0, 0, 0))
    in_specs = [
        pl.BlockSpec((tb, RW_W), lambda b, j: (blk(b, j), P_R // RW_W)),
        pl.BlockSpec((tb, RW_W), lambda b, j: (blk(b, j), P_K // RW_W)),
        pl.BlockSpec((tb, RW_W), lambda b, j: (blk(b, j), P_V // RW_W)),
        pl.BlockSpec((tb, COL_TILE), lambda b, j: (blk(b, j), P_LORA // COL_TILE)),
        pl.BlockSpec((8, RW_W), lambda b, j: (prev(b, j), P_R // RW_W)),
        pl.BlockSpec((8, RW_W), lambda b, j: (prev(b, j), P_K // RW_W)),
        pl.BlockSpec((8, RW_W), lambda b, j: (prev(b, j), P_V // RW_W)),
        pl.BlockSpec((8, COL_TILE), lambda b, j: (prev(b, j), P_LORA // COL_TILE)),
        full(3, RW_W), row(COL_TILE), row(RW_W), full(DECAY_LORA, RW_W), row(RW_W),
        full(AAA_LORA, RW_W), full(GATE_LORA, RW_W), row(RW_W), row(RW_W),
        par, par, par,
    ]
    assert len(in_specs) == N_TOKEN_INPUTS + 3
    head_major = pltpu.VMEM((RW_H, tb, RW_HD), F32)
    return pl.pallas_call(
        functools.partial(_rwkv_kernel, L=L, nC=nC), name="rwkv",
        out_shape=jax.ShapeDtypeStruct((M, RW_W), BF16),
        grid=(batch, nj),
        in_specs=in_specs,
        out_specs=pl.BlockSpec((tb, RW_W), lambda b, j: (blk(b, j), 0)),
        scratch_shapes=[pltpu.VMEM((RW_H, RW_HD, RW_HD), F32)] + [head_major] * 7,
        compiler_params=_cparams(("parallel", "arbitrary")),
    )(*([hz] * 8), mu3, mul, w0, w2, a0, a2, g2, k_k, k_a, r_k, lnx_g, lnx_b)


FOX_XD = 2 * FOX_HD


def _fox_prep_kernel(*refs, tm, seq):
    q_refs, k_refs, v_refs = refs[0:3], refs[3:6], refs[6:9]
    f_ref, qg_ref, kg_ref, fb_ref, kx_ref, qxT_ref, vT_ref, carry = refs[9:]

    @pl.when((pl.program_id(0) * tm) % seq == 0)
    def _():
        carry[...] = jnp.zeros_like(carry)

    logf = -_softplus(-(f_ref[...] + fb_ref[...]))
    ti = lax.broadcasted_iota(jnp.int32, (tm, tm), 0)
    si = lax.broadcasted_iota(jnp.int32, (tm, tm), 1)
    tri = (ti >= si).astype(BF16)
    cum = carry[...]
    for piece in reversed(_split3(logf)):
        cum = cum + jnp.dot(tri, piece, preferred_element_type=F32)
    carry[...] = cum[tm - 1:tm, :]
    cum_t = cum.T

    scale = FOX_HD ** -0.5
    lane = lax.broadcasted_iota(jnp.int32, (tm, FOX_HD), 1)
    sub = lax.broadcasted_iota(jnp.int32, (FOX_HD, tm), 0)
    pieces = lambda x: [t.astype(F32) for t in _split3(x)]
    for h in range(FOX_H):
        blk, half = h // 2, slice((h % 2) * FOX_HD, (h % 2 + 1) * FOX_HD)
        q = q_refs[blk][:, half]
        k = k_refs[blk][:, half]
        qn = q * lax.rsqrt(jnp.mean(q * q, axis=-1, keepdims=True) + RMS_EPS) * qg_ref[...]
        kn = k * lax.rsqrt(jnp.mean(k * k, axis=-1, keepdims=True) + RMS_EPS) * kg_ref[...]

        c_hi, c_mid, c_lo = pieces(cum[:, h:h + 1])
        k_extra = jnp.where(lane < 3, 1.0, 0.0)
        k_extra = jnp.where(lane == 3, -c_hi, k_extra)
        k_extra = jnp.where(lane == 4, -c_mid, k_extra)
        k_extra = jnp.where(lane == 5, -c_lo, k_extra)
        kx_ref[:, h * FOX_XD:h * FOX_XD + FOX_HD] = kn.astype(BF16)
        kx_ref[:, h * FOX_XD + FOX_HD:(h + 1) * FOX_XD] = k_extra.astype(BF16)

        r_hi, r_mid, r_lo = pieces(cum_t[h:h + 1, :])
        q_extra = jnp.where(sub < 6, 1.0, 0.0)
        q_extra = jnp.where(sub == 0, r_hi, q_extra)
        q_extra = jnp.where(sub == 1, r_mid, q_extra)
        q_extra = jnp.where(sub == 2, r_lo, q_extra)
        qxT_ref[h * FOX_XD:h * FOX_XD + FOX_HD, :] = (qn * scale).T.astype(BF16)
        qxT_ref[h * FOX_XD + FOX_HD:(h + 1) * FOX_XD, :] = q_extra.astype(BF16)

        vT_ref[h * FOX_HD:(h + 1) * FOX_HD, :] = v_refs[blk][:, half].T.astype(BF16)


def _fox_prep(hz, qn_g, kn_g, fb128, seq):
    M = hz.shape[0]
    tm = 256
    row = pl.BlockSpec((1, LANE), lambda i: (0, 0))
    pieces = lambda start: [_col_spec(tm, COL_TILE, start + c * COL_TILE) for c in range(FOX_W // COL_TILE)]
    return pl.pallas_call(
        functools.partial(_fox_prep_kernel, tm=tm, seq=seq), name="fox_prep",
        out_shape=[jax.ShapeDtypeStruct((M, FOX_H * FOX_XD), BF16),
                   jax.ShapeDtypeStruct((FOX_H * FOX_XD, M), BF16),
                   jax.ShapeDtypeStruct((FOX_W, M), BF16)],
        grid=(M // tm,),
        in_specs=pieces(P_Q) + pieces(P_FK) + pieces(P_FV) + [_col_spec(tm, LANE, P_FF), row, row, row],
        out_specs=[pl.BlockSpec((tm, FOX_H * FOX_XD), lambda i: (i, 0)),
                   pl.BlockSpec((FOX_H * FOX_XD, tm), lambda i: (0, i)),
                   pl.BlockSpec((FOX_W, tm), lambda i: (0, i))],
        scratch_shapes=[pltpu.VMEM((1, LANE), F32)],
        compiler_params=_cparams(("arbitrary",)),
    )(*([hz] * 10), qn_g, kn_g, fb128)


def _fox_attn_kernel(qt_ref, kt_ref, kx_ref, qxT_ref, vT_ref, o_ref, m_scr, l_scr, acc_scr, *, tq, tk):
    step_id = pl.program_id(1)
    qi, ki = qt_ref[step_id], kt_ref[step_id]
    ratio = tq // tk
    first_diag = ratio * qi
    last = ki == first_diag + ratio - 1

    @pl.when(ki == 0)
    def _():
        m_scr[...] = jnp.full_like(m_scr, -1e30)
        l_scr[...] = jnp.zeros_like(l_scr)
        acc_scr[...] = jnp.zeros_like(acc_scr)

    def step(diagonal):
        if diagonal:
            keep = (lax.broadcasted_iota(jnp.int32, (tk, tq), 1) - lax.broadcasted_iota(jnp.int32, (tk, tq), 0)
                    >= ki * tk - qi * tq)
        for h in range(FOX_H):
            xs = slice(h * FOX_XD, (h + 1) * FOX_XD)
            hs = slice(h * FOX_HD, (h + 1) * FOX_HD)
            s = jnp.dot(kx_ref[:, xs], qxT_ref[xs, :], preferred_element_type=F32)
            if diagonal:
                s = jnp.where(keep, s, -1e30)
            m_prev = m_scr[h]
            m_new = jnp.maximum(m_prev, jnp.max(s, axis=0, keepdims=True))
            alpha = jnp.exp(m_prev - m_new)
            p = jnp.exp(s - m_new)
            m_scr[h] = m_new
            l_scr[h] = alpha * l_scr[h] + jnp.sum(p, axis=0, keepdims=True)
            acc_scr[hs, :] = alpha * acc_scr[hs, :] + jnp.dot(vT_ref[hs, :], p.astype(BF16),
                                                              preferred_element_type=F32)

    pl.when(ki < first_diag)(lambda: step(False))
    pl.when(ki >= first_diag)(lambda: step(True))

    @pl.when(last)
    def _():
        for h in range(FOX_H):
            hs = slice(h * FOX_HD, (h + 1) * FOX_HD)
            o_ref[:, hs] = (acc_scr[hs, :] / l_scr[h]).T.astype(o_ref.dtype)


def _fox_attn(kx, qxT, vT, batch, seq):
    M = kx.shape[0]
    tq, tk = min(1024, seq), 512
    nq, nk, ratio = seq // tq, seq // tk, tq // tk
    pairs = [(q, k) for q in range(nq) for k in range(ratio * (q + 1))]
    q_tab = jnp.asarray([q for q, _ in pairs], jnp.int32)
    k_tab = jnp.asarray([k for _, k in pairs], jnp.int32)
    return pl.pallas_call(
        functools.partial(_fox_attn_kernel, tq=tq, tk=tk), name="fox_attn",
        out_shape=jax.ShapeDtypeStruct((M, FOX_W), BF16),
        grid_spec=pltpu.PrefetchScalarGridSpec(
            num_scalar_prefetch=2,
            grid=(batch, len(pairs)),
            in_specs=[
                pl.BlockSpec((tk, FOX_H * FOX_XD), lambda b, s, qt, kt: (b * nk + kt[s], 0)),
                pl.BlockSpec((FOX_H * FOX_XD, tq), lambda b, s, qt, kt: (0, b * nq + qt[s])),
                pl.BlockSpec((FOX_W, tk), lambda b, s, qt, kt: (0, b * nk + kt[s])),
            ],
            out_specs=pl.BlockSpec((tq, FOX_W), lambda b, s, qt, kt: (b * nq + qt[s], 0)),
            scratch_shapes=[pltpu.VMEM((FOX_H, 1, tq), F32), pltpu.VMEM((FOX_H, 1, tq), F32),
                            pltpu.VMEM((FOX_W, tq), F32)],
        ),
        compiler_params=_cparams(("parallel", "arbitrary")),
    )(q_tab, k_tab, kx, qxT, vT)


def _sgu_kernel(u_ref, v_ref, lg_ref, lb_ref, ws_ref, sb_ref, o_ref, *, tm):
    ti = lax.broadcasted_iota(jnp.int32, (SG_CHUNK, SG_CHUNK), 0)
    si = lax.broadcasted_iota(jnp.int32, (SG_CHUNK, SG_CHUNK), 1)
    for g in range(SG_G):
        sl = slice(g * SG_GD, (g + 1) * SG_GD)
        u = _gelu_tanh(u_ref[:, sl])
        v = _gelu_tanh(v_ref[:, sl])
        mu = jnp.mean(v, axis=-1, keepdims=True)
        vc = v - mu
        var = jnp.mean(vc * vc, axis=-1, keepdims=True)
        vn = (vc * lax.rsqrt(var + LN_EPS) * lg_ref[g:g + 1, :] + lb_ref[g:g + 1, :]).astype(BF16)
        ws = jnp.where(ti >= si, ws_ref[g], 0.0).astype(BF16)
        bias = sb_ref[:, g:g + 1]
        for c in range(tm // SG_CHUNK):
            rows = slice(c * SG_CHUNK, (c + 1) * SG_CHUNK)
            mixed = jnp.dot(ws, vn[rows], preferred_element_type=F32) + bias
            o_ref[rows, sl] = (u[rows] * mixed).astype(o_ref.dtype)


def _sgu(hz, ln_g, ln_b, ws, sb_t):
    M = hz.shape[0]
    tm = 512
    return pl.pallas_call(
        functools.partial(_sgu_kernel, tm=tm), name="sgu",
        out_shape=jax.ShapeDtypeStruct((M, SG_W), BF16),
        grid=(M // tm,),
        in_specs=[
            _col_spec(tm, SG_W, P_U), _col_spec(tm, SG_W, P_SV),
            pl.BlockSpec((SG_G, SG_GD), lambda i: (0, 0)),
            pl.BlockSpec((SG_G, SG_GD), lambda i: (0, 0)),
            pl.BlockSpec((SG_G, SG_CHUNK, SG_CHUNK), lambda i: (0, 0, 0)),
            pl.BlockSpec((SG_CHUNK, SG_G), lambda i: (0, 0)),
        ],
        out_specs=pl.BlockSpec((tm, SG_W), lambda i: (i, 0)),
        compiler_params=_cparams(("parallel",)),
    )(hz, hz, ln_g, ln_b, ws, sb_t)


def _outproj_kernel(oa_ref, ob_ref, oc_ref, w_ref, x_ref, g1_ref, ng_ref, sc_ref, sh_ref,
                    xo_ref, h_ref):
    mix = jnp.dot(oa_ref[...], w_ref[0:RW_W, :], preferred_element_type=F32)
    mix += jnp.dot(ob_ref[...], w_ref[RW_W:RW_W + FOX_W, :], preferred_element_type=F32)
    mix += jnp.dot(oc_ref[...], w_ref[RW_W + FOX_W:, :], preferred_element_type=F32)
    x = x_ref[...] + g1_ref[0] * mix
    xo_ref[...] = x
    ms = jnp.mean(x * x, axis=-1, keepdims=True)
    y = x * lax.rsqrt(ms + RMS_EPS) * ng_ref[...]
    h_ref[...] = (y * (1.0 + sc_ref[0]) + sh_ref[0]).astype(h_ref.dtype)


def _outproj(oa, ob, oc, w, x2, g1, ng, sc, sh, seq, h_dtype):
    M, D = x2.shape
    tm = 512
    bidx = lambda i: (i * tm // seq, 0, 0)
    mod = pl.BlockSpec((1, 1, D), bidx)
    return pl.pallas_call(
        _outproj_kernel, name="outproj",
        out_shape=[jax.ShapeDtypeStruct((M, D), F32), jax.ShapeDtypeStruct((M, D), h_dtype)],
        grid=(M // tm,),
        in_specs=[
            pl.BlockSpec((tm, RW_W), lambda i: (i, 0)),
            pl.BlockSpec((tm, FOX_W), lambda i: (i, 0)),
            pl.BlockSpec((tm, SG_W), lambda i: (i, 0)),
            pl.BlockSpec((D, D), lambda i: (0, 0)),
            pl.BlockSpec((tm, D), lambda i: (i, 0)),
            mod,
            pl.BlockSpec((1, D), lambda i: (0, 0)),
            mod, mod,
        ],
        out_specs=[pl.BlockSpec((tm, D), lambda i: (i, 0)), pl.BlockSpec((tm, D), lambda i: (i, 0))],
        compiler_params=_cparams(("parallel",)),
    )(oa, ob, oc, w, x2, g1, ng, sc, sh)


def _swiglu_accumulate(h_ref, w1_ref, w3_ref, w2_ref, acc_ref, rows):
    h = h_ref[0:rows, :]
    a = jnp.dot(h, w1_ref[0].astype(BF16), preferred_element_type=F32)
    b = jnp.dot(h, w3_ref[0].astype(BF16), preferred_element_type=F32)
    acc_ref[0:rows, :] += jnp.dot((_silu(a) * b).astype(BF16), w2_ref[0].astype(BF16),
                                  preferred_element_type=F32)


def _ffn_kernel(h_ref, w1_ref, w3_ref, w2_ref, x_ref, g2_ref, o_ref):
    j = pl.program_id(1)

    @pl.when(j == 0)
    def _():
        o_ref[...] = jnp.zeros_like(o_ref)

    _swiglu_accumulate(h_ref, w1_ref, w3_ref, w2_ref, o_ref, h_ref.shape[0])

    @pl.when(j == pl.num_programs(1) - 1)
    def _():
        o_ref[...] = x_ref[...] + g2_ref[0] * o_ref[...]


def _ffn(h2, w1, w3, w2, layer, x2, g2, seq):
    M, D = x2.shape
    F = w1.shape[2]
    tm, tf = 1024, 512
    once = dict(pipeline_mode=pl.Buffered(1))
    mod = pl.BlockSpec((1, 1, D), lambda i, j: (i * tm // seq, 0, 0))
    return pl.pallas_call(
        _ffn_kernel, name="ffn",
        out_shape=jax.ShapeDtypeStruct((M, D), F32),
        grid=(M // tm, F // tf),
        in_specs=[
            pl.BlockSpec((tm, D), lambda i, j: (i, 0), **once),
            pl.BlockSpec((1, D, tf), lambda i, j: (layer, 0, j)),
            pl.BlockSpec((1, D, tf), lambda i, j: (layer, 0, j)),
            pl.BlockSpec((1, tf, D), lambda i, j: (layer, j, 0)),
            pl.BlockSpec((tm, D), lambda i, j: (i, 0), **once),
            mod,
        ],
        out_specs=pl.BlockSpec((tm, D), lambda i, j: (i, 0), **once),
        compiler_params=_cparams(("parallel", "arbitrary")),
    )(h2, w1, w3, w2, x2, g2)


def _router_kernel(h_ref, w_ref, b_ref, info_ref, cnt_ref, carry, *, tm):
    @pl.when(pl.program_id(0) == 0)
    def _():
        carry[...] = jnp.zeros_like(carry)

    logits = jnp.dot(h_ref[...].astype(BF16), w_ref[...], preferred_element_type=F32) + b_ref[...]
    lane = lax.broadcasted_iota(jnp.int32, logits.shape, 1)
    neg = -1e30
    logits = jnp.where(lane < N_EXPERTS, logits, neg)
    m1 = jnp.max(logits, axis=-1, keepdims=True)
    i1 = jnp.min(jnp.where(logits == m1, lane, LANE), axis=-1, keepdims=True)
    rest = jnp.where(lane == i1, neg, logits)
    m2 = jnp.max(rest, axis=-1, keepdims=True)
    i2 = jnp.min(jnp.where(rest == m2, lane, LANE), axis=-1, keepdims=True)
    e2 = jnp.exp(m2 - m1)
    p1 = 1.0 / (1.0 + e2)
    p2 = e2 / (1.0 + e2)

    oh1 = (lane == i1).astype(F32)
    oh2 = (lane == i2).astype(F32)
    both = oh1 + oh2
    ti = lax.broadcasted_iota(jnp.int32, (tm, tm), 0)
    si = lax.broadcasted_iota(jnp.int32, (tm, tm), 1)
    strict = (ti > si).astype(BF16)
    before = jnp.dot(strict, both.astype(BF16), preferred_element_type=F32) + carry[...]
    rank1 = jnp.sum(oh1 * before, axis=-1, keepdims=True)
    rank2 = jnp.sum(oh2 * before, axis=-1, keepdims=True)
    total = carry[...] + jnp.sum(both, axis=0, keepdims=True)
    carry[...] = total
    cnt_ref[...] = total

    info = jnp.where(lane == 0, i1.astype(F32), 0.0)
    info = jnp.where(lane == 1, i2.astype(F32), info)
    info = jnp.where(lane == 2, rank1, info)
    info = jnp.where(lane == 3, rank2, info)
    info = jnp.where(lane == 4, p1, info)
    info = jnp.where(lane == 5, p2, info)
    info_ref[...] = info


def _router(h2, rw, rb):
    M, D = h2.shape
    tm = 512
    return pl.pallas_call(
        functools.partial(_router_kernel, tm=tm), name="router",
        out_shape=[jax.ShapeDtypeStruct((M, LANE), F32), jax.ShapeDtypeStruct((1, LANE), F32)],
        grid=(M // tm,),
        in_specs=[
            pl.BlockSpec((tm, D), lambda i: (i, 0)),
            pl.BlockSpec((D, LANE), lambda i: (0, 0)),
            pl.BlockSpec((1, LANE), lambda i: (0, 0)),
        ],
        out_specs=[pl.BlockSpec((tm, LANE), lambda i: (i, 0)), pl.BlockSpec((1, LANE), lambda i: (0, 0))],
        scratch_shapes=[pltpu.VMEM((1, LANE), F32)],
        compiler_params=_cparams(("arbitrary",)),
    )(h2, rw, rb)


MOE_TILE = 1024


def _moe_dispatch_kernel(s1_ref, s2_ref, h_ref, xs_in_ref, xs_ref, sem, *, tm):
    del xs_in_ref
    base = pl.program_id(0) * tm

    def copies(r):
        src = h_ref.at[pl.ds(r, 1)]
        return (pltpu.make_async_copy(src, xs_ref.at[pl.ds(s1_ref[base + r], 1)], sem.at[0]),
                pltpu.make_async_copy(src, xs_ref.at[pl.ds(s2_ref[base + r], 1)], sem.at[1]))

    def start(r, carry):
        for cp in copies(r):
            cp.start()
        return carry

    def wait(r, carry):
        for cp in copies(r):
            cp.wait()
        return carry

    lax.fori_loop(0, tm, start, 0, unroll=8)
    lax.fori_loop(0, tm, wait, 0, unroll=8)


def _moe_dispatch(slot1, slot2, h2, n_rows):
    M, D = h2.shape
    tm = 256
    xs0 = jnp.zeros((n_rows, D), F32)
    return pl.pallas_call(
        functools.partial(_moe_dispatch_kernel, tm=tm), name="moe_dispatch",
        out_shape=jax.ShapeDtypeStruct((n_rows, D), F32),
        grid_spec=pltpu.PrefetchScalarGridSpec(
            num_scalar_prefetch=2,
            grid=(M // tm,),
            in_specs=[pl.BlockSpec((tm, D), lambda i, s1, s2: (i, 0)),
                      pl.BlockSpec(memory_space=pl.ANY)],
            out_specs=pl.BlockSpec(memory_space=pl.ANY),
            scratch_shapes=[pltpu.SemaphoreType.DMA((2,))],
        ),
        input_output_aliases={3: 0},
        compiler_params=_cparams(("arbitrary",)),
    )(slot1, slot2, h2, xs0)


def _moe_expert_kernel(te_ref, nv_ref, xs_ref, w1_ref, w3_ref, w2_ref, ys_ref, xb):
    j = pl.program_id(1)
    nv = nv_ref[pl.program_id(0)]
    half = xb.shape[0] // 2

    @pl.when(j == 0)
    def _():
        ys_ref[...] = jnp.zeros_like(ys_ref)
        xb[...] = xs_ref[...].astype(BF16)

    @pl.when(nv > half)
    def _():
        _swiglu_accumulate(xb, w1_ref, w3_ref, w2_ref, ys_ref, 2 * half)

    @pl.when((nv > 0) & (nv <= half))
    def _():
        _swiglu_accumulate(xb, w1_ref, w3_ref, w2_ref, ys_ref, half)


def _moe_experts(tile_expert, n_valid, xs, w1, w3, w2):
    P, D = xs.shape
    E, _, F = w1.shape
    tm, tf = MOE_TILE, 256
    nf = F // tf
    fj = lambda i, j, nv: jnp.where(nv[i] > 0, j, nf - 1)
    return pl.pallas_call(
        _moe_expert_kernel, name="moe_experts",
        out_shape=jax.ShapeDtypeStruct((P, D), F32),
        grid_spec=pltpu.PrefetchScalarGridSpec(
            num_scalar_prefetch=2,
            grid=(P // tm, nf),
            in_specs=[
                pl.BlockSpec((tm, D), lambda i, j, te, nv: (i, 0)),
                pl.BlockSpec((1, D, tf), lambda i, j, te, nv: (te[i], 0, fj(i, j, nv))),
                pl.BlockSpec((1, D, tf), lambda i, j, te, nv: (te[i], 0, fj(i, j, nv))),
                pl.BlockSpec((1, tf, D), lambda i, j, te, nv: (te[i], fj(i, j, nv), 0)),
            ],
            out_specs=pl.BlockSpec((tm, D), lambda i, j, te, nv: (i, 0)),
            scratch_shapes=[pltpu.VMEM((tm, D), BF16)],
        ),
        compiler_params=_cparams(("arbitrary", "arbitrary")),
    )(tile_expert, n_valid, xs, w1, w3, w2)


def _moe_combine_kernel(s1_ref, s2_ref, ys_ref, info_ref, x_ref, g2_ref, o_ref, a_buf, b_buf, sem, *, tm):
    base = pl.program_id(0) * tm

    def copies(r):
        return (pltpu.make_async_copy(ys_ref.at[pl.ds(s1_ref[base + r], 1)], a_buf.at[pl.ds(r, 1)], sem.at[0]),
                pltpu.make_async_copy(ys_ref.at[pl.ds(s2_ref[base + r], 1)], b_buf.at[pl.ds(r, 1)], sem.at[1]))

    def start(r, carry):
        for cp in copies(r):
            cp.start()
        return carry

    def wait(r, carry):
        for cp in copies(r):
            cp.wait()
        return carry

    lax.fori_loop(0, tm, start, 0, unroll=8)
    lax.fori_loop(0, tm, wait, 0, unroll=8)
    info = info_ref[...]
    p1 = info[:, 4:5]
    p2 = info[:, 5:6]
    o_ref[...] = x_ref[...] + g2_ref[0] * (p1 * a_buf[...] + p2 * b_buf[...])


def _moe_combine(slot1, slot2, ys, info, x2, g2, seq):
    M, D = x2.shape
    tm = 256
    return pl.pallas_call(
        functools.partial(_moe_combine_kernel, tm=tm), name="moe_combine",
        out_shape=jax.ShapeDtypeStruct((M, D), F32),
        grid_spec=pltpu.PrefetchScalarGridSpec(
            num_scalar_prefetch=2,
            grid=(M // tm,),
            in_specs=[
                pl.BlockSpec(memory_space=pl.ANY),
                pl.BlockSpec((tm, LANE), lambda i, s1, s2: (i, 0)),
                pl.BlockSpec((tm, D), lambda i, s1, s2: (i, 0)),
                pl.BlockSpec((1, 1, D), lambda i, s1, s2: (i * tm // seq, 0, 0)),
            ],
            out_specs=pl.BlockSpec((tm, D), lambda i, s1, s2: (i, 0)),
            scratch_shapes=[pltpu.VMEM((tm, D), F32), pltpu.VMEM((tm, D), F32),
                            pltpu.SemaphoreType.DMA((2,))],
        ),
        compiler_params=_cparams(("arbitrary",)),
    )(slot1, slot2, ys, info, x2, g2)


def _moe(h2, rw, rb, w1, w3, w2, x2, g2, seq):
    M, D = x2.shape
    E = w1.shape[0]
    T = MOE_TILE
    n_rows = 2 * M + E * T
    n_tiles = n_rows // T
    info, counts = _router(h2, rw, rb)

    e1, e2, rank1, rank2 = (info[:, c].astype(jnp.int32) for c in range(4))
    cnt = counts[0, :E].astype(jnp.int32)
    padded = (cnt + T - 1) // T * T
    ends = jnp.cumsum(padded)
    off = ends - padded
    expert_ids = jnp.arange(E, dtype=jnp.int32)
    offset_of = lambda e: jnp.sum(jnp.where(e[:, None] == expert_ids[None, :], off[None, :], 0), axis=1)
    slot1 = offset_of(e1) + rank1
    slot2 = offset_of(e2) + rank2
    tile_start = jnp.arange(n_tiles, dtype=jnp.int32) * T
    in_use = tile_start < ends[E - 1]
    clamped = jnp.minimum(tile_start, ends[E - 1] - T)
    tile_expert = jnp.sum(clamped[:, None] >= ends[None, :], axis=1).astype(jnp.int32)
    is_e = tile_expert[:, None] == expert_ids[None, :]
    tokens_end = jnp.sum(jnp.where(is_e, (off + cnt)[None, :], 0), axis=1)
    n_valid = jnp.where(in_use, jnp.clip(tokens_end - tile_start, 0, T), 0).astype(jnp.int32)

    xs = _moe_dispatch(slot1, slot2, h2, n_rows)
    ys = _moe_experts(tile_expert, n_valid, xs, w1, w3, w2)
    return _moe_combine(slot1, slot2, ys, info, x2, g2, seq)


def _mixing_layer(x2, mods, p, w_t, layer, batch, seq, h_dtype):
    row = lambda t: t.reshape(1, -1)
    hz = _inproj(x2, row(p["norm1_g"]), mods["sc1"], mods["sh1"], w_t, layer, seq)

    mu = p["shift_mu"]
    mu3 = mu[:3 * RW_W].reshape(3, RW_W)
    mul = row(mu[3 * RW_W:])
    hm = lambda t: t.reshape(RW_H, 1, RW_HD)
    o_a = _rwkv(hz, mu3, mul, row(p["rw_w0"]), p["rw_w2"].astype(BF16), row(p["rw_a0"]),
                p["rw_a2"].astype(BF16), p["rw_g2"].astype(BF16), row(p["rw_k_k"]), row(p["rw_k_a"]),
                hm(p["rw_r_k"]), hm(p["rw_lnx_g"]), hm(p["rw_lnx_b"]), batch, seq)

    fb128 = jnp.zeros((1, LANE), F32).at[0, :FOX_H].set(p["fox_fb"])
    kx, qxT, vT = _fox_prep(hz, row(p["fox_qn_g"]), row(p["fox_kn_g"]), fb128, seq)
    o_b = _fox_attn(kx, qxT, vT, batch, seq)

    o_c = _sgu(hz, p["sg_ln_g"], p["sg_ln_b"], p["sg_ws"], jnp.transpose(p["sg_b"]))

    return _outproj(o_a, o_b, o_c, p["w_out"].astype(BF16), x2, mods["g1"], row(p["norm2_g"]),
                    mods["sc2"], mods["sh2"], seq, h_dtype)


def kernel(x, c, ada_w, ada_b, norm1_g, norm2_g, w_in, shift_mu, rw_w0, rw_w2, rw_a0, rw_a2, rw_g2, rw_k_k, rw_k_a, rw_r_k, rw_lnx_g, rw_lnx_b, fox_qn_g, fox_kn_g, fox_fb, sg_ln_g, sg_ln_b, sg_ws, sg_b, w_out, ffn_w1, ffn_w3, ffn_w2, moe_router_w, moe_router_b, moe_w1, moe_w3, moe_w2):
    B, S, D = x.shape
    L = ada_w.shape[0]
    x2 = x.reshape(B * S, D)
    c8 = jnp.zeros((8, D), F32).at[:B].set(c)
    mod = _ada_mod(c8, ada_w, ada_b.reshape(L, 1, N_MOD * D))

    layer_params = dict(
        norm1_g=norm1_g, norm2_g=norm2_g, shift_mu=shift_mu, rw_w0=rw_w0, rw_w2=rw_w2,
        rw_a0=rw_a0, rw_a2=rw_a2, rw_g2=rw_g2, rw_k_k=rw_k_k, rw_k_a=rw_k_a, rw_r_k=rw_r_k,
        rw_lnx_g=rw_lnx_g, rw_lnx_b=rw_lnx_b, fox_qn_g=fox_qn_g, fox_kn_g=fox_kn_g, fox_fb=fox_fb,
        sg_ln_g=sg_ln_g, sg_ln_b=sg_ln_b, sg_ws=sg_ws, sg_b=sg_b, w_out=w_out)

    w_t = jnp.transpose(w_in, (2, 0, 1))
    for l in range(L):
        names = ("sh1", "sc1", "g1", "sh2", "sc2", "g2")
        mods = {n: mod[l, :B, i * D:(i + 1) * D].reshape(B, 1, D) for i, n in enumerate(names)}
        p = {n: t[l] for n, t in layer_params.items()}
        dense = l % 2 == 0
        x2, h2 = _mixing_layer(x2, mods, p, w_t, l, B, S, BF16 if dense else F32)
        j = l // 2
        if dense:
            x2 = _ffn(h2, ffn_w1, ffn_w3, ffn_w2, j, x2, mods["g2"], S)
        else:
            rw = jnp.zeros((D, LANE), BF16).at[:, :N_EXPERTS].set(moe_router_w[j].astype(BF16))
            rb = jnp.zeros((1, LANE), F32).at[0, :N_EXPERTS].set(moe_router_b[j])
            x2 = _moe(h2, rw, rb, moe_w1[j], moe_w3[j], moe_w2[j], x2, mods["g2"], S)
    return x2.reshape(B, S, D)
```

```python
import functools

import jax
import jax.numpy as jnp
from jax import lax
from jax.experimental import pallas as pl
from jax.experimental.pallas import tpu as pltpu

F32 = jnp.float32
BF16 = jnp.bfloat16

D_MODEL = 2048
DEPTH = 2
RW_HD = 64
RW_W = 768
RW_H = 12
DECAY_LORA = 64
AAA_LORA = 64
GATE_LORA = 128
FOX_HD = 128
FOX_W = 768
FOX_H = 6
SG_GD = 128
SG_W = 512
SG_G = 4
SG_CHUNK = 128
N_EXPERTS = 8
N_MOD = 6
RMS_EPS = 1e-6
LN_EPS = 1e-5
LNX_EPS = 64e-5

C_R = 0
C_RWKV_END = 3 * RW_W + DECAY_LORA + AAA_LORA + GATE_LORA
C_Q = C_RWKV_END
C_FF = C_Q + 3 * FOX_W
C_U = C_FF + FOX_H
C_SV = C_U + SG_W
N_IN = C_SV + SG_W

P_R, P_K, P_V = 0, 768, 1536
P_LORA = 2304
P_Q, P_FK, P_FV = 2560, 3328, 4096
P_FF = C_FF
P_U = 5120
P_SV = P_U + SG_W
N_PACK = P_SV + SG_W
COL_TILE = 256

LANE = 128
VMEM_LIMIT = 56 * 1024 * 1024


def _cparams(sem):
    return pltpu.CompilerParams(dimension_semantics=sem, vmem_limit_bytes=VMEM_LIMIT)


def _sigmoid(x):
    return 1.0 / (1.0 + jnp.exp(-x))


def _softplus(x):
    return jnp.maximum(x, 0.0) + jnp.log(1.0 + jnp.exp(-jnp.abs(x)))


def _gelu_tanh(x):
    return 0.5 * x * (1.0 + jnp.tanh(0.7978845608028654 * (x + 0.044715 * (x * x * x))))


def _silu(x):
    return x * _sigmoid(x)


def _ada_kernel(c_ref, w_ref, b_ref, o_ref):
    cs = _silu(c_ref[...]).astype(BF16)
    w = w_ref[0].astype(BF16)
    o_ref[0] = jnp.dot(cs, w, preferred_element_type=F32) + b_ref[0]


def _ada_mod(c8, ada_w, ada_b3):
    L, D, N = ada_w.shape
    tn = 1536
    return pl.pallas_call(
        _ada_kernel, name="ada_mod",
        out_shape=jax.ShapeDtypeStruct((L, 8, N), F32),
        grid=(L, N // tn),
        in_specs=[
            pl.BlockSpec((8, D), lambda l, j: (0, 0)),
            pl.BlockSpec((1, D, tn), lambda l, j: (l, 0, j)),
            pl.BlockSpec((1, 1, tn), lambda l, j: (l, 0, j)),
        ],
        out_specs=pl.BlockSpec((1, 8, tn), lambda l, j: (l, 0, j)),
        compiler_params=_cparams(("parallel", "parallel")),
    )(c8, ada_w, ada_b3)


def _col_spec(tm, width, start):
    assert start % width == 0
    return pl.BlockSpec((tm, width), lambda i, *_: (i, start // width))


def _inproj_kernel(x_ref, g_ref, sc_ref, sh_ref, w_hbm, o_ref, h_scr, wbuf, sem, *, layer, tn):
    i, j = pl.program_id(0), pl.program_id(1)
    ni, nj = pl.num_programs(0), pl.num_programs(1)
    step = i * nj + j
    slot = step % 2

    def weight_copy(jj, s):
        first_row = jnp.where(jj < nj - 2, jj * tn, N_IN - (nj - jj) * tn)
        return pltpu.make_async_copy(w_hbm.at[pl.ds(first_row, tn), layer], wbuf.at[s], sem.at[s])

    @pl.when(step == 0)
    def _():
        weight_copy(j, slot).start()

    @pl.when(step + 1 < ni * nj)
    def _():
        weight_copy(jnp.where(j + 1 < nj, j + 1, 0), 1 - slot).start()

    @pl.when(j == 0)
    def _():
        chunk = 512
        for c in range(x_ref.shape[0] // chunk):
            rows = pl.ds(c * chunk, chunk)
            x = x_ref[rows, :]
            ms = jnp.mean(x * x, axis=-1, keepdims=True)
            y = x * lax.rsqrt(ms + RMS_EPS) * g_ref[...]
            h_scr[rows, :] = (y * (1.0 + sc_ref[0]) + sh_ref[0]).astype(BF16)

    weight_copy(j, slot).wait()
    o_ref[...] = lax.dot_general(h_scr[...], wbuf[slot].astype(BF16), (((1,), (1,)), ((), ())),
                                 preferred_element_type=F32)


def _inproj(x2, g, sc, sh, w_t, layer, seq):
    M, D = x2.shape
    tm, tn = min(2048, seq), SG_W
    nj = N_PACK // tn
    assert (nj - 2) * tn == P_U and P_FF + FOX_H <= P_U
    return pl.pallas_call(
        functools.partial(_inproj_kernel, layer=layer, tn=tn), name="inproj",
        out_shape=jax.ShapeDtypeStruct((M, N_PACK), F32),
        grid=(M // tm, nj),
        in_specs=[
            pl.BlockSpec((tm, D), lambda i, j: (i, 0), pipeline_mode=pl.Buffered(1)),
            pl.BlockSpec((1, D), lambda i, j: (0, 0)),
            pl.BlockSpec((1, 1, D), lambda i, j: (i * tm // seq, 0, 0)),
            pl.BlockSpec((1, 1, D), lambda i, j: (i * tm // seq, 0, 0)),
            pl.BlockSpec(memory_space=pl.ANY),
        ],
        out_specs=pl.BlockSpec((tm, tn), lambda i, j: (i, j)),
        scratch_shapes=[pltpu.VMEM((tm, D), BF16), pltpu.VMEM((2, tn, D), F32),
                        pltpu.SemaphoreType.DMA((2,))],
        compiler_params=_cparams(("arbitrary", "arbitrary")),
    )(x2, g, sc, sh, w_t)


def _rwkv_token_quantities(first, zr_ref, zk_ref, zv_ref, zl_ref, pr_ref, pk_ref, pv_ref, pl_ref,
                           mu_ref, mul_ref, w0_ref, w2_ref, a0_ref, a2_ref, g2_ref, kk_ref, ka_ref,
                           r_out, lw_out, k_out, v_out, kk_out, a_out, g_out):
    def shift(z_ref, p_ref, mu):
        z = z_ref[...]
        prev_last = jnp.where(first, 0.0, p_ref[7:8, :])
        zp = pltpu.roll(z, 1, 0)
        row = lax.broadcasted_iota(jnp.int32, z.shape, 0)
        zp = jnp.where(row == 0, prev_last, zp)
        return z + (zp - z) * mu

    r = shift(zr_ref, pr_ref, mu_ref[0:1, :])
    k = shift(zk_ref, pk_ref, mu_ref[1:2, :])
    v = shift(zv_ref, pv_ref, mu_ref[2:3, :])
    lo = shift(zl_ref, pl_ref, mul_ref[...])
    wd = lo[:, 0:DECAY_LORA]
    ad = lo[:, DECAY_LORA:DECAY_LORA + AAA_LORA]
    gd = lo[:, DECAY_LORA + AAA_LORA:]

    dec = w0_ref[...] + jnp.dot(jnp.tanh(wd).astype(BF16), w2_ref[...], preferred_element_type=F32)
    w_log = -_softplus(-dec) - 0.5
    lw = -jnp.exp(w_log)
    a = _sigmoid(a0_ref[...] + jnp.dot(ad.astype(BF16), a2_ref[...], preferred_element_type=F32))
    g = jnp.dot(_sigmoid(gd).astype(BF16), g2_ref[...], preferred_element_type=F32)
    kk = k * kk_ref[...]
    km = k * (1.0 + (a - 1.0) * ka_ref[...])

    for h in range(RW_H):
        sl = slice(h * RW_HD, (h + 1) * RW_HD)
        r_out[h] = r[:, sl]
        lw_out[h] = lw[:, sl]
        k_out[h] = km[:, sl]
        v_out[h] = v[:, sl]
        kk_out[h] = kk[:, sl]
        a_out[h] = a[:, sl]
        g_out[h] = g[:, sl]


def _bmm(a, b, ca, cb):
    return lax.dot_general(a.astype(BF16), b.astype(BF16), (((ca,), (cb,)), ((0,), (0,))),
                           preferred_element_type=F32)


def _split3(x):
    hi = x.astype(BF16)
    r1 = x - hi.astype(F32)
    mid = r1.astype(BF16)
    lo = (r1 - mid.astype(F32)).astype(BF16)
    return hi, mid, lo


def _rwkv_scan_kernel(r_ref, lw_ref, k_ref, v_ref, kk_ref, a_ref, g_ref, rk_ref, lng_ref, lnb_ref,
                      o_ref, s_scr, *, L, nC):
    H, K = RW_H, RW_HD
    n = H * nC

    @pl.when(pl.program_id(1) == 0)
    def _():
        s_scr[...] = jnp.zeros_like(s_scr)

    ld = lambda ref: ref[...].reshape(n, L, K)
    r, lw, k, v, kk, a = ld(r_ref), ld(lw_ref), ld(k_ref), ld(v_ref), ld(kk_ref), ld(a_ref)

    kk = kk / jnp.maximum(jnp.sqrt(jnp.sum(kk * kk, axis=-1, keepdims=True)), 1e-12)
    b = kk * a

    ti = lax.broadcasted_iota(jnp.int32, (L, L), 0)
    si = lax.broadcasted_iota(jnp.int32, (L, L), 1)
    tri_incl = jnp.broadcast_to((ti >= si).astype(BF16), (n, L, L))
    cw = sum(_bmm(tri_incl, piece, 2, 1) for piece in reversed(_split3(lw)))
    cw_last = cw[:, L - 1:L, :]
    e_in = jnp.exp(cw)
    e_out = jnp.exp(-cw)
    e_ex = jnp.exp(cw - lw)
    e_end = jnp.exp(cw_last - cw)
    w_end = jnp.exp(cw_last)

    at = -kk * e_ex
    rt = r * e_in
    bb = b * e_out
    kb = k * e_out
    bh = b * e_end
    kh = k * e_end

    atrt = jnp.concatenate([at, rt], axis=1)
    t2 = lax.broadcasted_iota(jnp.int32, (2 * L, L), 0)
    s2 = lax.broadcasted_iota(jnp.int32, (2 * L, L), 1)
    causal = (((t2 & (L - 1)) - s2 + jnp.where(t2 < L, 0, 1)) > 0)[None]
    pb = jnp.where(causal, _bmm(atrt, bb, 2, 2), 0.0)
    pk = jnp.where(causal, _bmm(atrt, kb, 2, 2), 0.0)
    a_ab, b_rb = pb[:, :L], pb[:, L:]
    a_ak, b_rk = pk[:, :L], pk[:, L:]

    x = jnp.broadcast_to((ti == si).astype(F32), (n, L, L))
    size = 1
    while size < L:
        sh = size.bit_length() - 1
        m = (((ti >> (sh + 1)) == (si >> (sh + 1))) & (((ti >> sh) & 1) == 1) & (((si >> sh) & 1) == 0))[None]
        x = x + _bmm(_bmm(x, jnp.where(m, a_ab, 0.0), 2, 1), x, 2, 1)
        size *= 2

    xu = _bmm(x, jnp.concatenate([at, _bmm(a_ak, v, 2, 1)], axis=-1), 2, 1)
    w = _bmm(b_rb, xu, 2, 1)
    rp = rt + w[..., :K]
    y0 = w[..., K:] + _bmm(b_rk, v, 2, 1)
    big = _bmm(xu, bh, 1, 1)
    gp = big[:, :K]
    cc = big[:, K:] + _bmm(v, kh, 1, 1)

    c4 = lambda t: t.reshape((H, nC) + t.shape[1:])
    rp, y0, gp, cc, w_end = c4(rp), c4(y0), c4(gp), c4(cc), c4(w_end)
    s = s_scr[...]
    ys = []
    for c in range(nC):
        ys.append(_bmm(rp[:, c], s, 2, 2) + y0[:, c])
        s = s * w_end[:, c] + _bmm(s, gp[:, c], 2, 1) + cc[:, c]
    s_scr[...] = s
    y = ys[0] if nC == 1 else jnp.concatenate(ys, axis=1)

    mu = jnp.mean(y, axis=-1, keepdims=True)
    yc = y - mu
    var = jnp.mean(yc * yc, axis=-1, keepdims=True)
    yn = yc * lax.rsqrt(var + LNX_EPS) * lng_ref[...] + lnb_ref[...]
    r3, k3, v3 = r_ref[...], k_ref[...], v_ref[...]
    bonus = jnp.sum(r3 * k3 * rk_ref[...], axis=-1, keepdims=True) * v3
    out = (yn + bonus) * g_ref[...]
    for p in range(H // 2):
        pair = jnp.concatenate([out[2 * p], out[2 * p + 1]], axis=-1)
        o_ref[:, p * LANE:(p + 1) * LANE] = pair.astype(o_ref.dtype)


N_TOKEN_INPUTS = 17


def _rwkv_kernel(*refs, L, nC):
    token_in = refs[:N_TOKEN_INPUTS]
    rk_ref, lng_ref, lnb_ref, o_ref, s_scr = refs[N_TOKEN_INPUTS:N_TOKEN_INPUTS + 5]
    head_major = refs[N_TOKEN_INPUTS + 5:]
    _rwkv_token_quantities(pl.program_id(1) == 0, *token_in, *head_major)
    _rwkv_scan_kernel(*head_major, rk_ref, lng_ref, lnb_ref, o_ref, s_scr, L=L, nC=nC)


def _rwkv(hz, mu3, mul, w0, w2, a0, a2, g2, k_k, k_a, r_k, lnx_g, lnx_b, batch, seq):
    M = hz.shape[0]
    L, nC = 64, 4
    tb = L * nC
    nj = seq // tb
    blk = lambda b, j: b * nj + j
    prev = lambda b, j: jnp.maximum(blk(b, j) * (tb // 8) - 1, 0)
    row = lambda n: pl.BlockSpec((1, n), lambda b, j: (0, 0))
    full = lambda r, c: pl.BlockSpec((r, c), lambda b, j: (0, 0))
    par = pl.BlockSpec((RW_H, 1, RW_HD), lambda b, j: (0, 0, 0))
    in_specs = [
        pl.BlockSpec((tb, RW_W), lambda b, j: (blk(b, j), P_R // RW_W)),
        pl.BlockSpec((tb, RW_W), lambda b, j: (blk(b, j), P_K // RW_W)),
        pl.BlockSpec((tb, RW_W), lambda b, j: (blk(b, j), P_V // RW_W)),
        pl.BlockSpec((tb, COL_TILE), lambda b, j: (blk(b, j), P_LORA // COL_TILE)),
        pl.BlockSpec((8, RW_W), lambda b, j: (prev(b, j), P_R // RW_W)),
        pl.BlockSpec((8, RW_W), lambda b, j: (prev(b, j), P_K // RW_W)),
        pl.BlockSpec((8, RW_W), lambda b, j: (prev(b, j), P_V // RW_W)),
        pl.BlockSpec((8, COL_TILE), lambda b, j: (prev(b, j), P_LORA // COL_TILE)),
        full(3, RW_W), row(COL_TILE), row(RW_W), full(DECAY_LORA, RW_W), row(RW_W),
        full(AAA_LORA, RW_W), full(GATE_LORA, RW_W), row(RW_W), row(RW_W),
        par, par, par,
    ]
    assert len(in_specs) == N_TOKEN_INPUTS + 3
    head_major = pltpu.VMEM((RW_H, tb, RW_HD), F32)
    return pl.pallas_call(
        functools.partial(_rwkv_kernel, L=L, nC=nC), name="rwkv",
        out_shape=jax.ShapeDtypeStruct((M, RW_W), BF16),
        grid=(batch, nj),
        in_specs=in_specs,
        out_specs=pl.BlockSpec((tb, RW_W), lambda b, j: (blk(b, j), 0)),
        scratch_shapes=[pltpu.VMEM((RW_H, RW_HD, RW_HD), F32)] + [head_major] * 7,
        compiler_params=_cparams(("parallel", "arbitrary")),
    )(*([hz] * 8), mu3, mul, w0, w2, a0, a2, g2, k_k, k_a, r_k, lnx_g, lnx_b)


FOX_XD = 2 * FOX_HD


def _fox_prep_kernel(*refs, tm, seq):
    q_refs, k_refs, v_refs = refs[0:3], refs[3:6], refs[6:9]
    f_ref, qg_ref, kg_ref, fb_ref, kx_ref, qxT_ref, vT_ref, carry = refs[9:]

    @pl.when((pl.program_id(0) * tm) % seq == 0)
    def _():
        carry[...] = jnp.zeros_like(carry)

    logf = -_softplus(-(f_ref[...] + fb_ref[...]))
    ti = lax.broadcasted_iota(jnp.int32, (tm, tm), 0)
    si = lax.broadcasted_iota(jnp.int32, (tm, tm), 1)
    tri = (ti >= si).astype(BF16)
    cum = carry[...]
    for piece in reversed(_split3(logf)):
        cum = cum + jnp.dot(tri, piece, preferred_element_type=F32)
    carry[...] = cum[tm - 1:tm, :]
    cum_t = cum.T

    scale = FOX_HD ** -0.5
    lane = lax.broadcasted_iota(jnp.int32, (tm, FOX_HD), 1)
    sub = lax.broadcasted_iota(jnp.int32, (FOX_HD, tm), 0)
    pieces = lambda x: [t.astype(F32) for t in _split3(x)]
    for h in range(FOX_H):
        blk, half = h // 2, slice((h % 2) * FOX_HD, (h % 2 + 1) * FOX_HD)
        q = q_refs[blk][:, half]
        k = k_refs[blk][:, half]
        qn = q * lax.rsqrt(jnp.mean(q * q, axis=-1, keepdims=True) + RMS_EPS) * qg_ref[...]
        kn = k * lax.rsqrt(jnp.mean(k * k, axis=-1, keepdims=True) + RMS_EPS) * kg_ref[...]

        c_hi, c_mid, c_lo = pieces(cum[:, h:h + 1])
        k_extra = jnp.where(lane < 3, 1.0, 0.0)
        k_extra = jnp.where(lane == 3, -c_hi, k_extra)
        k_extra = jnp.where(lane == 4, -c_mid, k_extra)
        k_extra = jnp.where(lane == 5, -c_lo, k_extra)
        kx_ref[:, h * FOX_XD:h * FOX_XD + FOX_HD] = kn.astype(BF16)
        kx_ref[:, h * FOX_XD + FOX_HD:(h + 1) * FOX_XD] = k_extra.astype(BF16)

        r_hi, r_mid, r_lo = pieces(cum_t[h:h + 1, :])
        q_extra = jnp.where(sub < 6, 1.0, 0.0)
        q_extra = jnp.where(sub == 0, r_hi, q_extra)
        q_extra = jnp.where(sub == 1, r_mid, q_extra)
        q_extra = jnp.where(sub == 2, r_lo, q_extra)
        qxT_ref[h * FOX_XD:h * FOX_XD + FOX_HD, :] = (qn * scale).T.astype(BF16)
        qxT_ref[h * FOX_XD + FOX_HD:(h + 1) * FOX_XD, :] = q_extra.astype(BF16)

        vT_ref[h * FOX_HD:(h + 1) * FOX_HD, :] = v_refs[blk][:, half].T.astype(BF16)


def _fox_prep(hz, qn_g, kn_g, fb128, seq):
    M = hz.shape[0]
    tm = 512
    row = pl.BlockSpec((1, LANE), lambda i: (0, 0))
    pieces = lambda start: [_col_spec(tm, COL_TILE, start + c * COL_TILE) for c in range(FOX_W // COL_TILE)]
    return pl.pallas_call(
        functools.partial(_fox_prep_kernel, tm=tm, seq=seq), name="fox_prep",
        out_shape=[jax.ShapeDtypeStruct((M, FOX_H * FOX_XD), BF16),
                   jax.ShapeDtypeStruct((FOX_H * FOX_XD, M), BF16),
                   jax.ShapeDtypeStruct((FOX_W, M), BF16)],
        grid=(M // tm,),
        in_specs=pieces(P_Q) + pieces(P_FK) + pieces(P_FV) + [_col_spec(tm, LANE, P_FF), row, row, row],
        out_specs=[pl.BlockSpec((tm, FOX_H * FOX_XD), lambda i: (i, 0)),
                   pl.BlockSpec((FOX_H * FOX_XD, tm), lambda i: (0, i)),
                   pl.BlockSpec((FOX_W, tm), lambda i: (0, i))],
        scratch_shapes=[pltpu.VMEM((1, LANE), F32)],
        compiler_params=_cparams(("arbitrary",)),
    )(*([hz] * 10), qn_g, kn_g, fb128)


def _fox_attn_kernel(qt_ref, kt_ref, kx_ref, qxT_ref, vT_ref, o_ref, m_scr, l_scr, acc_scr, *, tq, tk):
    step_id = pl.program_id(1)
    qi, ki = qt_ref[step_id], kt_ref[step_id]
    ratio = tq // tk
    first_diag = ratio * qi
    last = ki == first_diag + ratio - 1

    @pl.when(ki == 0)
    def _():
        m_scr[...] = jnp.full_like(m_scr, -1e30)
        l_scr[...] = jnp.zeros_like(l_scr)
        acc_scr[...] = jnp.zeros_like(acc_scr)

    def step(diagonal):
        if diagonal:
            keep = (lax.broadcasted_iota(jnp.int32, (tk, tq), 1) - lax.broadcasted_iota(jnp.int32, (tk, tq), 0)
                    >= ki * tk - qi * tq)
        for h in range(FOX_H):
            xs = slice(h * FOX_XD, (h + 1) * FOX_XD)
            hs = slice(h * FOX_HD, (h + 1) * FOX_HD)
            s = jnp.dot(kx_ref[:, xs], qxT_ref[xs, :], preferred_element_type=F32)
            if diagonal:
                s = jnp.where(keep, s, -1e30)
            m_prev = m_scr[h]
            m_new = jnp.maximum(m_prev, jnp.max(s, axis=0, keepdims=True))
            alpha = jnp.exp(m_prev - m_new)
            p = jnp.exp(s - m_new)
            m_scr[h] = m_new
            l_scr[h] = alpha * l_scr[h] + jnp.sum(p, axis=0, keepdims=True)
            acc_scr[hs, :] = alpha * acc_scr[hs, :] + jnp.dot(vT_ref[hs, :], p.astype(BF16),
                                                              preferred_element_type=F32)

    pl.when(ki < first_diag)(lambda: step(False))
    pl.when(ki >= first_diag)(lambda: step(True))

    @pl.when(last)
    def _():
        for h in range(FOX_H):
            hs = slice(h * FOX_HD, (h + 1) * FOX_HD)
            o_ref[:, hs] = (acc_scr[hs, :] / l_scr[h]).T.astype(o_ref.dtype)


def _fox_attn(kx, qxT, vT, batch, seq):
    M = kx.shape[0]
    tq, tk = min(1024, seq), 512
    nq, nk, ratio = seq // tq, seq // tk, tq // tk
    pairs = [(q, k) for q in range(nq) for k in range(ratio * (q + 1))]
    q_tab = jnp.asarray([q for q, _ in pairs], jnp.int32)
    k_tab = jnp.asarray([k for _, k in pairs], jnp.int32)
    return pl.pallas_call(
        functools.partial(_fox_attn_kernel, tq=tq, tk=tk), name="fox_attn",
        out_shape=jax.ShapeDtypeStruct((M, FOX_W), BF16),
        grid_spec=pltpu.PrefetchScalarGridSpec(
            num_scalar_prefetch=2,
            grid=(batch, len(pairs)),
            in_specs=[
                pl.BlockSpec((tk, FOX_H * FOX_XD), lambda b, s, qt, kt: (b * nk + kt[s], 0)),
                pl.BlockSpec((FOX_H * FOX_XD, tq), lambda b, s, qt, kt: (0, b * nq + qt[s])),
                pl.BlockSpec((FOX_W, tk), lambda b, s, qt, kt: (0, b * nk + kt[s])),
            ],
            out_specs=pl.BlockSpec((tq, FOX_W), lambda b, s, qt, kt: (b * nq + qt[s], 0)),
            scratch_shapes=[pltpu.VMEM((FOX_H, 1, tq), F32), pltpu.VMEM((FOX_H, 1, tq), F32),
                            pltpu.VMEM((FOX_W, tq), F32)],
        ),
        compiler_params=_cparams(("parallel", "arbitrary")),
    )(q_tab, k_tab, kx, qxT, vT)


def _sgu_kernel(u_ref, v_ref, lg_ref, lb_ref, ws_ref, sb_ref, o_ref, *, tm):
    ti = lax.broadcasted_iota(jnp.int32, (SG_CHUNK, SG_CHUNK), 0)
    si = lax.broadcasted_iota(jnp.int32, (SG_CHUNK, SG_CHUNK), 1)
    for g in range(SG_G):
        sl = slice(g * SG_GD, (g + 1) * SG_GD)
        u = _gelu_tanh(u_ref[:, sl])
        v = _gelu_tanh(v_ref[:, sl])
        mu = jnp.mean(v, axis=-1, keepdims=True)
        vc = v - mu
        var = jnp.mean(vc * vc, axis=-1, keepdims=True)
        vn = (vc * lax.rsqrt(var + LN_EPS) * lg_ref[g:g + 1, :] + lb_ref[g:g + 1, :]).astype(BF16)
        ws = jnp.where(ti >= si, ws_ref[g], 0.0).astype(BF16)
        bias = sb_ref[:, g:g + 1]
        for c in range(tm // SG_CHUNK):
            rows = slice(c * SG_CHUNK, (c + 1) * SG_CHUNK)
            mixed = jnp.dot(ws, vn[rows], preferred_element_type=F32) + bias
            o_ref[rows, sl] = (u[rows] * mixed).astype(o_ref.dtype)


def _sgu(hz, ln_g, ln_b, ws, sb_t):
    M = hz.shape[0]
    tm = 512
    return pl.pallas_call(
        functools.partial(_sgu_kernel, tm=tm), name="sgu",
        out_shape=jax.ShapeDtypeStruct((M, SG_W), BF16),
        grid=(M // tm,),
        in_specs=[
            _col_spec(tm, SG_W, P_U), _col_spec(tm, SG_W, P_SV),
            pl.BlockSpec((SG_G, SG_GD), lambda i: (0, 0)),
            pl.BlockSpec((SG_G, SG_GD), lambda i: (0, 0)),
            pl.BlockSpec((SG_G, SG_CHUNK, SG_CHUNK), lambda i: (0, 0, 0)),
            pl.BlockSpec((SG_CHUNK, SG_G), lambda i: (0, 0)),
        ],
        out_specs=pl.BlockSpec((tm, SG_W), lambda i: (i, 0)),
        compiler_params=_cparams(("parallel",)),
    )(hz, hz, ln_g, ln_b, ws, sb_t)


def _outproj_kernel(oa_ref, ob_ref, oc_ref, w_ref, x_ref, g1_ref, ng_ref, sc_ref, sh_ref,
                    xo_ref, h_ref):
    mix = jnp.dot(oa_ref[...], w_ref[0:RW_W, :], preferred_element_type=F32)
    mix += jnp.dot(ob_ref[...], w_ref[RW_W:RW_W + FOX_W, :], preferred_element_type=F32)
    mix += jnp.dot(oc_ref[...], w_ref[RW_W + FOX_W:, :], preferred_element_type=F32)
    x = x_ref[...] + g1_ref[0] * mix
    xo_ref[...] = x
    ms = jnp.mean(x * x, axis=-1, keepdims=True)
    y = x * lax.rsqrt(ms + RMS_EPS) * ng_ref[...]
    h_ref[...] = (y * (1.0 + sc_ref[0]) + sh_ref[0]).astype(h_ref.dtype)


def _outproj(oa, ob, oc, w, x2, g1, ng, sc, sh, seq, h_dtype):
    M, D = x2.shape
    tm = 512
    bidx = lambda i: (i * tm // seq, 0, 0)
    mod = pl.BlockSpec((1, 1, D), bidx)
    return pl.pallas_call(
        _outproj_kernel, name="outproj",
        out_shape=[jax.ShapeDtypeStruct((M, D), F32), jax.ShapeDtypeStruct((M, D), h_dtype)],
        grid=(M // tm,),
        in_specs=[
            pl.BlockSpec((tm, RW_W), lambda i: (i, 0)),
            pl.BlockSpec((tm, FOX_W), lambda i: (i, 0)),
            pl.BlockSpec((tm, SG_W), lambda i: (i, 0)),
            pl.BlockSpec((D, D), lambda i: (0, 0)),
            pl.BlockSpec((tm, D), lambda i: (i, 0)),
            mod,
            pl.BlockSpec((1, D), lambda i: (0, 0)),
            mod, mod,
        ],
        out_specs=[pl.BlockSpec((tm, D), lambda i: (i, 0)), pl.BlockSpec((tm, D), lambda i: (i, 0))],
        compiler_params=_cparams(("parallel",)),
    )(oa, ob, oc, w, x2, g1, ng, sc, sh)


def _swiglu_accumulate(h_ref, w1_ref, w3_ref, w2_ref, acc_ref, rows):
    h = h_ref[0:rows, :]
    a = jnp.dot(h, w1_ref[0].astype(BF16), preferred_element_type=F32)
    b = jnp.dot(h, w3_ref[0].astype(BF16), preferred_element_type=F32)
    acc_ref[0:rows, :] += jnp.dot((_silu(a) * b).astype(BF16), w2_ref[0].astype(BF16),
                                  preferred_element_type=F32)


def _ffn_kernel(h_ref, w1_ref, w3_ref, w2_ref, x_ref, g2_ref, o_ref):
    j = pl.program_id(1)

    @pl.when(j == 0)
    def _():
        o_ref[...] = jnp.zeros_like(o_ref)

    _swiglu_accumulate(h_ref, w1_ref, w3_ref, w2_ref, o_ref, h_ref.shape[0])

    @pl.when(j == pl.num_programs(1) - 1)
    def _():
        o_ref[...] = x_ref[...] + g2_ref[0] * o_ref[...]


def _ffn(h2, w1, w3, w2, layer, x2, g2, seq):
    M, D = x2.shape
    F = w1.shape[2]
    tm, tf = 1024, 512
    once = dict(pipeline_mode=pl.Buffered(1))
    mod = pl.BlockSpec((1, 1, D), lambda i, j: (i * tm // seq, 0, 0))
    return pl.pallas_call(
        _ffn_kernel, name="ffn",
        out_shape=jax.ShapeDtypeStruct((M, D), F32),
        grid=(M // tm, F // tf),
        in_specs=[
            pl.BlockSpec((tm, D), lambda i, j: (i, 0), **once),
            pl.BlockSpec((1, D, tf), lambda i, j: (layer, 0, j)),
            pl.BlockSpec((1, D, tf), lambda i, j: (layer, 0, j)),
            pl.BlockSpec((1, tf, D), lambda i, j: (layer, j, 0)),
            pl.BlockSpec((tm, D), lambda i, j: (i, 0), **once),
            mod,
        ],
        out_specs=pl.BlockSpec((tm, D), lambda i, j: (i, 0), **once),
        compiler_params=_cparams(("parallel", "arbitrary")),
    )(h2, w1, w3, w2, x2, g2)


def _router_kernel(h_ref, w_ref, b_ref, info_ref, cnt_ref, carry, *, tm):
    @pl.when(pl.program_id(0) == 0)
    def _():
        carry[...] = jnp.zeros_like(carry)

    logits = jnp.dot(h_ref[...].astype(BF16), w_ref[...], preferred_element_type=F32) + b_ref[...]
    lane = lax.broadcasted_iota(jnp.int32, logits.shape, 1)
    neg = -1e30
    logits = jnp.where(lane < N_EXPERTS, logits, neg)
    m1 = jnp.max(logits, axis=-1, keepdims=True)
    i1 = jnp.min(jnp.where(logits == m1, lane, LANE), axis=-1, keepdims=True)
    rest = jnp.where(lane == i1, neg, logits)
    m2 = jnp.max(rest, axis=-1, keepdims=True)
    i2 = jnp.min(jnp.where(rest == m2, lane, LANE), axis=-1, keepdims=True)
    e2 = jnp.exp(m2 - m1)
    p1 = 1.0 / (1.0 + e2)
    p2 = e2 / (1.0 + e2)

    oh1 = (lane == i1).astype(F32)
    oh2 = (lane == i2).astype(F32)
    both = oh1 + oh2
    ti = lax.broadcasted_iota(jnp.int32, (tm, tm), 0)
    si = lax.broadcasted_iota(jnp.int32, (tm, tm), 1)
    strict = (ti > si).astype(BF16)
    before = jnp.dot(strict, both.astype(BF16), preferred_element_type=F32) + carry[...]
    rank1 = jnp.sum(oh1 * before, axis=-1, keepdims=True)
    rank2 = jnp.sum(oh2 * before, axis=-1, keepdims=True)
    total = carry[...] + jnp.sum(both, axis=0, keepdims=True)
    carry[...] = total
    cnt_ref[...] = total

    info = jnp.where(lane == 0, i1.astype(F32), 0.0)
    info = jnp.where(lane == 1, i2.astype(F32), info)
    info = jnp.where(lane == 2, rank1, info)
    info = jnp.where(lane == 3, rank2, info)
    info = jnp.where(lane == 4, p1, info)
    info = jnp.where(lane == 5, p2, info)
    info_ref[...] = info


def _router(h2, rw, rb):
    M, D = h2.shape
    tm = 512
    return pl.pallas_call(
        functools.partial(_router_kernel, tm=tm), name="router",
        out_shape=[jax.ShapeDtypeStruct((M, LANE), F32), jax.ShapeDtypeStruct((1, LANE), F32)],
        grid=(M // tm,),
        in_specs=[
            pl.BlockSpec((tm, D), lambda i: (i, 0)),
            pl.BlockSpec((D, LANE), lambda i: (0, 0)),
            pl.BlockSpec((1, LANE), lambda i: (0, 0)),
        ],
        out_specs=[pl.BlockSpec((tm, LANE), lambda i: (i, 0)), pl.BlockSpec((1, LANE), lambda i: (0, 0))],
        scratch_shapes=[pltpu.VMEM((1, LANE), F32)],
        compiler_params=_cparams(("arbitrary",)),
    )(h2, rw, rb)


MOE_TILE = 1024


def _moe_dispatch_kernel(s1_ref, s2_ref, h_ref, xs_in_ref, xs_ref, sem, *, tm):
    del xs_in_ref
    base = pl.program_id(0) * tm

    def copies(r):
        src = h_ref.at[pl.ds(r, 1)]
        return (pltpu.make_async_copy(src, xs_ref.at[pl.ds(s1_ref[base + r], 1)], sem.at[0]),
                pltpu.make_async_copy(src, xs_ref.at[pl.ds(s2_ref[base + r], 1)], sem.at[1]))

    def start(r, carry):
        for cp in copies(r):
            cp.start()
        return carry

    def wait(r, carry):
        for cp in copies(r):
            cp.wait()
        return carry

    lax.fori_loop(0, tm, start, 0, unroll=8)
    lax.fori_loop(0, tm, wait, 0, unroll=8)


def _moe_dispatch(slot1, slot2, h2, n_rows):
    M, D = h2.shape
    tm = 256
    xs0 = jnp.zeros((n_rows, D), F32)
    return pl.pallas_call(
        functools.partial(_moe_dispatch_kernel, tm=tm), name="moe_dispatch",
        out_shape=jax.ShapeDtypeStruct((n_rows, D), F32),
        grid_spec=pltpu.PrefetchScalarGridSpec(
            num_scalar_prefetch=2,
            grid=(M // tm,),
            in_specs=[pl.BlockSpec((tm, D), lambda i, s1, s2: (i, 0)),
                      pl.BlockSpec(memory_space=pl.ANY)],
            out_specs=pl.BlockSpec(memory_space=pl.ANY),
            scratch_shapes=[pltpu.SemaphoreType.DMA((2,))],
        ),
        input_output_aliases={3: 0},
        compiler_params=_cparams(("arbitrary",)),
    )(slot1, slot2, h2, xs0)


def _moe_expert_kernel(te_ref, nv_ref, xs_ref, w1_ref, w3_ref, w2_ref, ys_ref, xb):
    j = pl.program_id(1)
    nv = nv_ref[pl.program_id(0)]
    half = xb.shape[0] // 2

    @pl.when(j == 0)
    def _():
        ys_ref[...] = jnp.zeros_like(ys_ref)
        xb[...] = xs_ref[...].astype(BF16)

    @pl.when(nv > half)
    def _():
        _swiglu_accumulate(xb, w1_ref, w3_ref, w2_ref, ys_ref, 2 * half)

    @pl.when((nv > 0) & (nv <= half))
    def _():
        _swiglu_accumulate(xb, w1_ref, w3_ref, w2_ref, ys_ref, half)


def _moe_experts(tile_expert, n_valid, xs, w1, w3, w2):
    P, D = xs.shape
    E, _, F = w1.shape
    tm, tf = MOE_TILE, 256
    nf = F // tf
    fj = lambda i, j, nv: jnp.where(nv[i] > 0, j, nf - 1)
    return pl.pallas_call(
        _moe_expert_kernel, name="moe_experts",
        out_shape=jax.ShapeDtypeStruct((P, D), F32),
        grid_spec=pltpu.PrefetchScalarGridSpec(
            num_scalar_prefetch=2,
            grid=(P // tm, nf),
            in_specs=[
                pl.BlockSpec((tm, D), lambda i, j, te, nv: (i, 0)),
                pl.BlockSpec((1, D, tf), lambda i, j, te, nv: (te[i], 0, fj(i, j, nv))),
                pl.BlockSpec((1, D, tf), lambda i, j, te, nv: (te[i], 0, fj(i, j, nv))),
                pl.BlockSpec((1, tf, D), lambda i, j, te, nv: (te[i], fj(i, j, nv), 0)),
            ],
            out_specs=pl.BlockSpec((tm, D), lambda i, j, te, nv: (i, 0)),
            scratch_shapes=[pltpu.VMEM((tm, D), BF16)],
        ),
        compiler_params=_cparams(("arbitrary", "arbitrary")),
    )(tile_expert, n_valid, xs, w1, w3, w2)


def _moe_combine_kernel(s1_ref, s2_ref, ys_ref, info_ref, x_ref, g2_ref, o_ref, a_buf, b_buf, sem, *, tm):
    base = pl.program_id(0) * tm

    def copies(r):
        return (pltpu.make_async_copy(ys_ref.at[pl.ds(s1_ref[base + r], 1)], a_buf.at[pl.ds(r, 1)], sem.at[0]),
                pltpu.make_async_copy(ys_ref.at[pl.ds(s2_ref[base + r], 1)], b_buf.at[pl.ds(r, 1)], sem.at[1]))

    def start(r, carry):
        for cp in copies(r):
            cp.start()
        return carry

    def wait(r, carry):
        for cp in copies(r):
            cp.wait()
        return carry

    lax.fori_loop(0, tm, start, 0, unroll=8)
    lax.fori_loop(0, tm, wait, 0, unroll=8)
    info = info_ref[...]
    p1 = info[:, 4:5]
    p2 = info[:, 5:6]
    o_ref[...] = x_ref[...] + g2_ref[0] * (p1 * a_buf[...] + p2 * b_buf[...])


def _moe_combine(slot1, slot2, ys, info, x2, g2, seq):
    M, D = x2.shape
    tm = 256
    return pl.pallas_call(
        functools.partial(_moe_combine_kernel, tm=tm), name="moe_combine",
        out_shape=jax.ShapeDtypeStruct((M, D), F32),
        grid_spec=pltpu.PrefetchScalarGridSpec(
            num_scalar_prefetch=2,
            grid=(M // tm,),
            in_specs=[
                pl.BlockSpec(memory_space=pl.ANY),
                pl.BlockSpec((tm, LANE), lambda i, s1, s2: (i, 0)),
                pl.BlockSpec((tm, D), lambda i, s1, s2: (i, 0)),
                pl.BlockSpec((1, 1, D), lambda i, s1, s2: (i * tm // seq, 0, 0)),
            ],
            out_specs=pl.BlockSpec((tm, D), lambda i, s1, s2: (i, 0)),
            scratch_shapes=[pltpu.VMEM((tm, D), F32), pltpu.VMEM((tm, D), F32),
                            pltpu.SemaphoreType.DMA((2,))],
        ),
        compiler_params=_cparams(("arbitrary",)),
    )(slot1, slot2, ys, info, x2, g2)


def _moe(h2, rw, rb, w1, w3, w2, x2, g2, seq):
    M, D = x2.shape
    E = w1.shape[0]
    T = MOE_TILE
    n_rows = 2 * M + E * T
    n_tiles = n_rows // T
    info, counts = _router(h2, rw, rb)

    e1, e2, rank1, rank2 = (info[:, c].astype(jnp.int32) for c in range(4))
    cnt = counts[0, :E].astype(jnp.int32)
    padded = (cnt + T - 1) // T * T
    ends = jnp.cumsum(padded)
    off = ends - padded
    expert_ids = jnp.arange(E, dtype=jnp.int32)
    offset_of = lambda e: jnp.sum(jnp.where(e[:, None] == expert_ids[None, :], off[None, :], 0), axis=1)
    slot1 = offset_of(e1) + rank1
    slot2 = offset_of(e2) + rank2
    tile_start = jnp.arange(n_tiles, dtype=jnp.int32) * T
    in_use = tile_start < ends[E - 1]
    clamped = jnp.minimum(tile_start, ends[E - 1] - T)
    tile_expert = jnp.sum(clamped[:, None] >= ends[None, :], axis=1).astype(jnp.int32)
    is_e = tile_expert[:, None] == expert_ids[None, :]
    tokens_end = jnp.sum(jnp.where(is_e, (off + cnt)[None, :], 0), axis=1)
    n_valid = jnp.where(in_use, jnp.clip(tokens_end - tile_start, 0, T), 0).astype(jnp.int32)

    xs = _moe_dispatch(slot1, slot2, h2, n_rows)
    ys = _moe_experts(tile_expert, n_valid, xs, w1, w3, w2)
    return _moe_combine(slot1, slot2, ys, info, x2, g2, seq)


def _mixing_layer(x2, mods, p, w_t, layer, batch, seq, h_dtype):
    row = lambda t: t.reshape(1, -1)
    hz = _inproj(x2, row(p["norm1_g"]), mods["sc1"], mods["sh1"], w_t, layer, seq)

    mu = p["shift_mu"]
    mu3 = mu[:3 * RW_W].reshape(3, RW_W)
    mul = row(mu[3 * RW_W:])
    hm = lambda t: t.reshape(RW_H, 1, RW_HD)
    o_a = _rwkv(hz, mu3, mul, row(p["rw_w0"]), p["rw_w2"].astype(BF16), row(p["rw_a0"]),
                p["rw_a2"].astype(BF16), p["rw_g2"].astype(BF16), row(p["rw_k_k"]), row(p["rw_k_a"]),
                hm(p["rw_r_k"]), hm(p["rw_lnx_g"]), hm(p["rw_lnx_b"]), batch, seq)

    fb128 = jnp.zeros((1, LANE), F32).at[0, :FOX_H].set(p["fox_fb"])
    kx, qxT, vT = _fox_prep(hz, row(p["fox_qn_g"]), row(p["fox_kn_g"]), fb128, seq)
    o_b = _fox_attn(kx, qxT, vT, batch, seq)

    o_c = _sgu(hz, p["sg_ln_g"], p["sg_ln_b"], p["sg_ws"], jnp.transpose(p["sg_b"]))

    return _outproj(o_a, o_b, o_c, p["w_out"].astype(BF16), x2, mods["g1"], row(p["norm2_g"]),
                    mods["sc2"], mods["sh2"], seq, h_dtype)


def kernel(x, c, ada_w, ada_b, norm1_g, norm2_g, w_in, shift_mu, rw_w0, rw_w2, rw_a0, rw_a2, rw_g2, rw_k_k, rw_k_a, rw_r_k, rw_lnx_g, rw_lnx_b, fox_qn_g, fox_kn_g, fox_fb, sg_ln_g, sg_ln_b, sg_ws, sg_b, w_out, ffn_w1, ffn_w3, ffn_w2, moe_router_w, moe_router_b, moe_w1, moe_w3, moe_w2):
    B, S, D = x.shape
    L = ada_w.shape[0]
    x2 = x.reshape(B * S, D)
    c8 = jnp.zeros((8, D), F32).at[:B].set(c)
    mod = _ada_mod(c8, ada_w, ada_b.reshape(L, 1, N_MOD * D))

    layer_params = dict(
        norm1_g=norm1_g, norm2_g=norm2_g, shift_mu=shift_mu, rw_w0=rw_w0, rw_w2=rw_w2,
        rw_a0=rw_a0, rw_a2=rw_a2, rw_g2=rw_g2, rw_k_k=rw_k_k, rw_k_a=rw_k_a, rw_r_k=rw_r_k,
        rw_lnx_g=rw_lnx_g, rw_lnx_b=rw_lnx_b, fox_qn_g=fox_qn_g, fox_kn_g=fox_kn_g, fox_fb=fox_fb,
        sg_ln_g=sg_ln_g, sg_ln_b=sg_ln_b, sg_ws=sg_ws, sg_b=sg_b, w_out=w_out)

    w_t = jnp.transpose(w_in, (2, 0, 1))
    for l in range(L):
        names = ("sh1", "sc1", "g1", "sh2", "sc2", "g2")
        mods = {n: mod[l, :B, i * D:(i + 1) * D].reshape(B, 1, D) for i, n in enumerate(names)}
        p = {n: t[l] for n, t in layer_params.items()}
        dense = l % 2 == 0
        x2, h2 = _mixing_layer(x2, mods, p, w_t, l, B, S, BF16 if dense else F32)
        j = l // 2
        if dense:
            x2 = _ffn(h2, ffn_w1, ffn_w3, ffn_w2, j, x2, mods["g2"], S)
        else:
            rw = jnp.zeros((D, LANE), BF16).at[:, :N_EXPERTS].set(moe_router_w[j].astype(BF16))
            rb = jnp.zeros((1, LANE), F32).at[0, :N_EXPERTS].set(moe_router_b[j])
            x2 = _moe(h2, rw, rb, moe_w1[j], moe_w3[j], moe_w2[j], x2, mods["g2"], S)
    return x2.reshape(B, S, D)
```

```python
import functools

import jax
import jax.numpy as jnp
from jax import lax
from jax.experimental import pallas as pl
from jax.experimental.pallas import tpu as pltpu

F32 = jnp.float32
BF16 = jnp.bfloat16

D_MODEL = 2048
DEPTH = 2
RW_HD = 64
RW_W = 768
RW_H = 12
DECAY_LORA = 64
AAA_LORA = 64
GATE_LORA = 128
FOX_HD = 128
FOX_W = 768
FOX_H = 6
SG_GD = 128
SG_W = 512
SG_G = 4
SG_CHUNK = 128
N_EXPERTS = 8
N_MOD = 6
RMS_EPS = 1e-6
LN_EPS = 1e-5
LNX_EPS = 64e-5

C_R = 0
C_RWKV_END = 3 * RW_W + DECAY_LORA + AAA_LORA + GATE_LORA
C_Q = C_RWKV_END
C_FF = C_Q + 3 * FOX_W
C_U = C_FF + FOX_H
C_SV = C_U + SG_W
N_IN = C_SV + SG_W

P_R, P_K, P_V = 0, 768, 1536
P_LORA = 2304
P_Q, P_FK, P_FV = 2560, 3328, 4096
P_FF = C_FF
P_U = 5120
P_SV = P_U + SG_W
N_PACK = P_SV + SG_W
COL_TILE = 256

LANE = 128
VMEM_LIMIT = 56 * 1024 * 1024


def _cparams(sem):
    return pltpu.CompilerParams(dimension_semantics=sem, vmem_limit_bytes=VMEM_LIMIT)


def _sigmoid(x):
    return 1.0 / (1.0 + jnp.exp(-x))


def _softplus(x):
    return jnp.maximum(x, 0.0) + jnp.log(1.0 + jnp.exp(-jnp.abs(x)))


def _gelu_tanh(x):
    return 0.5 * x * (1.0 + jnp.tanh(0.7978845608028654 * (x + 0.044715 * (x * x * x))))


def _silu(x):
    return x * _sigmoid(x)


def _ada_kernel(c_ref, w_ref, b_ref, o_ref):
    cs = _silu(c_ref[...]).astype(BF16)
    w = w_ref[0].astype(BF16)
    o_ref[0] = jnp.dot(cs, w, preferred_element_type=F32) + b_ref[0]


def _ada_mod(c8, ada_w, ada_b3):
    L, D, N = ada_w.shape
    tn = 1536
    return pl.pallas_call(
        _ada_kernel, name="ada_mod",
        out_shape=jax.ShapeDtypeStruct((L, 8, N), F32),
        grid=(L, N // tn),
        in_specs=[
            pl.BlockSpec((8, D), lambda l, j: (0, 0)),
            pl.BlockSpec((1, D, tn), lambda l, j: (l, 0, j)),
            pl.BlockSpec((1, 1, tn), lambda l, j: (l, 0, j)),
        ],
        out_specs=pl.BlockSpec((1, 8, tn), lambda l, j: (l, 0, j)),
        compiler_params=_cparams(("parallel", "parallel")),
    )(c8, ada_w, ada_b3)


def _col_spec(tm, width, start):
    assert start % width == 0
    return pl.BlockSpec((tm, width), lambda i, *_: (i, start // width))


def _inproj_kernel(x_ref, g_ref, sc_ref, sh_ref, w_hbm, o_ref, h_scr, wbuf, sem, *, layer, tn):
    i, j = pl.program_id(0), pl.program_id(1)
    ni, nj = pl.num_programs(0), pl.num_programs(1)
    step = i * nj + j
    slot = step % 2

    def weight_copy(jj, s):
        first_row = jnp.where(jj < nj - 2, jj * tn, N_IN - (nj - jj) * tn)
        return pltpu.make_async_copy(w_hbm.at[pl.ds(first_row, tn), layer], wbuf.at[s], sem.at[s])

    @pl.when(step == 0)
    def _():
        weight_copy(j, slot).start()

    @pl.when(step + 1 < ni * nj)
    def _():
        weight_copy(jnp.where(j + 1 < nj, j + 1, 0), 1 - slot).start()

    @pl.when(j == 0)
    def _():
        chunk = 512
        for c in range(x_ref.shape[0] // chunk):
            rows = pl.ds(c * chunk, chunk)
            x = x_ref[rows, :]
            ms = jnp.mean(x * x, axis=-1, keepdims=True)
            y = x * lax.rsqrt(ms + RMS_EPS) * g_ref[...]
            h_scr[rows, :] = (y * (1.0 + sc_ref[0]) + sh_ref[0]).astype(BF16)

    weight_copy(j, slot).wait()
    o_ref[...] = lax.dot_general(h_scr[...], wbuf[slot].astype(BF16), (((1,), (1,)), ((), ())),
                                 preferred_element_type=F32)


def _inproj(x2, g, sc, sh, w_t, layer, seq):
    M, D = x2.shape
    tm, tn = min(2048, seq), SG_W
    nj = N_PACK // tn
    assert (nj - 2) * tn == P_U and P_FF + FOX_H <= P_U
    return pl.pallas_call(
        functools.partial(_inproj_kernel, layer=layer, tn=tn), name="inproj",
        out_shape=jax.ShapeDtypeStruct((M, N_PACK), F32),
        grid=(M // tm, nj),
        in_specs=[
            pl.BlockSpec((tm, D), lambda i, j: (i, 0), pipeline_mode=pl.Buffered(1)),
            pl.BlockSpec((1, D), lambda i, j: (0, 0)),
            pl.BlockSpec((1, 1, D), lambda i, j: (i * tm // seq, 0, 0)),
            pl.BlockSpec((1, 1, D), lambda i, j: (i * tm // seq, 0, 0)),
            pl.BlockSpec(memory_space=pl.ANY),
        ],
        out_specs=pl.BlockSpec((tm, tn), lambda i, j: (i, j)),
        scratch_shapes=[pltpu.VMEM((tm, D), BF16), pltpu.VMEM((2, tn, D), F32),
                        pltpu.SemaphoreType.DMA((2,))],
        compiler_params=_cparams(("arbitrary", "arbitrary")),
    )(x2, g, sc, sh, w_t)


def _rwkv_token_quantities(first, zr_ref, zk_ref, zv_ref, zl_ref, pr_ref, pk_ref, pv_ref, pl_ref,
                           mu_ref, mul_ref, w0_ref, w2_ref, a0_ref, a2_ref, g2_ref, kk_ref, ka_ref,
                           r_out, lw_out, k_out, v_out, kk_out, a_out, g_out):
    def shift(z_ref, p_ref, mu):
        z = z_ref[...]
        prev_last = jnp.where(first, 0.0, p_ref[7:8, :])
        zp = pltpu.roll(z, 1, 0)
        row = lax.broadcasted_iota(jnp.int32, z.shape, 0)
        zp = jnp.where(row == 0, prev_last, zp)
        return z + (zp - z) * mu

    r = shift(zr_ref, pr_ref, mu_ref[0:1, :])
    k = shift(zk_ref, pk_ref, mu_ref[1:2, :])
    v = shift(zv_ref, pv_ref, mu_ref[2:3, :])
    lo = shift(zl_ref, pl_ref, mul_ref[...])
    wd = lo[:, 0:DECAY_LORA]
    ad = lo[:, DECAY_LORA:DECAY_LORA + AAA_LORA]
    gd = lo[:, DECAY_LORA + AAA_LORA:]

    dec = w0_ref[...] + jnp.dot(jnp.tanh(wd).astype(BF16), w2_ref[...], preferred_element_type=F32)
    w_log = -_softplus(-dec) - 0.5
    lw = -jnp.exp(w_log)
    a = _sigmoid(a0_ref[...] + jnp.dot(ad.astype(BF16), a2_ref[...], preferred_element_type=F32))
    g = jnp.dot(_sigmoid(gd).astype(BF16), g2_ref[...], preferred_element_type=F32)
    kk = k * kk_ref[...]
    km = k * (1.0 + (a - 1.0) * ka_ref[...])

    for h in range(RW_H):
        sl = slice(h * RW_HD, (h + 1) * RW_HD)
        r_out[h] = r[:, sl]
        lw_out[h] = lw[:, sl]
        k_out[h] = km[:, sl]
        v_out[h] = v[:, sl]
        kk_out[h] = kk[:, sl]
        a_out[h] = a[:, sl]
        g_out[h] = g[:, sl]


def _bmm(a, b, ca, cb):
    return lax.dot_general(a.astype(BF16), b.astype(BF16), (((ca,), (cb,)), ((0,), (0,))),
                           preferred_element_type=F32)


def _split3(x):
    hi = x.astype(BF16)
    r1 = x - hi.astype(F32)
    mid = r1.astype(BF16)
    lo = (r1 - mid.astype(F32)).astype(BF16)
    return hi, mid, lo


def _rwkv_scan_kernel(r_ref, lw_ref, k_ref, v_ref, kk_ref, a_ref, g_ref, rk_ref, lng_ref, lnb_ref,
                      o_ref, s_scr, *, L, nC):
    H, K = RW_H, RW_HD
    n = H * nC

    @pl.when(pl.program_id(1) == 0)
    def _():
        s_scr[...] = jnp.zeros_like(s_scr)

    ld = lambda ref: ref[...].reshape(n, L, K)
    r, lw, k, v, kk, a = ld(r_ref), ld(lw_ref), ld(k_ref), ld(v_ref), ld(kk_ref), ld(a_ref)

    kk = kk / jnp.maximum(jnp.sqrt(jnp.sum(kk * kk, axis=-1, keepdims=True)), 1e-12)
    b = kk * a

    ti = lax.broadcasted_iota(jnp.int32, (L, L), 0)
    si = lax.broadcasted_iota(jnp.int32, (L, L), 1)
    tri_incl = jnp.broadcast_to((ti >= si).astype(BF16), (n, L, L))
    cw = sum(_bmm(tri_incl, piece, 2, 1) for piece in reversed(_split3(lw)))
    cw_last = cw[:, L - 1:L, :]
    e_in = jnp.exp(cw)
    e_out = jnp.exp(-cw)
    e_ex = jnp.exp(cw - lw)
    e_end = jnp.exp(cw_last - cw)
    w_end = jnp.exp(cw_last)

    at = -kk * e_ex
    rt = r * e_in
    bb = b * e_out
    kb = k * e_out
    bh = b * e_end
    kh = k * e_end

    atrt = jnp.concatenate([at, rt], axis=1)
    t2 = lax.broadcasted_iota(jnp.int32, (2 * L, L), 0)
    s2 = lax.broadcasted_iota(jnp.int32, (2 * L, L), 1)
    causal = (((t2 & (L - 1)) - s2 + jnp.where(t2 < L, 0, 1)) > 0)[None]
    pb = jnp.where(causal, _bmm(atrt, bb, 2, 2), 0.0)
    pk = jnp.where(causal, _bmm(atrt, kb, 2, 2), 0.0)
    a_ab, b_rb = pb[:, :L], pb[:, L:]
    a_ak, b_rk = pk[:, :L], pk[:, L:]

    x = jnp.broadcast_to((ti == si).astype(F32), (n, L, L))
    size = 1
    while size < L:
        sh = size.bit_length() - 1
        m = (((ti >> (sh + 1)) == (si >> (sh + 1))) & (((ti >> sh) & 1) == 1) & (((si >> sh) & 1) == 0))[None]
        x = x + _bmm(_bmm(x, jnp.where(m, a_ab, 0.0), 2, 1), x, 2, 1)
        size *= 2

    xu = _bmm(x, jnp.concatenate([at, _bmm(a_ak, v, 2, 1)], axis=-1), 2, 1)
    w = _bmm(b_rb, xu, 2, 1)
    rp = rt + w[..., :K]
    y0 = w[..., K:] + _bmm(b_rk, v, 2, 1)
    big = _bmm(xu, bh, 1, 1)
    gp = big[:, :K]
    cc = big[:, K:] + _bmm(v, kh, 1, 1)

    c4 = lambda t: t.reshape((H, nC) + t.shape[1:])
    rp, y0, gp, cc, w_end = c4(rp), c4(y0), c4(gp), c4(cc), c4(w_end)
    s = s_scr[...]
    ys = []
    for c in range(nC):
        ys.append(_bmm(rp[:, c], s, 2, 2) + y0[:, c])
        s = s * w_end[:, c] + _bmm(s, gp[:, c], 2, 1) + cc[:, c]
    s_scr[...] = s
    y = ys[0] if nC == 1 else jnp.concatenate(ys, axis=1)

    mu = jnp.mean(y, axis=-1, keepdims=True)
    yc = y - mu
    var = jnp.mean(yc * yc, axis=-1, keepdims=True)
    yn = yc * lax.rsqrt(var + LNX_EPS) * lng_ref[...] + lnb_ref[...]
    r3, k3, v3 = r_ref[...], k_ref[...], v_ref[...]
    bonus = jnp.sum(r3 * k3 * rk_ref[...], axis=-1, keepdims=True) * v3
    out = (yn + bonus) * g_ref[...]
    for p in range(H // 2):
        pair = jnp.concatenate([out[2 * p], out[2 * p + 1]], axis=-1)
        o_ref[:, p * LANE:(p + 1) * LANE] = pair.astype(o_ref.dtype)


N_TOKEN_INPUTS = 17


def _rwkv_kernel(*refs, L, nC):
    token_in = refs[:N_TOKEN_INPUTS]
    rk_ref, lng_ref, lnb_ref, o_ref, s_scr = refs[N_TOKEN_INPUTS:N_TOKEN_INPUTS + 5]
    head_major = refs[N_TOKEN_INPUTS + 5:]
    _rwkv_token_quantities(pl.program_id(1) == 0, *token_in, *head_major)
    _rwkv_scan_kernel(*head_major, rk_ref, lng_ref, lnb_ref, o_ref, s_scr, L=L, nC=nC)


def _rwkv(hz, mu3, mul, w0, w2, a0, a2, g2, k_k, k_a, r_k, lnx_g, lnx_b, batch, seq):
    M = hz.shape[0]
    L, nC = 64, 4
    tb = L * nC
    nj = seq // tb
    blk = lambda b, j: b * nj + j
    prev = lambda b, j: jnp.maximum(blk(b, j) * (tb // 8) - 1, 0)
    row = lambda n: pl.BlockSpec((1, n), lambda b, j: (0, 0))
    full = lambda r, c: pl.BlockSpec((r, c), lambda b, j: (0, 0))
    par = pl.BlockSpec((RW_H, 1, RW_HD), lambda b, j: (0, 0, 0))
    in_specs = [
        pl.BlockSpec((tb, RW_W), lambda b, j: (blk(b, j), P_R // RW_W)),
        pl.BlockSpec((tb, RW_W), lambda b, j: (blk(b, j), P_K // RW_W)),
        pl.BlockSpec((tb, RW_W), lambda b, j: (blk(b, j), P_V // RW_W)),
        pl.BlockSpec((tb, COL_TILE), lambda b, j: (blk(b, j), P_LORA // COL_TILE)),
        pl.BlockSpec((8, RW_W), lambda b, j: (prev(b, j), P_R // RW_W)),
        pl.BlockSpec((8, RW_W), lambda b, j: (prev(b, j), P_K // RW_W)),
        pl.BlockSpec((8, RW_W), lambda b, j: (prev(b, j), P_V // RW_W)),
        pl.BlockSpec((8, COL_TILE), lambda b, j: (prev(b, j), P_LORA // COL_TILE)),
        full(3, RW_W), row(COL_TILE), row(RW_W), full(DECAY_LORA, RW_W), row(RW_W),
        full(AAA_LORA, RW_W), full(GATE_LORA, RW_W), row(RW_W), row(RW_W),
        par, par, par,
    ]
    assert len(in_specs) == N_TOKEN_INPUTS + 3
    head_major = pltpu.VMEM((RW_H, tb, RW_HD), F32)
    return pl.pallas_call(
        functools.partial(_rwkv_kernel, L=L, nC=nC), name="rwkv",
        out_shape=jax.ShapeDtypeStruct((M, RW_W), BF16),
        grid=(batch, nj),
        in_specs=in_specs,
        out_specs=pl.BlockSpec((tb, RW_W), lambda b, j: (blk(b, j), 0)),
        scratch_shapes=[pltpu.VMEM((RW_H, RW_HD, RW_HD), F32)] + [head_major] * 7,
        compiler_params=_cparams(("parallel", "arbitrary")),
    )(*([hz] * 8), mu3, mul, w0, w2, a0, a2, g2, k_k, k_a, r_k, lnx_g, lnx_b)


FOX_XD = 2 * FOX_HD


def _fox_prep_kernel(*refs, tm, seq):
    q_refs, k_refs, v_refs = refs[0:3], refs[3:6], refs[6:9]
    f_ref, qg_ref, kg_ref, fb_ref, kx_ref, qxT_ref, vT_ref, carry = refs[9:]

    @pl.when((pl.program_id(0) * tm) % seq == 0)
    def _():
        carry[...] = jnp.zeros_like(carry)

    logf = -_softplus(-(f_ref[...] + fb_ref[...]))
    ti = lax.broadcasted_iota(jnp.int32, (tm, tm), 0)
    si = lax.broadcasted_iota(jnp.int32, (tm, tm), 1)
    tri = (ti >= si).astype(BF16)
    cum = carry[...]
    for piece in reversed(_split3(logf)):
        cum = cum + jnp.dot(tri, piece, preferred_element_type=F32)
    carry[...] = cum[tm - 1:tm, :]
    cum_t = cum.T

    scale = FOX_HD ** -0.5
    lane = lax.broadcasted_iota(jnp.int32, (tm, FOX_HD), 1)
    sub = lax.broadcasted_iota(jnp.int32, (FOX_HD, tm), 0)
    pieces = lambda x: [t.astype(F32) for t in _split3(x)]
    for h in range(FOX_H):
        blk, half = h // 2, slice((h % 2) * FOX_HD, (h % 2 + 1) * FOX_HD)
        q = q_refs[blk][:, half]
        k = k_refs[blk][:, half]
        qn = q * lax.rsqrt(jnp.mean(q * q, axis=-1, keepdims=True) + RMS_EPS) * qg_ref[...]
        kn = k * lax.rsqrt(jnp.mean(k * k, axis=-1, keepdims=True) + RMS_EPS) * kg_ref[...]

        c_hi, c_mid, c_lo = pieces(cum[:, h:h + 1])
        k_extra = jnp.where(lane < 3, 1.0, 0.0)
        k_extra = jnp.where(lane == 3, -c_hi, k_extra)
        k_extra = jnp.where(lane == 4, -c_mid, k_extra)
        k_extra = jnp.where(lane == 5, -c_lo, k_extra)
        kx_ref[:, h * FOX_XD:h * FOX_XD + FOX_HD] = kn.astype(BF16)
        kx_ref[:, h * FOX_XD + FOX_HD:(h + 1) * FOX_XD] = k_extra.astype(BF16)

        r_hi, r_mid, r_lo = pieces(cum_t[h:h + 1, :])
        q_extra = jnp.where(sub < 6, 1.0, 0.0)
        q_extra = jnp.where(sub == 0, r_hi, q_extra)
        q_extra = jnp.where(sub == 1, r_mid, q_extra)
        q_extra = jnp.where(sub == 2, r_lo, q_extra)
        qxT_ref[h * FOX_XD:h * FOX_XD + FOX_HD, :] = (qn * scale).T.astype(BF16)
        qxT_ref[h * FOX_XD + FOX_HD:(h + 1) * FOX_XD, :] = q_extra.astype(BF16)

        vT_ref[h * FOX_HD:(h + 1) * FOX_HD, :] = v_refs[blk][:, half].T.astype(BF16)


def _fox_prep(hz, qn_g, kn_g, fb128, seq):
    M = hz.shape[0]
    tm = 512
    row = pl.BlockSpec((1, LANE), lambda i: (0, 0))
    pieces = lambda start: [_col_spec(tm, COL_TILE, start + c * COL_TILE) for c in range(FOX_W // COL_TILE)]
    return pl.pallas_call(
        functools.partial(_fox_prep_kernel, tm=tm, seq=seq), name="fox_prep",
        out_shape=[jax.ShapeDtypeStruct((M, FOX_H * FOX_XD), BF16),
                   jax.ShapeDtypeStruct((FOX_H * FOX_XD, M), BF16),
                   jax.ShapeDtypeStruct((FOX_W, M), BF16)],
        grid=(M // tm,),
        in_specs=pieces(P_Q) + pieces(P_FK) + pieces(P_FV) + [_col_spec(tm, LANE, P_FF), row, row, row],
        out_specs=[pl.BlockSpec((tm, FOX_H * FOX_XD), lambda i: (i, 0)),
                   pl.BlockSpec((FOX_H * FOX_XD, tm), lambda i: (0, i)),
                   pl.BlockSpec((FOX_W, tm), lambda i: (0, i))],
        scratch_shapes=[pltpu.VMEM((1, LANE), F32)],
        compiler_params=_cparams(("arbitrary",)),
    )(*([hz] * 10), qn_g, kn_g, fb128)


def _fox_attn_kernel(qt_ref, kt_ref, kx_ref, qxT_ref, vT_ref, o_ref, m_scr, l_scr, acc_scr, *, tq, tk):
    step_id = pl.program_id(1)
    qi, ki = qt_ref[step_id], kt_ref[step_id]
    ratio = tq // tk
    first_diag = ratio * qi
    last = ki == first_diag + ratio - 1

    @pl.when(ki == 0)
    def _():
        m_scr[...] = jnp.full_like(m_scr, -1e30)
        l_scr[...] = jnp.zeros_like(l_scr)
        acc_scr[...] = jnp.zeros_like(acc_scr)

    def step(diagonal):
        if diagonal:
            keep = (lax.broadcasted_iota(jnp.int32, (tk, tq), 1) - lax.broadcasted_iota(jnp.int32, (tk, tq), 0)
                    >= ki * tk - qi * tq)
        for h in range(FOX_H):
            xs = slice(h * FOX_XD, (h + 1) * FOX_XD)
            hs = slice(h * FOX_HD, (h + 1) * FOX_HD)
            s = jnp.dot(kx_ref[:, xs], qxT_ref[xs, :], preferred_element_type=F32)
            if diagonal:
                s = jnp.where(keep, s, -1e30)
            m_prev = m_scr[h]
            m_new = jnp.maximum(m_prev, jnp.max(s, axis=0, keepdims=True))
            alpha = jnp.exp(m_prev - m_new)
            p = jnp.exp(s - m_new)
            m_scr[h] = m_new
            l_scr[h] = alpha * l_scr[h] + jnp.sum(p, axis=0, keepdims=True)
            acc_scr[hs, :] = alpha * acc_scr[hs, :] + jnp.dot(vT_ref[hs, :], p.astype(BF16),
                                                              preferred_element_type=F32)

    pl.when(ki < first_diag)(lambda: step(False))
    pl.when(ki >= first_diag)(lambda: step(True))

    @pl.when(last)
    def _():
        for h in range(FOX_H):
            hs = slice(h * FOX_HD, (h + 1) * FOX_HD)
            o_ref[:, hs] = (acc_scr[hs, :] / l_scr[h]).T.astype(o_ref.dtype)


def _fox_attn(kx, qxT, vT, batch, seq):
    M = kx.shape[0]
    tq, tk = min(1024, seq), min(1024, seq)
    nq, nk, ratio = seq // tq, seq // tk, tq // tk
    pairs = [(q, k) for q in range(nq) for k in range(ratio * (q + 1))]
    q_tab = jnp.asarray([q for q, _ in pairs], jnp.int32)
    k_tab = jnp.asarray([k for _, k in pairs], jnp.int32)
    return pl.pallas_call(
        functools.partial(_fox_attn_kernel, tq=tq, tk=tk), name="fox_attn",
        out_shape=jax.ShapeDtypeStruct((M, FOX_W), BF16),
        grid_spec=pltpu.PrefetchScalarGridSpec(
            num_scalar_prefetch=2,
            grid=(batch, len(pairs)),
            in_specs=[
                pl.BlockSpec((tk, FOX_H * FOX_XD), lambda b, s, qt, kt: (b * nk + kt[s], 0)),
                pl.BlockSpec((FOX_H * FOX_XD, tq), lambda b, s, qt, kt: (0, b * nq + qt[s])),
                pl.BlockSpec((FOX_W, tk), lambda b, s, qt, kt: (0, b * nk + kt[s])),
            ],
            out_specs=pl.BlockSpec((tq, FOX_W), lambda b, s, qt, kt: (b * nq + qt[s], 0)),
            scratch_shapes=[pltpu.VMEM((FOX_H, 1, tq), F32), pltpu.VMEM((FOX_H, 1, tq), F32),
                            pltpu.VMEM((FOX_W, tq), F32)],
        ),
        compiler_params=_cparams(("parallel", "arbitrary")),
    )(q_tab, k_tab, kx, qxT, vT)


def _sgu_kernel(u_ref, v_ref, lg_ref, lb_ref, ws_ref, sb_ref, o_ref, *, tm):
    ti = lax.broadcasted_iota(jnp.int32, (SG_CHUNK, SG_CHUNK), 0)
    si = lax.broadcasted_iota(jnp.int32, (SG_CHUNK, SG_CHUNK), 1)
    for g in range(SG_G):
        sl = slice(g * SG_GD, (g + 1) * SG_GD)
        u = _gelu_tanh(u_ref[:, sl])
        v = _gelu_tanh(v_ref[:, sl])
        mu = jnp.mean(v, axis=-1, keepdims=True)
        vc = v - mu
        var = jnp.mean(vc * vc, axis=-1, keepdims=True)
        vn = (vc * lax.rsqrt(var + LN_EPS) * lg_ref[g:g + 1, :] + lb_ref[g:g + 1, :]).astype(BF16)
        ws = jnp.where(ti >= si, ws_ref[g], 0.0).astype(BF16)
        bias = sb_ref[:, g:g + 1]
        for c in range(tm // SG_CHUNK):
            rows = slice(c * SG_CHUNK, (c + 1) * SG_CHUNK)
            mixed = jnp.dot(ws, vn[rows], preferred_element_type=F32) + bias
            o_ref[rows, sl] = (u[rows] * mixed).astype(o_ref.dtype)


def _sgu(hz, ln_g, ln_b, ws, sb_t):
    M = hz.shape[0]
    tm = 512
    return pl.pallas_call(
        functools.partial(_sgu_kernel, tm=tm), name="sgu",
        out_shape=jax.ShapeDtypeStruct((M, SG_W), BF16),
        grid=(M // tm,),
        in_specs=[
            _col_spec(tm, SG_W, P_U), _col_spec(tm, SG_W, P_SV),
            pl.BlockSpec((SG_G, SG_GD), lambda i: (0, 0)),
            pl.BlockSpec((SG_G, SG_GD), lambda i: (0, 0)),
            pl.BlockSpec((SG_G, SG_CHUNK, SG_CHUNK), lambda i: (0, 0, 0)),
            pl.BlockSpec((SG_CHUNK, SG_G), lambda i: (0, 0)),
        ],
        out_specs=pl.BlockSpec((tm, SG_W), lambda i: (i, 0)),
        compiler_params=_cparams(("parallel",)),
    )(hz, hz, ln_g, ln_b, ws, sb_t)


def _outproj_kernel(oa_ref, ob_ref, oc_ref, w_ref, x_ref, g1_ref, ng_ref, sc_ref, sh_ref,
                    xo_ref, h_ref):
    mix = jnp.dot(oa_ref[...], w_ref[0:RW_W, :], preferred_element_type=F32)
    mix += jnp.dot(ob_ref[...], w_ref[RW_W:RW_W + FOX_W, :], preferred_element_type=F32)
    mix += jnp.dot(oc_ref[...], w_ref[RW_W + FOX_W:, :], preferred_element_type=F32)
    x = x_ref[...] + g1_ref[0] * mix
    xo_ref[...] = x
    ms = jnp.mean(x * x, axis=-1, keepdims=True)
    y = x * lax.rsqrt(ms + RMS_EPS) * ng_ref[...]
    h_ref[...] = (y * (1.0 + sc_ref[0]) + sh_ref[0]).astype(h_ref.dtype)


def _outproj(oa, ob, oc, w, x2, g1, ng, sc, sh, seq, h_dtype):
    M, D = x2.shape
    tm = 512
    bidx = lambda i: (i * tm // seq, 0, 0)
    mod = pl.BlockSpec((1, 1, D), bidx)
    return pl.pallas_call(
        _outproj_kernel, name="outproj",
        out_shape=[jax.ShapeDtypeStruct((M, D), F32), jax.ShapeDtypeStruct((M, D), h_dtype)],
        grid=(M // tm,),
        in_specs=[
            pl.BlockSpec((tm, RW_W), lambda i: (i, 0)),
            pl.BlockSpec((tm, FOX_W), lambda i: (i, 0)),
            pl.BlockSpec((tm, SG_W), lambda i: (i, 0)),
            pl.BlockSpec((D, D), lambda i: (0, 0)),
            pl.BlockSpec((tm, D), lambda i: (i, 0)),
            mod,
            pl.BlockSpec((1, D), lambda i: (0, 0)),
            mod, mod,
        ],
        out_specs=[pl.BlockSpec((tm, D), lambda i: (i, 0)), pl.BlockSpec((tm, D), lambda i: (i, 0))],
        compiler_params=_cparams(("parallel",)),
    )(oa, ob, oc, w, x2, g1, ng, sc, sh)


def _swiglu_accumulate(h_ref, w1_ref, w3_ref, w2_ref, acc_ref, rows):
    h = h_ref[0:rows, :]
    a = jnp.dot(h, w1_ref[0].astype(BF16), preferred_element_type=F32)
    b = jnp.dot(h, w3_ref[0].astype(BF16), preferred_element_type=F32)
    acc_ref[0:rows, :] += jnp.dot((_silu(a) * b).astype(BF16), w2_ref[0].astype(BF16),
                                  preferred_element_type=F32)


def _ffn_kernel(h_ref, w1_ref, w3_ref, w2_ref, x_ref, g2_ref, o_ref):
    j = pl.program_id(1)

    @pl.when(j == 0)
    def _():
        o_ref[...] = jnp.zeros_like(o_ref)

    _swiglu_accumulate(h_ref, w1_ref, w3_ref, w2_ref, o_ref, h_ref.shape[0])

    @pl.when(j == pl.num_programs(1) - 1)
    def _():
        o_ref[...] = x_ref[...] + g2_ref[0] * o_ref[...]


def _ffn(h2, w1, w3, w2, layer, x2, g2, seq):
    M, D = x2.shape
    F = w1.shape[2]
    tm, tf = 1024, 512
    once = dict(pipeline_mode=pl.Buffered(1))
    mod = pl.BlockSpec((1, 1, D), lambda i, j: (i * tm // seq, 0, 0))
    return pl.pallas_call(
        _ffn_kernel, name="ffn",
        out_shape=jax.ShapeDtypeStruct((M, D), F32),
        grid=(M // tm, F // tf),
        in_specs=[
            pl.BlockSpec((tm, D), lambda i, j: (i, 0), **once),
            pl.BlockSpec((1, D, tf), lambda i, j: (layer, 0, j)),
            pl.BlockSpec((1, D, tf), lambda i, j: (layer, 0, j)),
            pl.BlockSpec((1, tf, D), lambda i, j: (layer, j, 0)),
            pl.BlockSpec((tm, D), lambda i, j: (i, 0), **once),
            mod,
        ],
        out_specs=pl.BlockSpec((tm, D), lambda i, j: (i, 0), **once),
        compiler_params=_cparams(("parallel", "arbitrary")),
    )(h2, w1, w3, w2, x2, g2)


def _router_kernel(h_ref, w_ref, b_ref, info_ref, cnt_ref, carry, *, tm):
    @pl.when(pl.program_id(0) == 0)
    def _():
        carry[...] = jnp.zeros_like(carry)

    logits = jnp.dot(h_ref[...].astype(BF16), w_ref[...], preferred_element_type=F32) + b_ref[...]
    lane = lax.broadcasted_iota(jnp.int32, logits.shape, 1)
    neg = -1e30
    logits = jnp.where(lane < N_EXPERTS, logits, neg)
    m1 = jnp.max(logits, axis=-1, keepdims=True)
    i1 = jnp.min(jnp.where(logits == m1, lane, LANE), axis=-1, keepdims=True)
    rest = jnp.where(lane == i1, neg, logits)
    m2 = jnp.max(rest, axis=-1, keepdims=True)
    i2 = jnp.min(jnp.where(rest == m2, lane, LANE), axis=-1, keepdims=True)
    e2 = jnp.exp(m2 - m1)
    p1 = 1.0 / (1.0 + e2)
    p2 = e2 / (1.0 + e2)

    oh1 = (lane == i1).astype(F32)
    oh2 = (lane == i2).astype(F32)
    both = oh1 + oh2
    ti = lax.broadcasted_iota(jnp.int32, (tm, tm), 0)
    si = lax.broadcasted_iota(jnp.int32, (tm, tm), 1)
    strict = (ti > si).astype(BF16)
    before = jnp.dot(strict, both.astype(BF16), preferred_element_type=F32) + carry[...]
    rank1 = jnp.sum(oh1 * before, axis=-1, keepdims=True)
    rank2 = jnp.sum(oh2 * before, axis=-1, keepdims=True)
    total = carry[...] + jnp.sum(both, axis=0, keepdims=True)
    carry[...] = total
    cnt_ref[...] = total

    info = jnp.where(lane == 0, i1.astype(F32), 0.0)
    info = jnp.where(lane == 1, i2.astype(F32), info)
    info = jnp.where(lane == 2, rank1, info)
    info = jnp.where(lane == 3, rank2, info)
    info = jnp.where(lane == 4, p1, info)
    info = jnp.where(lane == 5, p2, info)
    info_ref[...] = info


def _router(h2, rw, rb):
    M, D = h2.shape
    tm = 512
    return pl.pallas_call(
        functools.partial(_router_kernel, tm=tm), name="router",
        out_shape=[jax.ShapeDtypeStruct((M, LANE), F32), jax.ShapeDtypeStruct((1, LANE), F32)],
        grid=(M // tm,),
        in_specs=[
            pl.BlockSpec((tm, D), lambda i: (i, 0)),
            pl.BlockSpec((D, LANE), lambda i: (0, 0)),
            pl.BlockSpec((1, LANE), lambda i: (0, 0)),
        ],
        out_specs=[pl.BlockSpec((tm, LANE), lambda i: (i, 0)), pl.BlockSpec((1, LANE), lambda i: (0, 0))],
        scratch_shapes=[pltpu.VMEM((1, LANE), F32)],
        compiler_params=_cparams(("arbitrary",)),
    )(h2, rw, rb)


MOE_TILE = 1024


def _moe_dispatch_kernel(s1_ref, s2_ref, h_ref, xs_in_ref, xs_ref, sem, *, tm):
    del xs_in_ref
    base = pl.program_id(0) * tm

    def copies(r):
        src = h_ref.at[pl.ds(r, 1)]
        return (pltpu.make_async_copy(src, xs_ref.at[pl.ds(s1_ref[base + r], 1)], sem.at[0]),
                pltpu.make_async_copy(src, xs_ref.at[pl.ds(s2_ref[base + r], 1)], sem.at[1]))

    def start(r, carry):
        for cp in copies(r):
            cp.start()
        return carry

    def wait(r, carry):
        for cp in copies(r):
            cp.wait()
        return carry

    lax.fori_loop(0, tm, start, 0, unroll=8)
    lax.fori_loop(0, tm, wait, 0, unroll=8)


def _moe_dispatch(slot1, slot2, h2, n_rows):
    M, D = h2.shape
    tm = 256
    xs0 = jnp.zeros((n_rows, D), F32)
    return pl.pallas_call(
        functools.partial(_moe_dispatch_kernel, tm=tm), name="moe_dispatch",
        out_shape=jax.ShapeDtypeStruct((n_rows, D), F32),
        grid_spec=pltpu.PrefetchScalarGridSpec(
            num_scalar_prefetch=2,
            grid=(M // tm,),
            in_specs=[pl.BlockSpec((tm, D), lambda i, s1, s2: (i, 0)),
                      pl.BlockSpec(memory_space=pl.ANY)],
            out_specs=pl.BlockSpec(memory_space=pl.ANY),
            scratch_shapes=[pltpu.SemaphoreType.DMA((2,))],
        ),
        input_output_aliases={3: 0},
        compiler_params=_cparams(("arbitrary",)),
    )(slot1, slot2, h2, xs0)


def _moe_expert_kernel(te_ref, nv_ref, xs_ref, w1_ref, w3_ref, w2_ref, ys_ref, xb):
    j = pl.program_id(1)
    nv = nv_ref[pl.program_id(0)]
    half = xb.shape[0] // 2

    @pl.when(j == 0)
    def _():
        ys_ref[...] = jnp.zeros_like(ys_ref)
        xb[...] = xs_ref[...].astype(BF16)

    @pl.when(nv > half)
    def _():
        _swiglu_accumulate(xb, w1_ref, w3_ref, w2_ref, ys_ref, 2 * half)

    @pl.when((nv > 0) & (nv <= half))
    def _():
        _swiglu_accumulate(xb, w1_ref, w3_ref, w2_ref, ys_ref, half)


def _moe_experts(tile_expert, n_valid, xs, w1, w3, w2):
    P, D = xs.shape
    E, _, F = w1.shape
    tm, tf = MOE_TILE, 256
    nf = F // tf
    fj = lambda i, j, nv: jnp.where(nv[i] > 0, j, nf - 1)
    return pl.pallas_call(
        _moe_expert_kernel, name="moe_experts",
        out_shape=jax.ShapeDtypeStruct((P, D), F32),
        grid_spec=pltpu.PrefetchScalarGridSpec(
            num_scalar_prefetch=2,
            grid=(P // tm, nf),
            in_specs=[
                pl.BlockSpec((tm, D), lambda i, j, te, nv: (i, 0)),
                pl.BlockSpec((1, D, tf), lambda i, j, te, nv: (te[i], 0, fj(i, j, nv))),
                pl.BlockSpec((1, D, tf), lambda i, j, te, nv: (te[i], 0, fj(i, j, nv))),
                pl.BlockSpec((1, tf, D), lambda i, j, te, nv: (te[i], fj(i, j, nv), 0)),
            ],
            out_specs=pl.BlockSpec((tm, D), lambda i, j, te, nv: (i, 0)),
            scratch_shapes=[pltpu.VMEM((tm, D), BF16)],
        ),
        compiler_params=_cparams(("arbitrary", "arbitrary")),
    )(tile_expert, n_valid, xs, w1, w3, w2)


def _moe_combine_kernel(s1_ref, s2_ref, ys_ref, info_ref, x_ref, g2_ref, o_ref, a_buf, b_buf, sem, *, tm):
    base = pl.program_id(0) * tm

    def copies(r):
        return (pltpu.make_async_copy(ys_ref.at[pl.ds(s1_ref[base + r], 1)], a_buf.at[pl.ds(r, 1)], sem.at[0]),
                pltpu.make_async_copy(ys_ref.at[pl.ds(s2_ref[base + r], 1)], b_buf.at[pl.ds(r, 1)], sem.at[1]))

    def start(r, carry):
        for cp in copies(r):
            cp.start()
        return carry

    def wait(r, carry):
        for cp in copies(r):
            cp.wait()
        return carry

    lax.fori_loop(0, tm, start, 0, unroll=8)
    lax.fori_loop(0, tm, wait, 0, unroll=8)
    info = info_ref[...]
    p1 = info[:, 4:5]
    p2 = info[:, 5:6]
    o_ref[...] = x_ref[...] + g2_ref[0] * (p1 * a_buf[...] + p2 * b_buf[...])


def _moe_combine(slot1, slot2, ys, info, x2, g2, seq):
    M, D = x2.shape
    tm = 256
    return pl.pallas_call(
        functools.partial(_moe_combine_kernel, tm=tm), name="moe_combine",
        out_shape=jax.ShapeDtypeStruct((M, D), F32),
        grid_spec=pltpu.PrefetchScalarGridSpec(
            num_scalar_prefetch=2,
            grid=(M // tm,),
            in_specs=[
                pl.BlockSpec(memory_space=pl.ANY),
                pl.BlockSpec((tm, LANE), lambda i, s1, s2: (i, 0)),
                pl.BlockSpec((tm, D), lambda i, s1, s2: (i, 0)),
                pl.BlockSpec((1, 1, D), lambda i, s1, s2: (i * tm // seq, 0, 0)),
            ],
            out_specs=pl.BlockSpec((tm, D), lambda i, s1, s2: (i, 0)),
            scratch_shapes=[pltpu.VMEM((tm, D), F32), pltpu.VMEM((tm, D), F32),
                            pltpu.SemaphoreType.DMA((2,))],
        ),
        compiler_params=_cparams(("arbitrary",)),
    )(slot1, slot2, ys, info, x2, g2)


def _moe(h2, rw, rb, w1, w3, w2, x2, g2, seq):
    M, D = x2.shape
    E = w1.shape[0]
    T = MOE_TILE
    n_rows = 2 * M + E * T
    n_tiles = n_rows // T
    info, counts = _router(h2, rw, rb)

    e1, e2, rank1, rank2 = (info[:, c].astype(jnp.int32) for c in range(4))
    cnt = counts[0, :E].astype(jnp.int32)
    padded = (cnt + T - 1) // T * T
    ends = jnp.cumsum(padded)
    off = ends - padded
    expert_ids = jnp.arange(E, dtype=jnp.int32)
    offset_of = lambda e: jnp.sum(jnp.where(e[:, None] == expert_ids[None, :], off[None, :], 0), axis=1)
    slot1 = offset_of(e1) + rank1
    slot2 = offset_of(e2) + rank2
    tile_start = jnp.arange(n_tiles, dtype=jnp.int32) * T
    in_use = tile_start < ends[E - 1]
    clamped = jnp.minimum(tile_start, ends[E - 1] - T)
    tile_expert = jnp.sum(clamped[:, None] >= ends[None, :], axis=1).astype(jnp.int32)
    is_e = tile_expert[:, None] == expert_ids[None, :]
    tokens_end = jnp.sum(jnp.where(is_e, (off + cnt)[None, :], 0), axis=1)
    n_valid = jnp.where(in_use, jnp.clip(tokens_end - tile_start, 0, T), 0).astype(jnp.int32)

    xs = _moe_dispatch(slot1, slot2, h2, n_rows)
    ys = _moe_experts(tile_expert, n_valid, xs, w1, w3, w2)
    return _moe_combine(slot1, slot2, ys, info, x2, g2, seq)


def _mixing_layer(x2, mods, p, w_t, layer, batch, seq, h_dtype):
    row = lambda t: t.reshape(1, -1)
    hz = _inproj(x2, row(p["norm1_g"]), mods["sc1"], mods["sh1"], w_t, layer, seq)

    mu = p["shift_mu"]
    mu3 = mu[:3 * RW_W].reshape(3, RW_W)
    mul = row(mu[3 * RW_W:])
    hm = lambda t: t.reshape(RW_H, 1, RW_HD)
    o_a = _rwkv(hz, mu3, mul, row(p["rw_w0"]), p["rw_w2"].astype(BF16), row(p["rw_a0"]),
                p["rw_a2"].astype(BF16), p["rw_g2"].astype(BF16), row(p["rw_k_k"]), row(p["rw_k_a"]),
                hm(p["rw_r_k"]), hm(p["rw_lnx_g"]), hm(p["rw_lnx_b"]), batch, seq)

    fb128 = jnp.zeros((1, LANE), F32).at[0, :FOX_H].set(p["fox_fb"])
    kx, qxT, vT = _fox_prep(hz, row(p["fox_qn_g"]), row(p["fox_kn_g"]), fb128, seq)
    o_b = _fox_attn(kx, qxT, vT, batch, seq)

    o_c = _sgu(hz, p["sg_ln_g"], p["sg_ln_b"], p["sg_ws"], jnp.transpose(p["sg_b"]))

    return _outproj(o_a, o_b, o_c, p["w_out"].astype(BF16), x2, mods["g1"], row(p["norm2_g"]),
                    mods["sc2"], mods["sh2"], seq, h_dtype)


def kernel(x, c, ada_w, ada_b, norm1_g, norm2_g, w_in, shift_mu, rw_w0, rw_w2, rw_a0, rw_a2, rw_g2, rw_k_k, rw_k_a, rw_r_k, rw_lnx_g, rw_lnx_b, fox_qn_g, fox_kn_g, fox_fb, sg_ln_g, sg_ln_b, sg_ws, sg_b, w_out, ffn_w1, ffn_w3, ffn_w2, moe_router_w, moe_router_b, moe_w1, moe_w3, moe_w2):
    B, S, D = x.shape
    L = ada_w.shape[0]
    x2 = x.reshape(B * S, D)
    c8 = jnp.zeros((8, D), F32).at[:B].set(c)
    mod = _ada_mod(c8, ada_w, ada_b.reshape(L, 1, N_MOD * D))

    layer_params = dict(
        norm1_g=norm1_g, norm2_g=norm2_g, shift_mu=shift_mu, rw_w0=rw_w0, rw_w2=rw_w2,
        rw_a0=rw_a0, rw_a2=rw_a2, rw_g2=rw_g2, rw_k_k=rw_k_k, rw_k_a=rw_k_a, rw_r_k=rw_r_k,
        rw_lnx_g=rw_lnx_g, rw_lnx_b=rw_lnx_b, fox_qn_g=fox_qn_g, fox_kn_g=fox_kn_g, fox_fb=fox_fb,
        sg_ln_g=sg_ln_g, sg_ln_b=sg_ln_b, sg_ws=sg_ws, sg_b=sg_b, w_out=w_out)

    w_t = jnp.transpose(w_in, (2, 0, 1))
    for l in range(L):
        names = ("sh1", "sc1", "g1", "sh2", "sc2", "g2")
        mods = {n: mod[l, :B, i * D:(i + 1) * D].reshape(B, 1, D) for i, n in enumerate(names)}
        p = {n: t[l] for n, t in layer_params.items()}
        dense = l % 2 == 0
        x2, h2 = _mixing_layer(x2, mods, p, w_t, l, B, S, BF16 if dense else F32)
        j = l // 2
        if dense:
            x2 = _ffn(h2, ffn_w1, ffn_w3, ffn_w2, j, x2, mods["g2"], S)
        else:
            rw = jnp.zeros((D, LANE), BF16).at[:, :N_EXPERTS].set(moe_router_w[j].astype(BF16))
            rb = jnp.zeros((1, LANE), F32).at[0, :N_EXPERTS].set(moe_router_b[j])
            x2 = _moe(h2, rw, rb, moe_w1[j], moe_w3[j], moe_w2[j], x2, mods["g2"], S)
    return x2.reshape(B, S, D)
```

```python
import functools

import jax
import jax.numpy as jnp
from jax import lax
from jax.experimental import pallas as pl
from jax.experimental.pallas import tpu as pltpu

F32 = jnp.float32
BF16 = jnp.bfloat16

D_MODEL = 2048
DEPTH = 2
RW_HD = 64
RW_W = 768
RW_H = 12
DECAY_LORA = 64
AAA_LORA = 64
GATE_LORA = 128
FOX_HD = 128
FOX_W = 768
FOX_H = 6
SG_GD = 128
SG_W = 512
SG_G = 4
SG_CHUNK = 128
N_EXPERTS = 8
N_MOD = 6
RMS_EPS = 1e-6
LN_EPS = 1e-5
LNX_EPS = 64e-5

C_R = 0
C_RWKV_END = 3 * RW_W + DECAY_LORA + AAA_LORA + GATE_LORA
C_Q = C_RWKV_END
C_FF = C_Q + 3 * FOX_W
C_U = C_FF + FOX_H
C_SV = C_U + SG_W
N_IN = C_SV + SG_W

P_R, P_K, P_V = 0, 768, 1536
P_LORA = 2304
P_Q, P_FK, P_FV = 2560, 3328, 4096
P_FF = C_FF
P_U = 5120
P_SV = P_U + SG_W
N_PACK = P_SV + SG_W
COL_TILE = 256

LANE = 128
VMEM_LIMIT = 56 * 1024 * 1024


def _cparams(sem):
    return pltpu.CompilerParams(dimension_semantics=sem, vmem_limit_bytes=VMEM_LIMIT)


def _sigmoid(x):
    return 1.0 / (1.0 + jnp.exp(-x))


def _softplus(x):
    return jnp.maximum(x, 0.0) + jnp.log(1.0 + jnp.exp(-jnp.abs(x)))


def _gelu_tanh(x):
    return 0.5 * x * (1.0 + jnp.tanh(0.7978845608028654 * (x + 0.044715 * (x * x * x))))


def _silu(x):
    return x * _sigmoid(x)


def _ada_kernel(c_ref, w_ref, b_ref, o_ref):
    cs = _silu(c_ref[...]).astype(BF16)
    w = w_ref[0].astype(BF16)
    o_ref[0] = jnp.dot(cs, w, preferred_element_type=F32) + b_ref[0]


def _ada_mod(c8, ada_w, ada_b3):
    L, D, N = ada_w.shape
    tn = 1536
    return pl.pallas_call(
        _ada_kernel, name="ada_mod",
        out_shape=jax.ShapeDtypeStruct((L, 8, N), F32),
        grid=(L, N // tn),
        in_specs=[
            pl.BlockSpec((8, D), lambda l, j: (0, 0)),
            pl.BlockSpec((1, D, tn), lambda l, j: (l, 0, j)),
            pl.BlockSpec((1, 1, tn), lambda l, j: (l, 0, j)),
        ],
        out_specs=pl.BlockSpec((1, 8, tn), lambda l, j: (l, 0, j)),
        compiler_params=_cparams(("parallel", "parallel")),
    )(c8, ada_w, ada_b3)


def _col_spec(tm, width, start):
    assert start % width == 0
    return pl.BlockSpec((tm, width), lambda i, *_: (i, start // width))


def _inproj_kernel(x_ref, g_ref, sc_ref, sh_ref, w_hbm, o_ref, h_scr, wbuf, sem, *, layer, tn):
    i, j = pl.program_id(0), pl.program_id(1)
    ni, nj = pl.num_programs(0), pl.num_programs(1)
    step = i * nj + j
    slot = step % 2

    def weight_copy(jj, s):
        first_row = jnp.where(jj < nj - 2, jj * tn, N_IN - (nj - jj) * tn)
        return pltpu.make_async_copy(w_hbm.at[pl.ds(first_row, tn), layer], wbuf.at[s], sem.at[s])

    @pl.when(step == 0)
    def _():
        weight_copy(j, slot).start()

    @pl.when(step + 1 < ni * nj)
    def _():
        weight_copy(jnp.where(j + 1 < nj, j + 1, 0), 1 - slot).start()

    @pl.when(j == 0)
    def _():
        chunk = 512
        for c in range(x_ref.shape[0] // chunk):
            rows = pl.ds(c * chunk, chunk)
            x = x_ref[rows, :]
            ms = jnp.mean(x * x, axis=-1, keepdims=True)
            y = x * lax.rsqrt(ms + RMS_EPS) * g_ref[...]
            h_scr[rows, :] = (y * (1.0 + sc_ref[0]) + sh_ref[0]).astype(BF16)

    weight_copy(j, slot).wait()
    o_ref[...] = lax.dot_general(h_scr[...], wbuf[slot].astype(BF16), (((1,), (1,)), ((), ())),
                                 preferred_element_type=F32)


def _inproj(x2, g, sc, sh, w_t, layer, seq):
    M, D = x2.shape
    tm, tn = min(2048, seq), SG_W
    nj = N_PACK // tn
    assert (nj - 2) * tn == P_U and P_FF + FOX_H <= P_U
    return pl.pallas_call(
        functools.partial(_inproj_kernel, layer=layer, tn=tn), name="inproj",
        out_shape=jax.ShapeDtypeStruct((M, N_PACK), F32),
        grid=(M // tm, nj),
        in_specs=[
            pl.BlockSpec((tm, D), lambda i, j: (i, 0), pipeline_mode=pl.Buffered(1)),
            pl.BlockSpec((1, D), lambda i, j: (0, 0)),
            pl.BlockSpec((1, 1, D), lambda i, j: (i * tm // seq, 0, 0)),
            pl.BlockSpec((1, 1, D), lambda i, j: (i * tm // seq, 0, 0)),
            pl.BlockSpec(memory_space=pl.ANY),
        ],
        out_specs=pl.BlockSpec((tm, tn), lambda i, j: (i, j)),
        scratch_shapes=[pltpu.VMEM((tm, D), BF16), pltpu.VMEM((2, tn, D), F32),
                        pltpu.SemaphoreType.DMA((2,))],
        compiler_params=_cparams(("arbitrary", "arbitrary")),
    )(x2, g, sc, sh, w_t)


def _rwkv_token_quantities(first, zr_ref, zk_ref, zv_ref, zl_ref, pr_ref, pk_ref, pv_ref, pl_ref,
                           mu_ref, mul_ref, w0_ref, w2_ref, a0_ref, a2_ref, g2_ref, kk_ref, ka_ref,
                           r_out, lw_out, k_out, v_out, kk_out, a_out, g_out):
    def shift(z_ref, p_ref, mu):
        z = z_ref[...]
        prev_last = jnp.where(first, 0.0, p_ref[7:8, :])
        zp = pltpu.roll(z, 1, 0)
        row = lax.broadcasted_iota(jnp.int32, z.shape, 0)
        zp = jnp.where(row == 0, prev_last, zp)
        return z + (zp - z) * mu

    r = shift(zr_ref, pr_ref, mu_ref[0:1, :])
    k = shift(zk_ref, pk_ref, mu_ref[1:2, :])
    v = shift(zv_ref, pv_ref, mu_ref[2:3, :])
    lo = shift(zl_ref, pl_ref, mul_ref[...])
    wd = lo[:, 0:DECAY_LORA]
    ad = lo[:, DECAY_LORA:DECAY_LORA + AAA_LORA]
    gd = lo[:, DECAY_LORA + AAA_LORA:]

    dec = w0_ref[...] + jnp.dot(jnp.tanh(wd).astype(BF16), w2_ref[...], preferred_element_type=F32)
    w_log = -_softplus(-dec) - 0.5
    lw = -jnp.exp(w_log)
    a = _sigmoid(a0_ref[...] + jnp.dot(ad.astype(BF16), a2_ref[...], preferred_element_type=F32))
    g = jnp.dot(_sigmoid(gd).astype(BF16), g2_ref[...], preferred_element_type=F32)
    kk = k * kk_ref[...]
    km = k * (1.0 + (a - 1.0) * ka_ref[...])

    for h in range(RW_H):
        sl = slice(h * RW_HD, (h + 1) * RW_HD)
        r_out[h] = r[:, sl]
        lw_out[h] = lw[:, sl]
        k_out[h] = km[:, sl]
        v_out[h] = v[:, sl]
        kk_out[h] = kk[:, sl]
        a_out[h] = a[:, sl]
        g_out[h] = g[:, sl]


def _bmm(a, b, ca, cb):
    return lax.dot_general(a.astype(BF16), b.astype(BF16), (((ca,), (cb,)), ((0,), (0,))),
                           preferred_element_type=F32)


def _split3(x):
    hi = x.astype(BF16)
    r1 = x - hi.astype(F32)
    mid = r1.astype(BF16)
    lo = (r1 - mid.astype(F32)).astype(BF16)
    return hi, mid, lo


def _rwkv_scan_kernel(r_ref, lw_ref, k_ref, v_ref, kk_ref, a_ref, g_ref, rk_ref, lng_ref, lnb_ref,
                      o_ref, s_scr, *, L, nC):
    H, K = RW_H, RW_HD
    n = H * nC

    @pl.when(pl.program_id(1) == 0)
    def _():
        s_scr[...] = jnp.zeros_like(s_scr)

    ld = lambda ref: ref[...].reshape(n, L, K)
    r, lw, k, v, kk, a = ld(r_ref), ld(lw_ref), ld(k_ref), ld(v_ref), ld(kk_ref), ld(a_ref)

    kk = kk / jnp.maximum(jnp.sqrt(jnp.sum(kk * kk, axis=-1, keepdims=True)), 1e-12)
    b = kk * a

    ti = lax.broadcasted_iota(jnp.int32, (L, L), 0)
    si = lax.broadcasted_iota(jnp.int32, (L, L), 1)
    tri_incl = jnp.broadcast_to((ti >= si).astype(BF16), (n, L, L))
    cw = sum(_bmm(tri_incl, piece, 2, 1) for piece in reversed(_split3(lw)))
    cw_last = cw[:, L - 1:L, :]
    e_in = jnp.exp(cw)
    e_out = jnp.exp(-cw)
    e_ex = jnp.exp(cw - lw)
    e_end = jnp.exp(cw_last - cw)
    w_end = jnp.exp(cw_last)

    at = -kk * e_ex
    rt = r * e_in
    bb = b * e_out
    kb = k * e_out
    bh = b * e_end
    kh = k * e_end

    atrt = jnp.concatenate([at, rt], axis=1)
    t2 = lax.broadcasted_iota(jnp.int32, (2 * L, L), 0)
    s2 = lax.broadcasted_iota(jnp.int32, (2 * L, L), 1)
    causal = (((t2 & (L - 1)) - s2 + jnp.where(t2 < L, 0, 1)) > 0)[None]
    pb = jnp.where(causal, _bmm(atrt, bb, 2, 2), 0.0)
    pk = jnp.where(causal, _bmm(atrt, kb, 2, 2), 0.0)
    a_ab, b_rb = pb[:, :L], pb[:, L:]
    a_ak, b_rk = pk[:, :L], pk[:, L:]

    x = jnp.broadcast_to((ti == si).astype(F32), (n, L, L))
    size = 1
    while size < L:
        sh = size.bit_length() - 1
        m = (((ti >> (sh + 1)) == (si >> (sh + 1))) & (((ti >> sh) & 1) == 1) & (((si >> sh) & 1) == 0))[None]
        x = x + _bmm(_bmm(x, jnp.where(m, a_ab, 0.0), 2, 1), x, 2, 1)
        size *= 2

    xu = _bmm(x, jnp.concatenate([at, _bmm(a_ak, v, 2, 1)], axis=-1), 2, 1)
    w = _bmm(b_rb, xu, 2, 1)
    rp = rt + w[..., :K]
    y0 = w[..., K:] + _bmm(b_rk, v, 2, 1)
    big = _bmm(xu, bh, 1, 1)
    gp = big[:, :K]
    cc = big[:, K:] + _bmm(v, kh, 1, 1)

    c4 = lambda t: t.reshape((H, nC) + t.shape[1:])
    rp, y0, gp, cc, w_end = c4(rp), c4(y0), c4(gp), c4(cc), c4(w_end)
    s = s_scr[...]
    ys = []
    for c in range(nC):
        ys.append(_bmm(rp[:, c], s, 2, 2) + y0[:, c])
        s = s * w_end[:, c] + _bmm(s, gp[:, c], 2, 1) + cc[:, c]
    s_scr[...] = s
    y = ys[0] if nC == 1 else jnp.concatenate(ys, axis=1)

    mu = jnp.mean(y, axis=-1, keepdims=True)
    yc = y - mu
    var = jnp.mean(yc * yc, axis=-1, keepdims=True)
    yn = yc * lax.rsqrt(var + LNX_EPS) * lng_ref[...] + lnb_ref[...]
    r3, k3, v3 = r_ref[...], k_ref[...], v_ref[...]
    bonus = jnp.sum(r3 * k3 * rk_ref[...], axis=-1, keepdims=True) * v3
    out = (yn + bonus) * g_ref[...]
    for p in range(H // 2):
        pair = jnp.concatenate([out[2 * p], out[2 * p + 1]], axis=-1)
        o_ref[:, p * LANE:(p + 1) * LANE] = pair.astype(o_ref.dtype)


N_TOKEN_INPUTS = 17


def _rwkv_kernel(*refs, L, nC):
    token_in = refs[:N_TOKEN_INPUTS]
    rk_ref, lng_ref, lnb_ref, o_ref, s_scr = refs[N_TOKEN_INPUTS:N_TOKEN_INPUTS + 5]
    head_major = refs[N_TOKEN_INPUTS + 5:]
    _rwkv_token_quantities(pl.program_id(1) == 0, *token_in, *head_major)
    _rwkv_scan_kernel(*head_major, rk_ref, lng_ref, lnb_ref, o_ref, s_scr, L=L, nC=nC)


def _rwkv(hz, mu3, mul, w0, w2, a0, a2, g2, k_k, k_a, r_k, lnx_g, lnx_b, batch, seq):
    M = hz.shape[0]
    L, nC = 64, 4
    tb = L * nC
    nj = seq // tb
    blk = lambda b, j: b * nj + j
    prev = lambda b, j: jnp.maximum(blk(b, j) * (tb // 8) - 1, 0)
    row = lambda n: pl.BlockSpec((1, n), lambda b, j: (0, 0))
    full = lambda r, c: pl.BlockSpec((r, c), lambda b, j: (0, 0))
    par = pl.BlockSpec((RW_H, 1, RW_HD), lambda b, j: (0, 0, 0))
    in_specs = [
        pl.BlockSpec((tb, RW_W), lambda b, j: (blk(b, j), P_R // RW_W)),
        pl.BlockSpec((tb, RW_W), lambda b, j: (blk(b, j), P_K // RW_W)),
        pl.BlockSpec((tb, RW_W), lambda b, j: (blk(b, j), P_V // RW_W)),
        pl.BlockSpec((tb, COL_TILE), lambda b, j: (blk(b, j), P_LORA // COL_TILE)),
        pl.BlockSpec((8, RW_W), lambda b, j: (prev(b, j), P_R // RW_W)),
        pl.BlockSpec((8, RW_W), lambda b, j: (prev(b, j), P_K // RW_W)),
        pl.BlockSpec((8, RW_W), lambda b, j: (prev(b, j), P_V // RW_W)),
        pl.BlockSpec((8, COL_TILE), lambda b, j: (prev(b, j), P_LORA // COL_TILE)),
        full(3, RW_W), row(COL_TILE), row(RW_W), full(DECAY_LORA, RW_W), row(RW_W),
        full(AAA_LORA, RW_W), full(GATE_LORA, RW_W), row(RW_W), row(RW_W),
        par, par, par,
    ]
    assert len(in_specs) == N_TOKEN_INPUTS + 3
    head_major = pltpu.VMEM((RW_H, tb, RW_HD), F32)
    return pl.pallas_call(
        functools.partial(_rwkv_kernel, L=L, nC=nC), name="rwkv",
        out_shape=jax.ShapeDtypeStruct((M, RW_W), BF16),
        grid=(batch, nj),
        in_specs=in_specs,
        out_specs=pl.BlockSpec((tb, RW_W), lambda b, j: (blk(b, j), 0)),
        scratch_shapes=[pltpu.VMEM((RW_H, RW_HD, RW_HD), F32)] + [head_major] * 7,
        compiler_params=_cparams(("parallel", "arbitrary")),
    )(*([hz] * 8), mu3, mul, w0, w2, a0, a2, g2, k_k, k_a, r_k, lnx_g, lnx_b)


FOX_XD = 2 * FOX_HD


def _fox_prep_kernel(*refs, tm, seq):
    q_refs, k_refs, v_refs = refs[0:3], refs[3:6], refs[6:9]
    f_ref, qg_ref, kg_ref, fb_ref, kx_ref, qxT_ref, vT_ref, carry = refs[9:]

    @pl.when((pl.program_id(0) * tm) % seq == 0)
    def _():
        carry[...] = jnp.zeros_like(carry)

    logf = -_softplus(-(f_ref[...] + fb_ref[...]))
    ti = lax.broadcasted_iota(jnp.int32, (tm, tm), 0)
    si = lax.broadcasted_iota(jnp.int32, (tm, tm), 1)
    tri = (ti >= si).astype(BF16)
    cum = carry[...]
    for piece in reversed(_split3(logf)):
        cum = cum + jnp.dot(tri, piece, preferred_element_type=F32)
    carry[...] = cum[tm - 1:tm, :]
    cum_t = cum.T

    scale = FOX_HD ** -0.5
    lane = lax.broadcasted_iota(jnp.int32, (tm, FOX_HD), 1)
    sub = lax.broadcasted_iota(jnp.int32, (FOX_HD, tm), 0)
    pieces = lambda x: [t.astype(F32) for t in _split3(x)]
    for h in range(FOX_H):
        blk, half = h // 2, slice((h % 2) * FOX_HD, (h % 2 + 1) * FOX_HD)
        q = q_refs[blk][:, half]
        k = k_refs[blk][:, half]
        qn = q * lax.rsqrt(jnp.mean(q * q, axis=-1, keepdims=True) + RMS_EPS) * qg_ref[...]
        kn = k * lax.rsqrt(jnp.mean(k * k, axis=-1, keepdims=True) + RMS_EPS) * kg_ref[...]

        c_hi, c_mid, c_lo = pieces(cum[:, h:h + 1])
        k_extra = jnp.where(lane < 3, 1.0, 0.0)
        k_extra = jnp.where(lane == 3, -c_hi, k_extra)
        k_extra = jnp.where(lane == 4, -c_mid, k_extra)
        k_extra = jnp.where(lane == 5, -c_lo, k_extra)
        kx_ref[:, h * FOX_XD:h * FOX_XD + FOX_HD] = kn.astype(BF16)
        kx_ref[:, h * FOX_XD + FOX_HD:(h + 1) * FOX_XD] = k_extra.astype(BF16)

        r_hi, r_mid, r_lo = pieces(cum_t[h:h + 1, :])
        q_extra = jnp.where(sub < 6, 1.0, 0.0)
        q_extra = jnp.where(sub == 0, r_hi, q_extra)
        q_extra = jnp.where(sub == 1, r_mid, q_extra)
        q_extra = jnp.where(sub == 2, r_lo, q_extra)
        qxT_ref[h * FOX_XD:h * FOX_XD + FOX_HD, :] = (qn * scale).T.astype(BF16)
        qxT_ref[h * FOX_XD + FOX_HD:(h + 1) * FOX_XD, :] = q_extra.astype(BF16)

        vT_ref[h * FOX_HD:(h + 1) * FOX_HD, :] = v_refs[blk][:, half].T.astype(BF16)


def _fox_prep(hz, qn_g, kn_g, fb128, seq):
    M = hz.shape[0]
    tm = 512
    row = pl.BlockSpec((1, LANE), lambda i: (0, 0))
    pieces = lambda start: [_col_spec(tm, COL_TILE, start + c * COL_TILE) for c in range(FOX_W // COL_TILE)]
    return pl.pallas_call(
        functools.partial(_fox_prep_kernel, tm=tm, seq=seq), name="fox_prep",
        out_shape=[jax.ShapeDtypeStruct((M, FOX_H * FOX_XD), BF16),
                   jax.ShapeDtypeStruct((FOX_H * FOX_XD, M), BF16),
                   jax.ShapeDtypeStruct((FOX_W, M), BF16)],
        grid=(M // tm,),
        in_specs=pieces(P_Q) + pieces(P_FK) + pieces(P_FV) + [_col_spec(tm, LANE, P_FF), row, row, row],
        out_specs=[pl.BlockSpec((tm, FOX_H * FOX_XD), lambda i: (i, 0)),
                   pl.BlockSpec((FOX_H * FOX_XD, tm), lambda i: (0, i)),
                   pl.BlockSpec((FOX_W, tm), lambda i: (0, i))],
        scratch_shapes=[pltpu.VMEM((1, LANE), F32)],
        compiler_params=_cparams(("arbitrary",)),
    )(*([hz] * 10), qn_g, kn_g, fb128)


def _fox_attn_kernel(qt_ref, kt_ref, kx_ref, qxT_ref, vT_ref, o_ref, m_scr, l_scr, acc_scr, *, tq, tk):
    step_id = pl.program_id(1)
    qi, ki = qt_ref[step_id], kt_ref[step_id]
    ratio = tq // tk
    first_diag = ratio * qi
    last = ki == first_diag + ratio - 1

    @pl.when(ki == 0)
    def _():
        m_scr[...] = jnp.full_like(m_scr, -1e30)
        l_scr[...] = jnp.zeros_like(l_scr)
        acc_scr[...] = jnp.zeros_like(acc_scr)

    def step(diagonal):
        if diagonal:
            keep = (lax.broadcasted_iota(jnp.int32, (tk, tq), 1) - lax.broadcasted_iota(jnp.int32, (tk, tq), 0)
                    >= ki * tk - qi * tq)
        for h in range(FOX_H):
            xs = slice(h * FOX_XD, (h + 1) * FOX_XD)
            hs = slice(h * FOX_HD, (h + 1) * FOX_HD)
            s = jnp.dot(kx_ref[:, xs], qxT_ref[xs, :], preferred_element_type=F32)
            if diagonal:
                s = jnp.where(keep, s, -1e30)
            m_prev = m_scr[h]
            m_new = jnp.maximum(m_prev, jnp.max(s, axis=0, keepdims=True))
            alpha = jnp.exp(m_prev - m_new)
            p = jnp.exp(s - m_new)
            m_scr[h] = m_new
            l_scr[h] = alpha * l_scr[h] + jnp.sum(p, axis=0, keepdims=True)
            acc_scr[hs, :] = alpha * acc_scr[hs, :] + jnp.dot(vT_ref[hs, :], p.astype(BF16),
                                                              preferred_element_type=F32)

    pl.when(ki < first_diag)(lambda: step(False))
    pl.when(ki >= first_diag)(lambda: step(True))

    @pl.when(last)
    def _():
        for h in range(FOX_H):
            hs = slice(h * FOX_HD, (h + 1) * FOX_HD)
            o_ref[:, hs] = (acc_scr[hs, :] / l_scr[h]).T.astype(o_ref.dtype)


def _fox_attn(kx, qxT, vT, batch, seq):
    M = kx.shape[0]
    tq, tk = min(1024, seq), min(1024, seq)
    nq, nk, ratio = seq // tq, seq // tk, tq // tk
    pairs = [(q, k) for q in range(nq) for k in range(ratio * (q + 1))]
    q_tab = jnp.asarray([q for q, _ in pairs], jnp.int32)
    k_tab = jnp.asarray([k for _, k in pairs], jnp.int32)
    return pl.pallas_call(
        functools.partial(_fox_attn_kernel, tq=tq, tk=tk), name="fox_attn",
        out_shape=jax.ShapeDtypeStruct((M, FOX_W), BF16),
        grid_spec=pltpu.PrefetchScalarGridSpec(
            num_scalar_prefetch=2,
            grid=(batch, len(pairs)),
            in_specs=[
                pl.BlockSpec((tk, FOX_H * FOX_XD), lambda b, s, qt, kt: (b * nk + kt[s], 0)),
                pl.BlockSpec((FOX_H * FOX_XD, tq), lambda b, s, qt, kt: (0, b * nq + qt[s])),
                pl.BlockSpec((FOX_W, tk), lambda b, s, qt, kt: (0, b * nk + kt[s])),
            ],
            out_specs=pl.BlockSpec((tq, FOX_W), lambda b, s, qt, kt: (b * nq + qt[s], 0)),
            scratch_shapes=[pltpu.VMEM((FOX_H, 1, tq), F32), pltpu.VMEM((FOX_H, 1, tq), F32),
                            pltpu.VMEM((FOX_W, tq), F32)],
        ),
        compiler_params=_cparams(("parallel", "arbitrary")),
    )(q_tab, k_tab, kx, qxT, vT)


def _sgu_kernel(u_ref, v_ref, lg_ref, lb_ref, ws_ref, sb_ref, o_ref, *, tm):
    ti = lax.broadcasted_iota(jnp.int32, (SG_CHUNK, SG_CHUNK), 0)
    si = lax.broadcasted_iota(jnp.int32, (SG_CHUNK, SG_CHUNK), 1)
    for g in range(SG_G):
        sl = slice(g * SG_GD, (g + 1) * SG_GD)
        u = _gelu_tanh(u_ref[:, sl])
        v = _gelu_tanh(v_ref[:, sl])
        mu = jnp.mean(v, axis=-1, keepdims=True)
        vc = v - mu
        var = jnp.mean(vc * vc, axis=-1, keepdims=True)
        vn = (vc * lax.rsqrt(var + LN_EPS) * lg_ref[g:g + 1, :] + lb_ref[g:g + 1, :]).astype(BF16)
        ws = jnp.where(ti >= si, ws_ref[g], 0.0).astype(BF16)
        bias = sb_ref[:, g:g + 1]
        for c in range(tm // SG_CHUNK):
            rows = slice(c * SG_CHUNK, (c + 1) * SG_CHUNK)
            mixed = jnp.dot(ws, vn[rows], preferred_element_type=F32) + bias
            o_ref[rows, sl] = (u[rows] * mixed).astype(o_ref.dtype)


def _sgu(hz, ln_g, ln_b, ws, sb_t):
    M = hz.shape[0]
    tm = 512
    return pl.pallas_call(
        functools.partial(_sgu_kernel, tm=tm), name="sgu",
        out_shape=jax.ShapeDtypeStruct((M, SG_W), BF16),
        grid=(M // tm,),
        in_specs=[
            _col_spec(tm, SG_W, P_U), _col_spec(tm, SG_W, P_SV),
            pl.BlockSpec((SG_G, SG_GD), lambda i: (0, 0)),
            pl.BlockSpec((SG_G, SG_GD), lambda i: (0, 0)),
            pl.BlockSpec((SG_G, SG_CHUNK, SG_CHUNK), lambda i: (0, 0, 0)),
            pl.BlockSpec((SG_CHUNK, SG_G), lambda i: (0, 0)),
        ],
        out_specs=pl.BlockSpec((tm, SG_W), lambda i: (i, 0)),
        compiler_params=_cparams(("parallel",)),
    )(hz, hz, ln_g, ln_b, ws, sb_t)


def _outproj_kernel(oa_ref, ob_ref, oc_ref, w_ref, x_ref, g1_ref, ng_ref, sc_ref, sh_ref,
                    xo_ref, h_ref):
    mix = jnp.dot(oa_ref[...], w_ref[0:RW_W, :], preferred_element_type=F32)
    mix += jnp.dot(ob_ref[...], w_ref[RW_W:RW_W + FOX_W, :], preferred_element_type=F32)
    mix += jnp.dot(oc_ref[...], w_ref[RW_W + FOX_W:, :], preferred_element_type=F32)
    x = x_ref[...] + g1_ref[0] * mix
    xo_ref[...] = x
    ms = jnp.mean(x * x, axis=-1, keepdims=True)
    y = x * lax.rsqrt(ms + RMS_EPS) * ng_ref[...]
    h_ref[...] = (y * (1.0 + sc_ref[0]) + sh_ref[0]).astype(h_ref.dtype)


def _outproj(oa, ob, oc, w, x2, g1, ng, sc, sh, seq, h_dtype):
    M, D = x2.shape
    tm = 512
    bidx = lambda i: (i * tm // seq, 0, 0)
    mod = pl.BlockSpec((1, 1, D), bidx)
    return pl.pallas_call(
        _outproj_kernel, name="outproj",
        out_shape=[jax.ShapeDtypeStruct((M, D), F32), jax.ShapeDtypeStruct((M, D), h_dtype)],
        grid=(M // tm,),
        in_specs=[
            pl.BlockSpec((tm, RW_W), lambda i: (i, 0)),
            pl.BlockSpec((tm, FOX_W), lambda i: (i, 0)),
            pl.BlockSpec((tm, SG_W), lambda i: (i, 0)),
            pl.BlockSpec((D, D), lambda i: (0, 0)),
            pl.BlockSpec((tm, D), lambda i: (i, 0)),
            mod,
            pl.BlockSpec((1, D), lambda i: (0, 0)),
            mod, mod,
        ],
        out_specs=[pl.BlockSpec((tm, D), lambda i: (i, 0)), pl.BlockSpec((tm, D), lambda i: (i, 0))],
        compiler_params=_cparams(("parallel",)),
    )(oa, ob, oc, w, x2, g1, ng, sc, sh)


def _swiglu_accumulate(h_ref, w1_ref, w3_ref, w2_ref, acc_ref, rows):
    h = h_ref[0:rows, :]
    a = jnp.dot(h, w1_ref[0].astype(BF16), preferred_element_type=F32)
    b = jnp.dot(h, w3_ref[0].astype(BF16), preferred_element_type=F32)
    acc_ref[0:rows, :] += jnp.dot((_silu(a) * b).astype(BF16), w2_ref[0].astype(BF16),
                                  preferred_element_type=F32)


def _ffn_kernel(h_ref, w1_ref, w3_ref, w2_ref, x_ref, g2_ref, o_ref):
    j = pl.program_id(1)

    @pl.when(j == 0)
    def _():
        o_ref[...] = jnp.zeros_like(o_ref)

    _swiglu_accumulate(h_ref, w1_ref, w3_ref, w2_ref, o_ref, h_ref.shape[0])

    @pl.when(j == pl.num_programs(1) - 1)
    def _():
        o_ref[...] = x_ref[...] + g2_ref[0] * o_ref[...]


def _ffn(h2, w1, w3, w2, layer, x2, g2, seq):
    M, D = x2.shape
    F = w1.shape[2]
    tm, tf = 1024, 512
    once = dict(pipeline_mode=pl.Buffered(1))
    mod = pl.BlockSpec((1, 1, D), lambda i, j: (i * tm // seq, 0, 0))
    return pl.pallas_call(
        _ffn_kernel, name="ffn",
        out_shape=jax.ShapeDtypeStruct((M, D), F32),
        grid=(M // tm, F // tf),
        in_specs=[
            pl.BlockSpec((tm, D), lambda i, j: (i, 0), **once),
            pl.BlockSpec((1, D, tf), lambda i, j: (layer, 0, j)),
            pl.BlockSpec((1, D, tf), lambda i, j: (layer, 0, j)),
            pl.BlockSpec((1, tf, D), lambda i, j: (layer, j, 0)),
            pl.BlockSpec((tm, D), lambda i, j: (i, 0), **once),
            mod,
        ],
        out_specs=pl.BlockSpec((tm, D), lambda i, j: (i, 0), **once),
        compiler_params=_cparams(("parallel", "arbitrary")),
    )(h2, w1, w3, w2, x2, g2)


def _router_kernel(h_ref, w_ref, b_ref, info_ref, cnt_ref, carry, *, tm):
    @pl.when(pl.program_id(0) == 0)
    def _():
        carry[...] = jnp.zeros_like(carry)

    logits = jnp.dot(h_ref[...].astype(BF16), w_ref[...], preferred_element_type=F32) + b_ref[...]
    lane = lax.broadcasted_iota(jnp.int32, logits.shape, 1)
    neg = -1e30
    logits = jnp.where(lane < N_EXPERTS, logits, neg)
    m1 = jnp.max(logits, axis=-1, keepdims=True)
    i1 = jnp.min(jnp.where(logits == m1, lane, LANE), axis=-1, keepdims=True)
    rest = jnp.where(lane == i1, neg, logits)
    m2 = jnp.max(rest, axis=-1, keepdims=True)
    i2 = jnp.min(jnp.where(rest == m2, lane, LANE), axis=-1, keepdims=True)
    e2 = jnp.exp(m2 - m1)
    p1 = 1.0 / (1.0 + e2)
    p2 = e2 / (1.0 + e2)

    oh1 = (lane == i1).astype(F32)
    oh2 = (lane == i2).astype(F32)
    both = oh1 + oh2
    ti = lax.broadcasted_iota(jnp.int32, (tm, tm), 0)
    si = lax.broadcasted_iota(jnp.int32, (tm, tm), 1)
    strict = (ti > si).astype(BF16)
    before = jnp.dot(strict, both.astype(BF16), preferred_element_type=F32) + carry[...]
    rank1 = jnp.sum(oh1 * before, axis=-1, keepdims=True)
    rank2 = jnp.sum(oh2 * before, axis=-1, keepdims=True)
    total = carry[...] + jnp.sum(both, axis=0, keepdims=True)
    carry[...] = total
    cnt_ref[...] = total

    info = jnp.where(lane == 0, i1.astype(F32), 0.0)
    info = jnp.where(lane == 1, i2.astype(F32), info)
    info = jnp.where(lane == 2, rank1, info)
    info = jnp.where(lane == 3, rank2, info)
    info = jnp.where(lane == 4, p1, info)
    info = jnp.where(lane == 5, p2, info)
    info_ref[...] = info


def _router(h2, rw, rb):
    M, D = h2.shape
    tm = 512
    return pl.pallas_call(
        functools.partial(_router_kernel, tm=tm), name="router",
        out_shape=[jax.ShapeDtypeStruct((M, LANE), F32), jax.ShapeDtypeStruct((1, LANE), F32)],
        grid=(M // tm,),
        in_specs=[
            pl.BlockSpec((tm, D), lambda i: (i, 0)),
            pl.BlockSpec((D, LANE), lambda i: (0, 0)),
            pl.BlockSpec((1, LANE), lambda i: (0, 0)),
        ],
        out_specs=[pl.BlockSpec((tm, LANE), lambda i: (i, 0)), pl.BlockSpec((1, LANE), lambda i: (0, 0))],
        scratch_shapes=[pltpu.VMEM((1, LANE), F32)],
        compiler_params=_cparams(("arbitrary",)),
    )(h2, rw, rb)


MOE_TILE = 1024


def _moe_dispatch_kernel(s1_ref, s2_ref, h_ref, xs_in_ref, xs_ref, sem, *, tm):
    del xs_in_ref
    base = pl.program_id(0) * tm

    def copies(r):
        src = h_ref.at[pl.ds(r, 1)]
        return (pltpu.make_async_copy(src, xs_ref.at[pl.ds(s1_ref[base + r], 1)], sem.at[0]),
                pltpu.make_async_copy(src, xs_ref.at[pl.ds(s2_ref[base + r], 1)], sem.at[1]))

    def start(r, carry):
        for queue, cp in enumerate(copies(r)):
            cp.start(priority=queue)
        return carry

    def wait(r, carry):
        for cp in copies(r):
            cp.wait()
        return carry

    lax.fori_loop(0, tm, start, 0, unroll=8)
    lax.fori_loop(0, tm, wait, 0, unroll=8)


def _moe_dispatch(slot1, slot2, h2, n_rows):
    M, D = h2.shape
    tm = 256
    xs0 = jnp.zeros((n_rows, D), F32)
    return pl.pallas_call(
        functools.partial(_moe_dispatch_kernel, tm=tm), name="moe_dispatch",
        out_shape=jax.ShapeDtypeStruct((n_rows, D), F32),
        grid_spec=pltpu.PrefetchScalarGridSpec(
            num_scalar_prefetch=2,
            grid=(M // tm,),
            in_specs=[pl.BlockSpec((tm, D), lambda i, s1, s2: (i, 0)),
                      pl.BlockSpec(memory_space=pl.ANY)],
            out_specs=pl.BlockSpec(memory_space=pl.ANY),
            scratch_shapes=[pltpu.SemaphoreType.DMA((2,))],
        ),
        input_output_aliases={3: 0},
        compiler_params=_cparams(("arbitrary",)),
    )(slot1, slot2, h2, xs0)


def _moe_expert_kernel(te_ref, nv_ref, xs_ref, w1_ref, w3_ref, w2_ref, ys_ref, xb):
    j = pl.program_id(1)
    nv = nv_ref[pl.program_id(0)]
    half = xb.shape[0] // 2

    @pl.when(j == 0)
    def _():
        ys_ref[...] = jnp.zeros_like(ys_ref)
        xb[...] = xs_ref[...].astype(BF16)

    @pl.when(nv > half)
    def _():
        _swiglu_accumulate(xb, w1_ref, w3_ref, w2_ref, ys_ref, 2 * half)

    @pl.when((nv > 0) & (nv <= half))
    def _():
        _swiglu_accumulate(xb, w1_ref, w3_ref, w2_ref, ys_ref, half)


def _moe_experts(tile_expert, n_valid, xs, w1, w3, w2):
    P, D = xs.shape
    E, _, F = w1.shape
    tm, tf = MOE_TILE, 256
    nf = F // tf
    fj = lambda i, j, nv: jnp.where(nv[i] > 0, j, nf - 1)
    return pl.pallas_call(
        _moe_expert_kernel, name="moe_experts",
        out_shape=jax.ShapeDtypeStruct((P, D), F32),
        grid_spec=pltpu.PrefetchScalarGridSpec(
            num_scalar_prefetch=2,
            grid=(P // tm, nf),
            in_specs=[
                pl.BlockSpec((tm, D), lambda i, j, te, nv: (i, 0)),
                pl.BlockSpec((1, D, tf), lambda i, j, te, nv: (te[i], 0, fj(i, j, nv))),
                pl.BlockSpec((1, D, tf), lambda i, j, te, nv: (te[i], 0, fj(i, j, nv))),
                pl.BlockSpec((1, tf, D), lambda i, j, te, nv: (te[i], fj(i, j, nv), 0)),
            ],
            out_specs=pl.BlockSpec((tm, D), lambda i, j, te, nv: (i, 0)),
            scratch_shapes=[pltpu.VMEM((tm, D), BF16)],
        ),
        compiler_params=_cparams(("arbitrary", "arbitrary")),
    )(tile_expert, n_valid, xs, w1, w3, w2)


def _moe_combine_kernel(s1_ref, s2_ref, ys_ref, info_ref, x_ref, g2_ref, o_ref, a_buf, b_buf, sem, *, tm):
    base = pl.program_id(0) * tm

    def copies(r):
        return (pltpu.make_async_copy(ys_ref.at[pl.ds(s1_ref[base + r], 1)], a_buf.at[pl.ds(r, 1)], sem.at[0]),
                pltpu.make_async_copy(ys_ref.at[pl.ds(s2_ref[base + r], 1)], b_buf.at[pl.ds(r, 1)], sem.at[1]))

    def start(r, carry):
        for queue, cp in enumerate(copies(r)):
            cp.start(priority=queue)
        return carry

    def wait(r, carry):
        for cp in copies(r):
            cp.wait()
        return carry

    lax.fori_loop(0, tm, start, 0, unroll=8)
    lax.fori_loop(0, tm, wait, 0, unroll=8)
    info = info_ref[...]
    p1 = info[:, 4:5]
    p2 = info[:, 5:6]
    o_ref[...] = x_ref[...] + g2_ref[0] * (p1 * a_buf[...] + p2 * b_buf[...])


def _moe_combine(slot1, slot2, ys, info, x2, g2, seq):
    M, D = x2.shape
    tm = 256
    return pl.pallas_call(
        functools.partial(_moe_combine_kernel, tm=tm), name="moe_combine",
        out_shape=jax.ShapeDtypeStruct((M, D), F32),
        grid_spec=pltpu.PrefetchScalarGridSpec(
            num_scalar_prefetch=2,
            grid=(M // tm,),
            in_specs=[
                pl.BlockSpec(memory_space=pl.ANY),
                pl.BlockSpec((tm, LANE), lambda i, s1, s2: (i, 0)),
                pl.BlockSpec((tm, D), lambda i, s1, s2: (i, 0)),
                pl.BlockSpec((1, 1, D), lambda i, s1, s2: (i * tm // seq, 0, 0)),
            ],
            out_specs=pl.BlockSpec((tm, D), lambda i, s1, s2: (i, 0)),
            scratch_shapes=[pltpu.VMEM((tm, D), F32), pltpu.VMEM((tm, D), F32),
                            pltpu.SemaphoreType.DMA((2,))],
        ),
        compiler_params=_cparams(("arbitrary",)),
    )(slot1, slot2, ys, info, x2, g2)


def _moe(h2, rw, rb, w1, w3, w2, x2, g2, seq):
    M, D = x2.shape
    E = w1.shape[0]
    T = MOE_TILE
    n_rows = 2 * M + E * T
    n_tiles = n_rows // T
    info, counts = _router(h2, rw, rb)

    e1, e2, rank1, rank2 = (info[:, c].astype(jnp.int32) for c in range(4))
    cnt = counts[0, :E].astype(jnp.int32)
    padded = (cnt + T - 1) // T * T
    ends = jnp.cumsum(padded)
    off = ends - padded
    expert_ids = jnp.arange(E, dtype=jnp.int32)
    offset_of = lambda e: jnp.sum(jnp.where(e[:, None] == expert_ids[None, :], off[None, :], 0), axis=1)
    slot1 = offset_of(e1) + rank1
    slot2 = offset_of(e2) + rank2
    tile_start = jnp.arange(n_tiles, dtype=jnp.int32) * T
    in_use = tile_start < ends[E - 1]
    clamped = jnp.minimum(tile_start, ends[E - 1] - T)
    tile_expert = jnp.sum(clamped[:, None] >= ends[None, :], axis=1).astype(jnp.int32)
    is_e = tile_expert[:, None] == expert_ids[None, :]
    tokens_end = jnp.sum(jnp.where(is_e, (off + cnt)[None, :], 0), axis=1)
    n_valid = jnp.where(in_use, jnp.clip(tokens_end - tile_start, 0, T), 0).astype(jnp.int32)

    xs = _moe_dispatch(slot1, slot2, h2, n_rows)
    ys = _moe_experts(tile_expert, n_valid, xs, w1, w3, w2)
    return _moe_combine(slot1, slot2, ys, info, x2, g2, seq)


def _mixing_layer(x2, mods, p, w_t, layer, batch, seq, h_dtype):
    row = lambda t: t.reshape(1, -1)
    hz = _inproj(x2, row(p["norm1_g"]), mods["sc1"], mods["sh1"], w_t, layer, seq)

    mu = p["shift_mu"]
    mu3 = mu[:3 * RW_W].reshape(3, RW_W)
    mul = row(mu[3 * RW_W:])
    hm = lambda t: t.reshape(RW_H, 1, RW_HD)
    o_a = _rwkv(hz, mu3, mul, row(p["rw_w0"]), p["rw_w2"].astype(BF16), row(p["rw_a0"]),
                p["rw_a2"].astype(BF16), p["rw_g2"].astype(BF16), row(p["rw_k_k"]), row(p["rw_k_a"]),
                hm(p["rw_r_k"]), hm(p["rw_lnx_g"]), hm(p["rw_lnx_b"]), batch, seq)

    fb128 = jnp.zeros((1, LANE), F32).at[0, :FOX_H].set(p["fox_fb"])
    kx, qxT, vT = _fox_prep(hz, row(p["fox_qn_g"]), row(p["fox_kn_g"]), fb128, seq)
    o_b = _fox_attn(kx, qxT, vT, batch, seq)

    o_c = _sgu(hz, p["sg_ln_g"], p["sg_ln_b"], p["sg_ws"], jnp.transpose(p["sg_b"]))

    return _outproj(o_a, o_b, o_c, p["w_out"].astype(BF16), x2, mods["g1"], row(p["norm2_g"]),
                    mods["sc2"], mods["sh2"], seq, h_dtype)


def kernel(x, c, ada_w, ada_b, norm1_g, norm2_g, w_in, shift_mu, rw_w0, rw_w2, rw_a0, rw_a2, rw_g2, rw_k_k, rw_k_a, rw_r_k, rw_lnx_g, rw_lnx_b, fox_qn_g, fox_kn_g, fox_fb, sg_ln_g, sg_ln_b, sg_ws, sg_b, w_out, ffn_w1, ffn_w3, ffn_w2, moe_router_w, moe_router_b, moe_w1, moe_w3, moe_w2):
    B, S, D = x.shape
    L = ada_w.shape[0]
    x2 = x.reshape(B * S, D)
    c8 = jnp.zeros((8, D), F32).at[:B].set(c)
    mod = _ada_mod(c8, ada_w, ada_b.reshape(L, 1, N_MOD * D))

    layer_params = dict(
        norm1_g=norm1_g, norm2_g=norm2_g, shift_mu=shift_mu, rw_w0=rw_w0, rw_w2=rw_w2,
        rw_a0=rw_a0, rw_a2=rw_a2, rw_g2=rw_g2, rw_k_k=rw_k_k, rw_k_a=rw_k_a, rw_r_k=rw_r_k,
        rw_lnx_g=rw_lnx_g, rw_lnx_b=rw_lnx_b, fox_qn_g=fox_qn_g, fox_kn_g=fox_kn_g, fox_fb=fox_fb,
        sg_ln_g=sg_ln_g, sg_ln_b=sg_ln_b, sg_ws=sg_ws, sg_b=sg_b, w_out=w_out)

    w_t = jnp.transpose(w_in, (2, 0, 1))
    for l in range(L):
        names = ("sh1", "sc1", "g1", "sh2", "sc2", "g2")
        mods = {n: mod[l, :B, i * D:(i + 1) * D].reshape(B, 1, D) for i, n in enumerate(names)}
        p = {n: t[l] for n, t in layer_params.items()}
        dense = l % 2 == 0
        x2, h2 = _mixing_layer(x2, mods, p, w_t, l, B, S, BF16 if dense else F32)
        j = l // 2
        if dense:
            x2 = _ffn(h2, ffn_w1, ffn_w3, ffn_w2, j, x2, mods["g2"], S)
        else:
            rw = jnp.zeros((D, LANE), BF16).at[:, :N_EXPERTS].set(moe_router_w[j].astype(BF16))
            rb = jnp.zeros((1, LANE), F32).at[0, :N_EXPERTS].set(moe_router_b[j])
            x2 = _moe(h2, rw, rb, moe_w1[j], moe_w3[j], moe_w2[j], x2, mods["g2"], S)
    return x2.reshape(B, S, D)
```
